```python
import math
import jax, jax.numpy as jnp
from jax import lax
import numpy as np

D_MODEL = 2048
BATCH = 4
SEQ = 2048
DEPTH = 2
DEC_BATCH = 128
DEC_SEQ = 8
PAST_LEN = 16384
PAGE_SIZE = 128

P_DIM = 256
CONV_W = 4
CHUNK = 64
N_BRANCH = 3

ML_HEADS = 4
ML_DK = 128
ML_DV = 256
ML_GATE_CAP = 15.0

LRU_WIDTH = 1024
LRU_BLOCKS = 4
LRU_BLOCK = LRU_WIDTH // LRU_BLOCKS
LRU_C = 8.0

GDN_HEADS = 4
GDN_DK = 128
GDN_DV = 256
GDN_QK = GDN_HEADS * GDN_DK
GDN_CONV_DIM = 2 * GDN_QK + GDN_HEADS * GDN_DV

D_FF = 5632
N_EXPERTS = 8
TOP_K = 2
D_FF_EXPERT = 2816
N_DENSE = (DEPTH + 1) // 2
N_MOE = DEPTH // 2

DEEPNORM_ALPHA = (2 * DEPTH) ** 0.25
DEEPNORM_BETA = (8 * DEPTH) ** -0.25
LN_EPS = 1e-5
RMS_EPS = 1e-6
L2_EPS = 1e-6

IN_SPLITS = (
    ML_HEADS * ML_DK, ML_HEADS * ML_DK, ML_HEADS * ML_DV, ML_HEADS, ML_HEADS, ML_HEADS * ML_DV,
    LRU_WIDTH, LRU_WIDTH,
    GDN_CONV_DIM, GDN_HEADS, GDN_HEADS, GDN_HEADS * GDN_DV,
    N_BRANCH * D_MODEL,
)
D_IN = sum(IN_SPLITS)

kernel_name = 'hybrid_mlstm_rglru_gdn_decoder_step'

F32 = jnp.float32


def _layer_norm(x, g, b):
    xf = x.astype(F32)
    mu = jnp.mean(xf, -1, keepdims=True)
    var = jnp.mean(jnp.square(xf - mu), -1, keepdims=True)
    return ((xf - mu) * lax.rsqrt(var + LN_EPS) * g.astype(F32) + b.astype(F32)).astype(x.dtype)


def _rms_norm(x, w):
    xf = x.astype(F32)
    return xf * lax.rsqrt(jnp.mean(jnp.square(xf), -1, keepdims=True) + RMS_EPS) * w.astype(F32)


def _l2_normalize(x):
    return x * lax.rsqrt(jnp.sum(jnp.square(x), -1, keepdims=True) + L2_EPS)


def _causal_conv(x, buf, w):
    t = x.shape[1]
    xx = jnp.concatenate([buf.astype(x.dtype), x], axis=1)
    y = xx[:, 0:t] * w[0]
    for j in range(1, CONV_W):
        y = y + xx[:, j:j + t] * w[j]
    return y, xx[:, -(CONV_W - 1):]


def _chunk_len(t):
    return max(d for d in range(1, min(CHUNK, t) + 1) if t % d == 0)


def _to_chunks(a, nc, L):
    b = a.shape[0]
    a = a.reshape((b, nc, L) + a.shape[2:])
    return jnp.moveaxis(jnp.moveaxis(a, 3, 2), 1, 0)


def _from_chunks(a):
    nc, b, h, L = a.shape[:4]
    a = jnp.moveaxis(jnp.moveaxis(a, 0, 1), 3, 2)
    return a.reshape((b, nc * L, h) + a.shape[4:])


def _mlstm(q, k, v, i_pre, logf, c0, n0, m0):
    b, t = q.shape[:2]
    L = _chunk_len(t)
    nc = t // L
    q = q * ML_DK ** -0.5
    xs = tuple(_to_chunks(a, nc, L) for a in (q, k, v, i_pre, logf))
    causal = jnp.tril(jnp.ones((L, L), dtype=bool))

    def step(carry, xc):
        c, n, m = carry
        qc, kc, vc, ic, fc = xc
        bcum = jnp.cumsum(fc, axis=-1)
        m_t = bcum + jnp.maximum(m[..., None], lax.cummax(ic - bcum, axis=2))
        log_d = bcum[..., :, None] - bcum[..., None, :] + ic[..., None, :] - m_t[..., :, None]
        dmat = jnp.exp(jnp.where(causal, log_d, -jnp.inf))
        s = jnp.einsum('bhtd,bhsd->bhts', qc, kc) * dmat
        inter = jnp.exp(bcum + m[..., None] - m_t)
        num = jnp.einsum('bhts,bhse->bhte', s, vc) + inter[..., None] * jnp.einsum('bhed,bhtd->bhte', c, qc)
        den = jnp.sum(s, -1) + inter * jnp.einsum('bhd,bhtd->bht', n, qc)
        h = num / jnp.maximum(jnp.abs(den), jnp.exp(-m_t))[..., None]
        m_new = m_t[..., -1]
        w_end = jnp.exp(bcum[..., -1:] - bcum + ic - m_new[..., None])
        dec = jnp.exp(bcum[..., -1] + m - m_new)
        c_new = dec[..., None, None] * c + jnp.einsum('bhs,bhse,bhsd->bhed', w_end, vc, kc)
        n_new = dec[..., None] * n + jnp.einsum('bhs,bhsd->bhd', w_end, kc)
        return (c_new, n_new, m_new), h

    (c, n, m), h = lax.scan(step, (c0.astype(F32), n0.astype(F32), m0.astype(F32)), xs)
    return _from_chunks(h), c, n, m


def _rglru(xc, h0, w_a, b_a, w_x, b_x, lam):
    b, t, _ = xc.shape
    xb = xc.reshape(b, t, LRU_BLOCKS, LRU_BLOCK)
    r = jax.nn.sigmoid(jnp.einsum('btnd,nde->btne', xb, w_a.astype(F32)).reshape(b, t, LRU_WIDTH) + b_a.astype(F32))
    i = jax.nn.sigmoid(jnp.einsum('btnd,nde->btne', xb, w_x.astype(F32)).reshape(b, t, LRU_WIDTH) + b_x.astype(F32))
    log_a = -LRU_C * r * jax.nn.softplus(-lam.astype(F32))
    a = jnp.exp(log_a)
    u = jnp.sqrt(-jnp.expm1(2.0 * log_a)) * (i * xc)
    u = u.at[:, 0].add(a[:, 0] * h0.astype(F32))

    def comb(left, right):
        a1, b1 = left
        a2, b2 = right
        return a1 * a2, a2 * b1 + b2

    _, h = lax.associative_scan(comb, (a, u), axis=1)
    return h, h[:, -1]


def _gated_delta(q, k, v, beta, g, s0):
    b, t = q.shape[:2]
    L = _chunk_len(t)
    nc = t // L
    q = q * GDN_DK ** -0.5
    qc, kc, vc = (_to_chunks(a, nc, L) for a in (q, k, v))
    bc = _to_chunks(beta, nc, L)
    gcum = jnp.cumsum(_to_chunks(g, nc, L), axis=-1)
    causal = jnp.tril(jnp.ones((L, L), dtype=bool))
    strict = jnp.tril(jnp.ones((L, L), dtype=bool), -1)
    decay = jnp.exp(jnp.where(causal, gcum[..., :, None] - gcum[..., None, :], -jnp.inf))
    kk = jnp.einsum('cbhtd,cbhsd->cbhts', kc, kc)
    a_mat = jnp.where(strict, bc[..., None] * kk * decay, 0.0) + jnp.eye(L, dtype=F32)
    rhs = jnp.concatenate([bc[..., None] * vc, (bc * jnp.exp(gcum))[..., None] * kc], axis=-1)
    sol = lax.linalg.triangular_solve(a_mat, rhs, left_side=True, lower=True)
    u_pre, w = sol[..., :GDN_DV], sol[..., GDN_DV:]
    qk = jnp.einsum('cbhtd,cbhsd->cbhts', qc, kc) * decay
    q_dec = qc * jnp.exp(gcum)[..., None]
    g_end = gcum[..., -1]
    k_dec = kc * jnp.exp(g_end[..., None] - gcum)[..., None]

    def step(s, xc):
        u_c, w_c, qk_c, qd_c, kd_c, ge_c = xc
        u = u_c - jnp.einsum('bhtd,bhde->bhte', w_c, s)
        o = jnp.einsum('bhtd,bhde->bhte', qd_c, s) + jnp.einsum('bhts,bhse->bhte', qk_c, u)
        s_new = jnp.exp(ge_c)[..., None, None] * s + jnp.einsum('bhsd,bhse->bhde', kd_c, u)
        return s_new, o

    s_fin, o = lax.scan(step, s0.astype(F32), (u_pre, w, qk, q_dec, k_dec, g_end))
    return _from_chunks(o), s_fin


def _split_cols(h):
    idx = np.cumsum(np.array(IN_SPLITS))[:-1].tolist()
    return jnp.split(h, idx, axis=-1)


def _mixer(x, st, l, P):
    b, t, _ = x.shape
    (ml_q, ml_k, ml_v, ml_i, ml_f, ml_o, rg_x, rg_y,
     gd_qkv, gd_b, gd_a, gd_z, mg) = _split_cols(x @ P['w_in'][l])

    cap = lambda z: ML_GATE_CAP * jnp.tanh(z / ML_GATE_CAP)
    i_pre = cap(ml_i.astype(F32) + P['ml_b_i'][l].astype(F32))
    logf = jax.nn.log_sigmoid(cap(ml_f.astype(F32) + P['ml_b_f'][l].astype(F32)))
    h_a, ml_c, ml_n, ml_m = _mlstm(
        ml_q.astype(F32).reshape(b, t, ML_HEADS, ML_DK),
        ml_k.astype(F32).reshape(b, t, ML_HEADS, ML_DK),
        ml_v.astype(F32).reshape(b, t, ML_HEADS, ML_DV),
        i_pre, logf, st['ml_C'], st['ml_n'], st['ml_m'])
    h_a = _rms_norm(h_a, P['ml_norm_w'][l].reshape(ML_HEADS, ML_DV)).reshape(b, t, ML_HEADS * ML_DV)
    y_a = (jax.nn.sigmoid(ml_o.astype(F32)) * h_a).astype(x.dtype)

    xc, rg_conv = _causal_conv(rg_x, st['rg_conv'], P['rg_conv_w'][l])
    xc = xc.astype(F32) + P['rg_conv_b'][l].astype(F32)
    h_b, rg_h = _rglru(xc, st['rg_h'], P['rg_w_a'][l], P['rg_b_a'][l], P['rg_w_x'][l], P['rg_b_x'][l],
                       P['rg_lambda'][l])
    y_b = (jax.nn.gelu(rg_y.astype(F32)) * h_b).astype(x.dtype)

    qkv, gd_conv = _causal_conv(gd_qkv, st['gd_conv'], P['gd_conv_w'][l])
    qkv = jax.nn.silu(qkv.astype(F32))
    g_q = _l2_normalize(qkv[..., :GDN_QK].reshape(b, t, GDN_HEADS, GDN_DK))
    g_k = _l2_normalize(qkv[..., GDN_QK:2 * GDN_QK].reshape(b, t, GDN_HEADS, GDN_DK))
    g_v = qkv[..., 2 * GDN_QK:].reshape(b, t, GDN_HEADS, GDN_DV)
    beta = jax.nn.sigmoid(gd_b.astype(F32))
    g = -jnp.exp(P['gd_A_log'][l].astype(F32)) * jax.nn.softplus(gd_a.astype(F32) + P['gd_dt_bias'][l].astype(F32))
    h_c, gd_s = _gated_delta(g_q, g_k, g_v, beta, g, st['gd_S'])
    h_c = _rms_norm(h_c, P['gd_norm_w'][l]) * jax.nn.silu(gd_z.astype(F32).reshape(b, t, GDN_HEADS, GDN_DV))
    y_c = h_c.reshape(b, t, GDN_HEADS * GDN_DV).astype(x.dtype)

    g_a, g_b, g_c = jnp.split(jax.nn.sigmoid(mg), N_BRANCH, axis=-1)
    merged = (g_a * (y_a @ P['w_up_mlstm'][l]) + g_b * (y_b @ P['w_up_rglru'][l])
              + g_c * (y_c @ P['w_up_gdn'][l]))
    out = merged @ P['w_out'][l]
    new_st = dict(ml_C=ml_c, ml_n=ml_n, ml_m=ml_m, rg_h=rg_h, rg_conv=rg_conv, gd_S=gd_s, gd_conv=gd_conv)
    return out, new_st


def _swiglu(x, w1, w3, w2):
    return (jax.nn.silu(x @ w1) * (x @ w3)) @ w2


def _moe(x, router, w1, w3, w2):
    b, t, d = x.shape
    xt = x.reshape(b * t, d)
    logits = (xt @ router).astype(F32)
    top_v, top_i = lax.top_k(logits, TOP_K)
    top_w = jax.nn.softmax(top_v, axis=-1)
    gates = jnp.sum(top_w[..., None] * jax.nn.one_hot(top_i, N_EXPERTS, dtype=F32), axis=1)
    out = jnp.zeros((b * t, d), F32)
    for e in range(N_EXPERTS):
        out = out + gates[:, e:e + 1] * _swiglu(xt, w1[e], w3[e], w2[e]).astype(F32)
    return out.astype(x.dtype).reshape(b, t, d)


def _run_trunk(x, p, states, P):
    names = ('ml_C', 'ml_n', 'ml_m', 'rg_h', 'rg_conv', 'gd_S', 'gd_conv')
    new = {k: [] for k in names}
    for l in range(DEPTH):
        st = {k: states[k][l] for k in names}
        mix, st_new = _mixer(x, st, l, P)
        x = _layer_norm(DEEPNORM_ALPHA * x + mix, P['ln1_g'][l], P['ln1_b'][l])
        if l % 2 == 0:
            j = l // 2
            f = _swiglu(x, P['ffn_w1'][j], P['ffn_w3'][j], P['ffn_w2'][j])
        else:
            j = l // 2
            f = _moe(x, P['moe_router'][j], P['moe_w1'][j], P['moe_w3'][j], P['moe_w2'][j])
        ple = jax.nn.sigmoid(x @ P['ple_gate_w'][l]) * (p[l].astype(x.dtype) @ P['ple_w'][l])
        x = _layer_norm(DEEPNORM_ALPHA * x + f + ple, P['ln2_g'][l], P['ln2_b'][l])
        for k in names:
            new[k].append(st_new[k])
    return x, {k: jnp.stack(v) for k, v in new.items()}


def setup_inputs(seed: int = 0) -> dict:
    key = jax.random.key(seed)
    ks = iter(jax.random.split(key, 64))

    def nrm(shape, scale=1.0):
        return scale * jax.random.normal(next(ks), shape, F32)

    x_prompt = nrm((BATCH, SEQ, D_MODEL))
    x_sample = nrm((DEC_BATCH, DEC_SEQ, D_MODEL))
    state_mlstm_C = nrm((DEPTH, DEC_BATCH, ML_HEADS, ML_DV, ML_DK), 0.3)
    state_mlstm_n = nrm((DEPTH, DEC_BATCH, ML_HEADS, ML_DK), 0.3)
    state_mlstm_m = nrm((DEPTH, DEC_BATCH, ML_HEADS), 1.0)
    state_rglru_h = nrm((DEPTH, DEC_BATCH, LRU_WIDTH), 0.5)
    state_rglru_conv = nrm((DEPTH, DEC_BATCH, CONV_W - 1, LRU_WIDTH))
    state_gdn_S = nrm((DEPTH, DEC_BATCH, GDN_HEADS, GDN_DK, GDN_DV), GDN_DK ** -0.5)
    state_gdn_conv = nrm((DEPTH, DEC_BATCH, CONV_W - 1, GDN_CONV_DIM))
    p_prompt = nrm((DEPTH, BATCH, SEQ, P_DIM))
    p_sample = nrm((DEPTH, DEC_BATCH, DEC_SEQ, P_DIM))

    w_in = nrm((DEPTH, D_MODEL, D_IN), D_MODEL ** -0.5)
    ml_b_i = nrm((DEPTH, ML_HEADS), 0.1)
    ml_b_f = 3.0 + nrm((DEPTH, ML_HEADS), 0.5)
    ml_norm_w = 1.0 + nrm((DEPTH, ML_HEADS * ML_DV), 0.02)
    rg_conv_w = nrm((DEPTH, CONV_W, LRU_WIDTH), CONV_W ** -0.5)
    rg_conv_b = nrm((DEPTH, LRU_WIDTH), 0.02)
    rg_w_a = nrm((DEPTH, LRU_BLOCKS, LRU_BLOCK, LRU_BLOCK), LRU_BLOCK ** -0.5)
    rg_b_a = nrm((DEPTH, LRU_WIDTH), 0.02)
    rg_w_x = nrm((DEPTH, LRU_BLOCKS, LRU_BLOCK, LRU_BLOCK), LRU_BLOCK ** -0.5)
    rg_b_x = nrm((DEPTH, LRU_WIDTH), 0.02)
    a_target = jax.random.uniform(next(ks), (DEPTH, LRU_WIDTH), F32, 0.9, 0.999)
    s_lam = a_target ** (1.0 / LRU_C)
    rg_lambda = jnp.log(s_lam) - jnp.log1p(-s_lam)
    gd_conv_w = nrm((DEPTH, CONV_W, GDN_CONV_DIM), CONV_W ** -0.5)
    gd_A_log = jnp.log(jax.random.uniform(next(ks), (DEPTH, GDN_HEADS), F32, 1.0, 16.0))
    dt = jnp.exp(jax.random.uniform(next(ks), (DEPTH, GDN_HEADS), F32, math.log(1e-3), math.log(1e-1)))
    gd_dt_bias = dt + jnp.log(-jnp.expm1(-dt))
    gd_norm_w = 1.0 + nrm((DEPTH, GDN_DV), 0.02)
    w_up_mlstm = nrm((DEPTH, ML_HEADS * ML_DV, D_MODEL), (ML_HEADS * ML_DV) ** -0.5)
    w_up_rglru = nrm((DEPTH, LRU_WIDTH, D_MODEL), LRU_WIDTH ** -0.5)
    w_up_gdn = nrm((DEPTH, GDN_HEADS * GDN_DV, D_MODEL), (GDN_HEADS * GDN_DV) ** -0.5)
    w_out = nrm((DEPTH, D_MODEL, D_MODEL), DEEPNORM_BETA * D_MODEL ** -0.5)
    ln1_g = 1.0 + nrm((DEPTH, D_MODEL), 0.02)
    ln1_b = nrm((DEPTH, D_MODEL), 0.02)
    ffn_w1 = nrm((N_DENSE, D_MODEL, D_FF), D_MODEL ** -0.5)
    ffn_w3 = nrm((N_DENSE, D_MODEL, D_FF), D_MODEL ** -0.5)
    ffn_w2 = nrm((N_DENSE, D_FF, D_MODEL), DEEPNORM_BETA * D_FF ** -0.5)
    moe_router = nrm((N_MOE, D_MODEL, N_EXPERTS), D_MODEL ** -0.5)
    moe_w1 = nrm((N_MOE, N_EXPERTS, D_MODEL, D_FF_EXPERT), D_MODEL ** -0.5)
    moe_w3 = nrm((N_MOE, N_EXPERTS, D_MODEL, D_FF_EXPERT), D_MODEL ** -0.5)
    moe_w2 = nrm((N_MOE, N_EXPERTS, D_FF_EXPERT, D_MODEL), DEEPNORM_BETA * D_FF_EXPERT ** -0.5)
    ple_w = nrm((DEPTH, P_DIM, D_MODEL), P_DIM ** -0.5)
    ple_gate_w = nrm((DEPTH, D_MODEL, D_MODEL), D_MODEL ** -0.5)
    ln2_g = 1.0 + nrm((DEPTH, D_MODEL), 0.02)
    ln2_b = nrm((DEPTH, D_MODEL), 0.02)
    return {
        'x_prompt': x_prompt, 'x_sample': x_sample,
        'state_mlstm_C': state_mlstm_C, 'state_mlstm_n': state_mlstm_n, 'state_mlstm_m': state_mlstm_m,
        'state_rglru_h': state_rglru_h, 'state_rglru_conv': state_rglru_conv,
        'state_gdn_S': state_gdn_S, 'state_gdn_conv': state_gdn_conv,
        'p_prompt': p_prompt, 'p_sample': p_sample,
        'w_in': w_in, 'ml_b_i': ml_b_i, 'ml_b_f': ml_b_f, 'ml_norm_w': ml_norm_w,
        'rg_conv_w': rg_conv_w, 'rg_conv_b': rg_conv_b, 'rg_w_a': rg_w_a, 'rg_b_a': rg_b_a,
        'rg_w_x': rg_w_x, 'rg_b_x': rg_b_x, 'rg_lambda': rg_lambda,
        'gd_conv_w': gd_conv_w, 'gd_A_log': gd_A_log, 'gd_dt_bias': gd_dt_bias, 'gd_norm_w': gd_norm_w,
        'w_up_mlstm': w_up_mlstm, 'w_up_rglru': w_up_rglru, 'w_up_gdn': w_up_gdn, 'w_out': w_out,
        'ln1_g': ln1_g, 'ln1_b': ln1_b,
        'ffn_w1': ffn_w1, 'ffn_w3': ffn_w3, 'ffn_w2': ffn_w2,
        'moe_router': moe_router, 'moe_w1': moe_w1, 'moe_w3': moe_w3, 'moe_w2': moe_w2,
        'ple_w': ple_w, 'ple_gate_w': ple_gate_w, 'ln2_g': ln2_g, 'ln2_b': ln2_b,
    }


def reference(x_prompt, x_sample, state_mlstm_C, state_mlstm_n, state_mlstm_m, state_rglru_h,
              state_rglru_conv, state_gdn_S, state_gdn_conv, p_prompt, p_sample,
              w_in, ml_b_i, ml_b_f, ml_norm_w, rg_conv_w, rg_conv_b, rg_w_a, rg_b_a, rg_w_x, rg_b_x,
              rg_lambda, gd_conv_w, gd_A_log, gd_dt_bias, gd_norm_w, w_up_mlstm, w_up_rglru, w_up_gdn,
              w_out, ln1_g, ln1_b, ffn_w1, ffn_w3, ffn_w2, moe_router, moe_w1, moe_w3, moe_w2,
              ple_w, ple_gate_w, ln2_g, ln2_b):
    P = dict(w_in=w_in, ml_b_i=ml_b_i, ml_b_f=ml_b_f, ml_norm_w=ml_norm_w, rg_conv_w=rg_conv_w,
             rg_conv_b=rg_conv_b, rg_w_a=rg_w_a, rg_b_a=rg_b_a, rg_w_x=rg_w_x, rg_b_x=rg_b_x,
             rg_lambda=rg_lambda, gd_conv_w=gd_conv_w, gd_A_log=gd_A_log, gd_dt_bias=gd_dt_bias,
             gd_norm_w=gd_norm_w, w_up_mlstm=w_up_mlstm, w_up_rglru=w_up_rglru, w_up_gdn=w_up_gdn,
             w_out=w_out, ln1_g=ln1_g, ln1_b=ln1_b, ffn_w1=ffn_w1, ffn_w3=ffn_w3, ffn_w2=ffn_w2,
             moe_router=moe_router, moe_w1=moe_w1, moe_w3=moe_w3, moe_w2=moe_w2,
             ple_w=ple_w, ple_gate_w=ple_gate_w, ln2_g=ln2_g, ln2_b=ln2_b)

    b = x_prompt.shape[0]
    init = dict(
        ml_C=jnp.zeros((DEPTH, b, ML_HEADS, ML_DV, ML_DK), F32),
        ml_n=jnp.zeros((DEPTH, b, ML_HEADS, ML_DK), F32),
        ml_m=jnp.zeros((DEPTH, b, ML_HEADS), F32),
        rg_h=jnp.zeros((DEPTH, b, LRU_WIDTH), F32),
        rg_conv=jnp.zeros((DEPTH, b, CONV_W - 1, LRU_WIDTH), x_prompt.dtype),
        gd_S=jnp.zeros((DEPTH, b, GDN_HEADS, GDN_DK, GDN_DV), F32),
        gd_conv=jnp.zeros((DEPTH, b, CONV_W - 1, GDN_CONV_DIM), x_prompt.dtype))
    y_prompt, sp = _run_trunk(x_prompt, p_prompt, init, P)

    past = dict(ml_C=state_mlstm_C, ml_n=state_mlstm_n, ml_m=state_mlstm_m, rg_h=state_rglru_h,
                rg_conv=state_rglru_conv, gd_S=state_gdn_S, gd_conv=state_gdn_conv)
    y_sample, ss = _run_trunk(x_sample, p_sample, past, P)

    pd = x_prompt.dtype
    return (y_prompt, y_sample,
            sp['ml_C'].astype(pd), sp['ml_n'].astype(pd), sp['ml_m'].astype(pd), sp['rg_h'].astype(pd),
            sp['rg_conv'].astype(pd), sp['gd_S'].astype(pd), sp['gd_conv'].astype(pd),
            ss['ml_C'].astype(state_mlstm_C.dtype), ss['ml_n'].astype(state_mlstm_n.dtype),
            ss['ml_m'].astype(state_mlstm_m.dtype), ss['rg_h'].astype(state_rglru_h.dtype),
            ss['rg_conv'].astype(state_rglru_conv.dtype), ss['gd_S'].astype(state_gdn_S.dtype),
            ss['gd_conv'].astype(state_gdn_conv.dtype))
```

```python
import functools
import math

import jax
import jax.numpy as jnp
from jax import lax
from jax.experimental import pallas as pl
from jax.experimental.pallas import tpu as pltpu

F32 = jnp.float32
BF16 = jnp.bfloat16

D_MODEL = 2048
DEPTH = 2
P_DIM = 256
CONV_W = 4
N_BRANCH = 3
ML_HEADS = 4
ML_DK = 128
ML_DV = 256
ML_GATE_CAP = 15.0
LRU_WIDTH = 1024
LRU_BLOCKS = 4
LRU_BLOCK = LRU_WIDTH // LRU_BLOCKS
LRU_C = 8.0
GDN_HEADS = 4
GDN_DK = 128
GDN_DV = 256
GDN_QK = GDN_HEADS * GDN_DK
GDN_CONV_DIM = 2 * GDN_QK + GDN_HEADS * GDN_DV
N_EXPERTS = 8
DEEPNORM_ALPHA = (2 * DEPTH) ** 0.25
LN_EPS = 1e-5
RMS_EPS = 1e-6
L2_EPS = 1e-6

LANE = 128
SUBLANE = 8
CHUNK = 128
VMEM_LIMIT = 56 * 1024 * 1024

C_MLQ, C_MLK, C_MLV, C_MLO = 0, 512, 1024, 2048
C_RGX, C_RGY = 3072, 4096
C_GDQ, C_GDK, C_GDV, C_GDZ = 5120, 5632, 6144, 7168
C_MG = 8192
N_MAIN = C_MG + N_BRANCH * D_MODEL
G_MLI, G_MLF, G_GDB, G_GDA = 0, 4, 8, 12

HIGHEST = lax.Precision.HIGHEST


def _cparams(sem):
    return pltpu.CompilerParams(dimension_semantics=sem, vmem_limit_bytes=VMEM_LIMIT)


def _dot(a, b):
    return jnp.dot(a.astype(BF16), b.astype(BF16), preferred_element_type=F32)


def _dot_nt(a, b):
    return lax.dot_general(a.astype(BF16), b.astype(BF16), (((1,), (1,)), ((), ())),
                           preferred_element_type=F32)


def _dot_tn(a, b):
    return lax.dot_general(a.astype(BF16), b.astype(BF16), (((0,), (0,)), ((), ())),
                           preferred_element_type=F32)


def _dot_hi(a, b):
    return jnp.dot(a, b, precision=HIGHEST, preferred_element_type=F32)


def _dot_nt_hi(a, b):
    return lax.dot_general(a, b, (((1,), (1,)), ((), ())), precision=HIGHEST,
                           preferred_element_type=F32)


def _sigmoid(x):
    return 1.0 / (1.0 + jnp.exp(-x))


def _silu(x):
    return x * _sigmoid(x)


def _softplus(x):
    return jnp.maximum(x, 0.0) + jnp.log1p(jnp.exp(-jnp.abs(x)))


def _proj(xs, ws, w_specs, extras, e_specs, epilogue, out_shapes, n_cols, tm, tn, name,
          lead_grid=(), x_of_w=None):
    m = xs[0].shape[0]
    nl = len(lead_grid)
    grid = tuple(lead_grid) + (n_cols // tn, m // tm)
    x_specs = [pl.BlockSpec((tm, x.shape[1]), lambda *g: (g[-1], 0)) for x in xs]
    o_specs = [pl.BlockSpec((tm, tn), lambda *g: (g[-1], g[-2]) if nl == 0 else
                            (g[-1], g[0] * (n_cols // tn) + g[-2])) for _ in out_shapes]
    scratch = [pltpu.VMEM(tuple(d for d in s.block_shape if d is not None), BF16) for s in w_specs]
    xw = tuple(range(len(ws))) if x_of_w is None else tuple(x_of_w)

    def body(*refs):
        nx, nw, ne, no = len(xs), len(ws), len(extras), len(out_shapes)
        x_refs = refs[:nx]
        w_refs = refs[nx:nx + nw]
        e_refs = refs[nx + nw:nx + nw + ne]
        o_refs = refs[nx + nw + ne:nx + nw + ne + no]
        wb_refs = refs[nx + nw + ne + no:]

        @pl.when(pl.program_id(nl + 1) == 0)
        def _():
            for w, wb in zip(w_refs, wb_refs):
                wb[...] = w[...].astype(BF16)

        xv = [x[...].astype(BF16) for x in x_refs]
        accs = [jnp.dot(xv[j], wb[...], preferred_element_type=F32) for j, wb in zip(xw, wb_refs)]
        outs = epilogue(accs, [e[...] for e in e_refs], [pl.program_id(i) for i in range(nl)])
        for o, v in zip(o_refs, outs):
            o[...] = v.astype(o.dtype)

    return pl.pallas_call(
        body,
        out_shape=out_shapes,
        grid=grid,
        in_specs=x_specs + list(w_specs) + list(e_specs),
        out_specs=o_specs,
        scratch_shapes=scratch,
        compiler_params=_cparams(("arbitrary",) * len(grid)),
        name=name,
    )(*xs, *ws, *extras)


def _rowfull_ln(h, w_bf16, res, extra, gamma, beta, tm, tk, name):
    m, k = h.shape
    n = w_bf16.shape[1]
    nk = k // tk
    has_extra = extra is not None

    def body(*refs):
        if has_extra:
            h_ref, w_ref, r_ref, e_ref, g_ref, b_ref, o_ref, ob_ref, acc_ref = refs
        else:
            h_ref, w_ref, r_ref, g_ref, b_ref, o_ref, ob_ref, acc_ref = refs
            e_ref = None
        kk = pl.program_id(1)

        @pl.when(kk == 0)
        def _():
            acc_ref[...] = jnp.zeros_like(acc_ref)

        acc_ref[...] += jnp.dot(h_ref[...], w_ref[...], preferred_element_type=F32)

        @pl.when(kk == nk - 1)
        def _():
            y = DEEPNORM_ALPHA * r_ref[...] + acc_ref[...]
            if e_ref is not None:
                y = y + e_ref[...].astype(F32)
            mu = jnp.mean(y, axis=-1, keepdims=True)
            yc = y - mu
            var = jnp.mean(yc * yc, axis=-1, keepdims=True)
            out = yc * lax.rsqrt(var + LN_EPS) * g_ref[...] + b_ref[...]
            o_ref[...] = out
            ob_ref[...] = out.astype(BF16)

    row = lambda i, j: (i, 0)
    in_specs = [pl.BlockSpec((tm, tk), lambda i, j: (i, j)),
                pl.BlockSpec((tk, n), lambda i, j: (j, 0)),
                pl.BlockSpec((tm, n), row)]
    args = [h, w_bf16, res]
    if has_extra:
        in_specs.append(pl.BlockSpec((tm, n), row))
        args.append(extra)
    in_specs += [pl.BlockSpec((1, n), lambda i, j: (0, 0))] * 2
    args += [gamma.reshape(1, n), beta.reshape(1, n)]
    return pl.pallas_call(
        body,
        out_shape=[jax.ShapeDtypeStruct((m, n), F32), jax.ShapeDtypeStruct((m, n), BF16)],
        grid=(m // tm, nk),
        in_specs=in_specs,
        out_specs=[pl.BlockSpec((tm, n), row), pl.BlockSpec((tm, n), row)],
        scratch_shapes=[pltpu.VMEM((tm, n), F32)],
        compiler_params=_cparams(("arbitrary", "arbitrary")),
        name=name,
    )(*args)


def _lane_pick(x, lane_ids, j):
    return jnp.sum(jnp.where(lane_ids == j, x, 0.0), axis=1, keepdims=True)


def _seq_masks(g, lg_shift):
    row = lax.broadcasted_iota(jnp.int32, (CHUNK, CHUNK), 0)
    col = lax.broadcasted_iota(jnp.int32, (CHUNK, CHUNK), 1)
    if g == 1:
        return col <= row, col < row
    same = (row >> lg_shift) == (col >> lg_shift)
    return same & (col <= row), same & (col < row)


def _per_seq_rows(vals, g, lg):
    if g == 1:
        return vals[0]
    return jnp.concatenate([jnp.broadcast_to(v, (lg, v.shape[1])) for v in vals], axis=0)


def _last_rows(col, g, lg):
    return _per_seq_rows([col[(i + 1) * lg - 1:(i + 1) * lg, :] for i in range(g)], g, lg)


def _outer_sum_hi(a_col, b_col, lane_ids):
    a2 = jnp.where(lane_ids == 0, a_col, jnp.where(lane_ids == 1, 1.0, 0.0))
    b2 = jnp.where(lane_ids == 0, 1.0, jnp.where(lane_ids == 1, b_col, 0.0))
    return _dot_nt_hi(a2, b2)


def _conv_rows(x, prev, w, g, lg):
    f = x.shape[1]
    xx = jnp.concatenate([prev.reshape(g, SUBLANE, f), x.reshape(g, lg, f)], axis=1)
    y = xx[:, SUBLANE:, :] * w[CONV_W - 1:CONV_W, :]
    for s in range(1, CONV_W):
        y = y + xx[:, SUBLANE - s:SUBLANE - s + lg, :] * w[CONV_W - 1 - s:CONV_W - s, :]
    return y.reshape(g * lg, f)


def _mlstm_body(r, g, q_ref, k_ref, v_ref, o_ref, gt_ref, gb_ref, nw_ref, c0_ref, n0_ref, m0_ref,
                y_ref, c_out, n_out, m_out, c_s, n_s, m_s):
    lg = CHUNK // g
    lg_shift = int(math.log2(lg))
    h = pl.program_id(1)
    t = pl.program_id(2)

    @pl.when(t == 0)
    def _():
        c_s[...] = c0_ref[:, 0]
        n_s[...] = n0_ref[:, 0]
        m_s[...] = jnp.broadcast_to(m0_ref[:, 0], m_s.shape)

    lane_ids = lax.broadcasted_iota(jnp.int32, (CHUNK, LANE), 1)
    causal, _ = _seq_masks(g, lg_shift)
    tri = causal.astype(F32)
    rows_i = lax.broadcasted_iota(jnp.int32, (CHUNK, 1), 0)
    scale = ML_DK ** -0.5

    for c in range(r // CHUNK):
        rs = slice(c * CHUNK, (c + 1) * CHUNK)
        pre = gt_ref[rs, :] + gb_ref[...]
        capd = ML_GATE_CAP * jnp.tanh(pre * (1.0 / ML_GATE_CAP))
        logsig = jnp.minimum(capd, 0.0) - jnp.log1p(jnp.exp(-jnp.abs(capd)))
        ic = _lane_pick(capd, lane_ids, G_MLI + h)
        lf_l = jnp.where(lane_ids == G_MLF + h, logsig, 0.0)
        bc = jnp.sum(_dot_hi(tri, lf_l), axis=1, keepdims=True)
        d = _outer_sum_hi(bc, ic - bc, lane_ids)
        dm = jnp.where(causal, d, -jnp.inf)
        m_prev = _per_seq_rows([m_s[i][:, 0:1] for i in range(g)], g, lg)
        m_t = jnp.maximum(m_prev + bc, jnp.max(dm, axis=1, keepdims=True))
        dmat = jnp.exp(dm - m_t)
        qs = q_ref[rs, :] * scale
        kc = k_ref[rs, :]
        vc = v_ref[rs, :]
        s = _dot_nt(qs, kc) * dmat
        inter = jnp.exp(bc + m_prev - m_t)
        if g == 1:
            qc = _dot_nt(qs, c_s[0])
            qn = jnp.sum(qs * n_s[0], axis=1, keepdims=True)
        else:
            qc = jnp.concatenate([_dot_nt(qs[i * lg:(i + 1) * lg], c_s[i]) for i in range(g)], axis=0)
            qn = jnp.concatenate([jnp.sum(qs[i * lg:(i + 1) * lg] * n_s[i], axis=1, keepdims=True)
                                  for i in range(g)], axis=0)
        num = _dot(s, vc) + inter * qc
        den = jnp.sum(s, axis=1, keepdims=True) + inter * qn
        hh = num / jnp.maximum(jnp.abs(den), jnp.exp(-m_t))
        bc_end = _last_rows(bc, g, lg)
        m_new = _last_rows(m_t, g, lg)
        w_col = jnp.exp(ic - bc + bc_end - m_new)
        dec = jnp.exp(bc_end + m_prev - m_new)
        vw = vc * w_col
        kw = kc * w_col
        for i in range(g):
            if g == 1:
                vwi, kwi, dec_i, mn_i = vw, kw, dec, m_new
            else:
                vwi = jnp.where((rows_i >> lg_shift) == i, vw, 0.0)
                kwi = kw[i * lg:(i + 1) * lg]
                dec_i = dec[i * lg:i * lg + 1, :]
                mn_i = m_new[i * lg:i * lg + 1, :]
            c_s[i] = dec_i * c_s[i] + _dot_tn(vwi, kc)
            n_s[i] = dec_i * n_s[i] + jnp.sum(kwi, axis=0, keepdims=True)
            m_s[i] = jnp.broadcast_to(mn_i, (1, LANE))
        hn = hh * lax.rsqrt(jnp.mean(hh * hh, axis=1, keepdims=True) + RMS_EPS) * nw_ref[...]
        y_ref[rs, :] = (_sigmoid(o_ref[rs, :]) * hn).astype(y_ref.dtype)

    @pl.when(t == pl.num_programs(2) - 1)
    def _():
        c_out[:, 0] = c_s[...]
        n_out[:, 0] = n_s[...]
        m_out[:, 0] = m_s[:, :, 0:1]


def _gdn_body(r, g, q_ref, k_ref, v_ref, z_ref, gt_ref, cwq_ref, cwk_ref, cwv_ref, al_ref, dtb_ref,
              nw_ref, bq_ref, bk_ref, bv_ref, s0_ref, y_ref, s_out, s_s, pq_s, pk_s, pv_s):
    lg = CHUNK // g
    lg_shift = int(math.log2(lg))
    lseq = r if g == 1 else lg
    h = pl.program_id(1)
    t = pl.program_id(2)

    @pl.when(t == 0)
    def _():
        s_s[...] = s0_ref[:, 0]
        pq_s[...] = bq_ref[...]
        pk_s[...] = bk_ref[...]
        pv_s[...] = bv_ref[...]

    xq, xk, xv = q_ref[...], k_ref[...], v_ref[...]
    qa = _silu(_conv_rows(xq, pq_s[...], cwq_ref[...], g, lseq))
    ka = _silu(_conv_rows(xk, pk_s[...], cwk_ref[...], g, lseq))
    va = _silu(_conv_rows(xv, pv_s[...], cwv_ref[...], g, lseq))
    if g == 1:
        pq_s[...] = xq[r - SUBLANE:, :]
        pk_s[...] = xk[r - SUBLANE:, :]
        pv_s[...] = xv[r - SUBLANE:, :]
    qn = qa * lax.rsqrt(jnp.sum(qa * qa, axis=1, keepdims=True) + L2_EPS) * (GDN_DK ** -0.5)
    kn = ka * lax.rsqrt(jnp.sum(ka * ka, axis=1, keepdims=True) + L2_EPS)

    lane_ids = lax.broadcasted_iota(jnp.int32, (CHUNK, LANE), 1)
    causal, strict = _seq_masks(g, lg_shift)
    tri = causal.astype(F32)
    eye = (lax.broadcasted_iota(jnp.int32, (CHUNK, CHUNK), 0)
           == lax.broadcasted_iota(jnp.int32, (CHUNK, CHUNK), 1)).astype(F32)
    rows_i = lax.broadcasted_iota(jnp.int32, (CHUNK, 1), 0)

    for c in range(r // CHUNK):
        rs = slice(c * CHUNK, (c + 1) * CHUNK)
        gt = gt_ref[rs, :]
        beta = _lane_pick(_sigmoid(gt), lane_ids, G_GDB + h)
        gval = -jnp.exp(al_ref[...]) * _softplus(gt + dtb_ref[...])
        gl_l = jnp.where(lane_ids == G_GDA + h, gval, 0.0)
        gc = jnp.sum(_dot_hi(tri, gl_l), axis=1, keepdims=True)
        dg = _outer_sum_hi(gc, -gc, lane_ids)
        decay = jnp.exp(jnp.where(causal, dg, -jnp.inf))
        qs, kc, vc = qn[rs], kn[rs], va[rs]
        kk = _dot_nt(kc, kc)
        p = jnp.where(strict, -(beta * kk * decay), 0.0)
        inv = eye + p
        pk = p
        for _ in range(1, lg_shift):
            pk = _dot_hi(pk, pk)
            inv = inv + _dot_hi(inv, pk)
        egc = jnp.exp(gc)
        u_pre = _dot_hi(inv, beta * vc)
        w = _dot_hi(inv, (beta * egc) * kc)
        qk = _dot_nt(qs, kc) * decay
        q_dec = qs * egc
        g_end = _last_rows(gc, g, lg)
        k_dec = kc * jnp.exp(g_end - gc)
        if g == 1:
            s_old = s_s[0]
            u = u_pre - _dot(w, s_old)
            o = _dot(q_dec, s_old) + _dot(qk, u)
            s_s[0] = jnp.exp(g_end) * s_old + _dot_tn(k_dec, u)
        else:
            us, os_ = [], []
            for i in range(g):
                sl = slice(i * lg, (i + 1) * lg)
                us.append(u_pre[sl] - _dot(w[sl], s_s[i]))
                os_.append(_dot(q_dec[sl], s_s[i]))
            u = jnp.concatenate(us, axis=0)
            o = jnp.concatenate(os_, axis=0) + _dot(qk, u)
            for i in range(g):
                kdi = jnp.where((rows_i >> lg_shift) == i, k_dec, 0.0)
                s_s[i] = jnp.exp(g_end[i * lg:i * lg + 1, :]) * s_s[i] + _dot_tn(kdi, u)
        hn = o * lax.rsqrt(jnp.mean(o * o, axis=1, keepdims=True) + RMS_EPS) * nw_ref[...]
        y_ref[rs, :] = (hn * _silu(z_ref[rs, :])).astype(y_ref.dtype)

    @pl.when(t == pl.num_programs(2) - 1)
    def _():
        s_out[:, 0] = s_s[...]


def _rglru_body(r, g, x_ref, yg_ref, cw_ref, cb_ref, wa_ref, wx_ref, ba_ref, bx_ref, lam_ref,
                buf_ref, h0_ref, y_ref, h_out, h_s, px_s):
    lseq = r if g == 1 else CHUNK // g
    t = pl.program_id(2)

    @pl.when(t == 0)
    def _():
        h_s[...] = h0_ref[:, 0]
        px_s[...] = buf_ref[...]

    x = x_ref[...]
    xc = _conv_rows(x, px_s[...], cw_ref[...], g, lseq) + cb_ref[...]
    if g == 1:
        px_s[...] = x[r - SUBLANE:, :]
    rg = _sigmoid(_dot(xc, wa_ref[0]) + ba_ref[...])
    ig = _sigmoid(_dot(xc, wx_ref[0]) + bx_ref[...])
    log_a = -LRU_C * rg * _softplus(-lam_ref[...])
    a = jnp.exp(log_a)
    u = jnp.sqrt(1.0 - jnp.exp(2.0 * log_a)) * (ig * xc)
    pos = lax.broadcasted_iota(jnp.int32, (r, 1), 0) & (lseq - 1)
    sh = 1
    while sh < lseq:
        a_sh = pltpu.roll(a, sh, 0)
        u_sh = pltpu.roll(u, sh, 0)
        msk = pos >= sh
        u = jnp.where(msk, a * u_sh + u, u)
        a = jnp.where(msk, a * a_sh, a)
        sh *= 2
    hh = u + a * _per_seq_rows([h_s[i] for i in range(g)], g, lseq)
    for i in range(g):
        h_s[i] = hh[(i + 1) * lseq - 1:(i + 1) * lseq, :]
    yg = yg_ref[...]
    gelu = 0.5 * yg * (1.0 + jnp.tanh(math.sqrt(2.0 / math.pi) * (yg + 0.044715 * (yg * yg * yg))))
    y_ref[...] = (gelu * hh).astype(y_ref.dtype)

    @pl.when(t == pl.num_programs(2) - 1)
    def _():
        h_out[:, 0] = h_s[...]


class _Group:
    def __init__(self, n_seq, seq_len, row0):
        self.n_seq, self.seq_len, self.row0 = n_seq, seq_len, row0
        self.rows = n_seq * seq_len
        if seq_len >= 4 * CHUNK:
            self.r, self.g = 4 * CHUNK, 1
        elif seq_len >= CHUNK:
            self.r, self.g = CHUNK, 1
        else:
            self.r, self.g = CHUNK, CHUNK // seq_len
        self.s = self.g
        self.nb = n_seq // self.s
        self.nt = seq_len // self.r if self.g == 1 else 1
        assert seq_len >= CONV_W - 1 and row0 % self.r == 0 and self.rows % self.r == 0
        assert n_seq % self.s == 0 and self.r & (self.r - 1) == 0
        self.rb0 = row0 // self.r

    def rb(self, b, t):
        return self.rb0 + b * self.nt + t


def _mlstm_call(grp, hmain, gates, gate_bias, norm_w, c0, n0, m0):
    r, g, s = grp.r, grp.g, grp.s
    row = lambda off, w: pl.BlockSpec((r, w), lambda b, h, t: (grp.rb(b, t), off // w + h))
    st = lambda *tail: pl.BlockSpec((s, 1) + tail, lambda b, h, t: (b, h, 0, 0))
    nseq = grp.n_seq
    return pl.pallas_call(
        functools.partial(_mlstm_body, r, g),
        out_shape=[jax.ShapeDtypeStruct((grp.rows, ML_HEADS * ML_DV), BF16),
                   jax.ShapeDtypeStruct((nseq, ML_HEADS, ML_DV, ML_DK), F32),
                   jax.ShapeDtypeStruct((nseq, ML_HEADS, 1, ML_DK), F32),
                   jax.ShapeDtypeStruct((nseq, ML_HEADS, 1, 1), F32)],
        grid=(grp.nb, ML_HEADS, grp.nt),
        in_specs=[row(C_MLQ, ML_DK), row(C_MLK, ML_DK), row(C_MLV, ML_DV), row(C_MLO, ML_DV),
                  pl.BlockSpec((r, LANE), lambda b, h, t: (grp.rb(b, t), 0)),
                  pl.BlockSpec((1, LANE), lambda b, h, t: (0, 0)),
                  pl.BlockSpec((1, ML_DV), lambda b, h, t: (0, h)),
                  st(ML_DV, ML_DK), st(1, ML_DK), st(1, 1)],
        out_specs=[pl.BlockSpec((r, ML_DV), lambda b, h, t: (b * grp.nt + t, h)),
                   st(ML_DV, ML_DK), st(1, ML_DK), st(1, 1)],
        scratch_shapes=[pltpu.VMEM((s, ML_DV, ML_DK), F32), pltpu.VMEM((s, 1, ML_DK), F32),
                        pltpu.VMEM((s, 1, LANE), F32)],
        compiler_params=_cparams(("arbitrary",) * 3),
        name="mlstm",
    )(hmain, hmain, hmain, hmain, gates, gate_bias, norm_w, c0, n0, m0)


def _gdn_call(grp, hmain, gates, conv_w, a_log_row, dt_bias_row, norm_w, bufpad, s0):
    r, g, s = grp.r, grp.g, grp.s
    row = lambda off, w: pl.BlockSpec((r, w), lambda b, h, t: (grp.rb(b, t), off // w + h))
    cw = lambda off, w: pl.BlockSpec((CONV_W, w), lambda b, h, t: (0, off // w + h))
    bf = lambda off, w: pl.BlockSpec((s * SUBLANE, w), lambda b, h, t: (b, off // w + h))
    one = pl.BlockSpec((1, LANE), lambda b, h, t: (0, 0))
    st = pl.BlockSpec((s, 1, GDN_DK, GDN_DV), lambda b, h, t: (b, h, 0, 0))
    return pl.pallas_call(
        functools.partial(_gdn_body, r, g),
        out_shape=[jax.ShapeDtypeStruct((grp.rows, GDN_HEADS * GDN_DV), BF16),
                   jax.ShapeDtypeStruct((grp.n_seq, GDN_HEADS, GDN_DK, GDN_DV), F32)],
        grid=(grp.nb, GDN_HEADS, grp.nt),
        in_specs=[row(C_GDQ, GDN_DK), row(C_GDK, GDN_DK), row(C_GDV, GDN_DV), row(C_GDZ, GDN_DV),
                  pl.BlockSpec((r, LANE), lambda b, h, t: (grp.rb(b, t), 0)),
                  cw(0, GDN_DK), cw(GDN_QK, GDN_DK), cw(2 * GDN_QK, GDN_DV),
                  one, one, pl.BlockSpec((1, GDN_DV), lambda b, h, t: (0, 0)),
                  bf(0, GDN_DK), bf(GDN_QK, GDN_DK), bf(2 * GDN_QK, GDN_DV), st],
        out_specs=[pl.BlockSpec((r, GDN_DV), lambda b, h, t: (b * grp.nt + t, h)), st],
        scratch_shapes=[pltpu.VMEM((s, GDN_DK, GDN_DV), F32),
                        pltpu.VMEM((s * SUBLANE, GDN_DK), F32), pltpu.VMEM((s * SUBLANE, GDN_DK), F32),
                        pltpu.VMEM((s * SUBLANE, GDN_DV), F32)],
        compiler_params=_cparams(("arbitrary",) * 3),
        name="gdn",
    )(hmain, hmain, hmain, hmain, gates, conv_w, conv_w, conv_w, a_log_row, dt_bias_row, norm_w,
      bufpad, bufpad, bufpad, s0)


def _rglru_call(grp, hmain, conv_w, conv_b, w_a, w_x, b_a, b_x, lam, bufpad, h0):
    r, g, s = grp.r, grp.g, grp.s
    w = LRU_BLOCK
    row = lambda off: pl.BlockSpec((r, w), lambda b, h, t: (grp.rb(b, t), off // w + h))
    vec = pl.BlockSpec((1, w), lambda b, h, t: (0, h))
    blk = pl.BlockSpec((1, w, w), lambda b, h, t: (h, 0, 0))
    st = pl.BlockSpec((s, 1, 1, w), lambda b, h, t: (b, h, 0, 0))
    return pl.pallas_call(
        functools.partial(_rglru_body, r, g),
        out_shape=[jax.ShapeDtypeStruct((grp.rows, LRU_WIDTH), BF16),
                   jax.ShapeDtypeStruct((grp.n_seq, LRU_BLOCKS, 1, w), F32)],
        grid=(grp.nb, LRU_BLOCKS, grp.nt),
        in_specs=[row(C_RGX), row(C_RGY),
                  pl.BlockSpec((CONV_W, w), lambda b, h, t: (0, h)), vec, blk, blk, vec, vec, vec,
                  pl.BlockSpec((s * SUBLANE, w), lambda b, h, t: (b, h)), st],
        out_specs=[pl.BlockSpec((r, w), lambda b, h, t: (b * grp.nt + t, h)), st],
        scratch_shapes=[pltpu.VMEM((s, 1, w), F32), pltpu.VMEM((s * SUBLANE, w), F32)],
        compiler_params=_cparams(("arbitrary",) * 3),
        name="rglru",
    )(hmain, hmain, conv_w, conv_b, w_a, w_x, b_a, b_x, lam, bufpad, h0)


def _pad_conv_state(buf):
    b, k, c = buf.shape
    return jnp.pad(buf.astype(F32), ((0, 0), (SUBLANE - k, 0), (0, 0))).reshape(b * SUBLANE, c)


def _router_call(x, router, tm):
    m, d = x.shape
    wr = jnp.pad(router.astype(F32), ((0, 0), (0, LANE - N_EXPERTS)))

    def body(x_ref, w_ref, o_ref):
        logits = _dot_hi(x_ref[...], w_ref[...])
        lane = lax.broadcasted_iota(jnp.int32, logits.shape, 1)
        lg = jnp.where(lane < N_EXPERTS, logits, -jnp.inf)
        v1 = jnp.max(lg, axis=1, keepdims=True)
        i1 = jnp.min(jnp.where(lg == v1, lane, LANE), axis=1, keepdims=True)
        lg2 = jnp.where(lane == i1, -jnp.inf, lg)
        v2 = jnp.max(lg2, axis=1, keepdims=True)
        i2 = jnp.min(jnp.where(lg2 == v2, lane, LANE), axis=1, keepdims=True)
        e2 = jnp.exp(v2 - v1)
        w1 = 1.0 / (1.0 + e2)
        w2 = e2 / (1.0 + e2)
        o_ref[...] = jnp.where(lane == i1, w1, jnp.where(lane == i2, w2, 0.0))

    return pl.pallas_call(
        body,
        out_shape=jax.ShapeDtypeStruct((m, LANE), F32),
        grid=(m // tm,),
        in_specs=[pl.BlockSpec((tm, d), lambda i: (i, 0)), pl.BlockSpec((d, LANE), lambda i: (0, 0))],
        out_specs=pl.BlockSpec((tm, LANE), lambda i: (i, 0)),
        compiler_params=_cparams(("arbitrary",)),
        name="router",
    )(x, wr)


def _run_group_mixers(grp, l, hmain, gates, st, prm):
    b = grp.n_seq
    ya, ml_c, ml_n, ml_m = _mlstm_call(
        grp, hmain, gates, prm["gate_bias"], prm["ml_norm_w"],
        st["ml_C"][l].astype(F32), st["ml_n"][l].astype(F32).reshape(b, ML_HEADS, 1, ML_DK),
        st["ml_m"][l].astype(F32).reshape(b, ML_HEADS, 1, 1))
    yb, rg_h = _rglru_call(
        grp, hmain, prm["rg_conv_w"], prm["rg_conv_b"], prm["rg_w_a"], prm["rg_w_x"], prm["rg_b_a"],
        prm["rg_b_x"], prm["rg_lambda"], _pad_conv_state(st["rg_conv"][l]),
        st["rg_h"][l].astype(F32).reshape(b, LRU_BLOCKS, 1, LRU_BLOCK))
    yc, gd_s = _gdn_call(
        grp, hmain, gates, prm["gd_conv_w"], prm["gd_a_log"], prm["gd_dt_bias"], prm["gd_norm_w"],
        _pad_conv_state(st["gd_conv"][l]), st["gd_S"][l].astype(F32))
    tail = hmain[grp.row0:grp.row0 + grp.rows].reshape(b, grp.seq_len, N_MAIN)[:, grp.seq_len - (CONV_W - 1):, :]
    new = dict(ml_C=ml_c, ml_n=ml_n.reshape(b, ML_HEADS, ML_DK), ml_m=ml_m.reshape(b, ML_HEADS),
               rg_h=rg_h.reshape(b, LRU_WIDTH), rg_conv=tail[:, :, C_RGX:C_RGX + LRU_WIDTH],
               gd_S=gd_s, gd_conv=tail[:, :, C_GDQ:C_GDQ + GDN_CONV_DIM])
    return ya, yb, yc, new


_NAMES = ("ml_C", "ml_n", "ml_m", "rg_h", "rg_conv", "gd_S", "gd_conv")


def kernel(x_prompt, x_sample, state_mlstm_C, state_mlstm_n, state_mlstm_m, state_rglru_h, state_rglru_conv, state_gdn_S, state_gdn_conv, p_prompt, p_sample, w_in, ml_b_i, ml_b_f, ml_norm_w, rg_conv_w, rg_conv_b, rg_w_a, rg_b_a, rg_w_x, rg_b_x, rg_lambda, gd_conv_w, gd_A_log, gd_dt_bias, gd_norm_w, w_up_mlstm, w_up_rglru, w_up_gdn, w_out, ln1_g, ln1_b, ffn_w1, ffn_w3, ffn_w2, moe_router, moe_w1, moe_w3, moe_w2, ple_w, ple_gate_w, ln2_g, ln2_b):
    bp, tp, d = x_prompt.shape
    bs, ts, _ = x_sample.shape
    depth = w_in.shape[0]
    gp = _Group(bp, tp, 0)
    gs = _Group(bs, ts, bp * tp)
    m = gp.rows + gs.rows
    pd = x_prompt.dtype

    zeros = lambda *shape: jnp.zeros(shape, F32)
    st_p = dict(ml_C=zeros(depth, bp, ML_HEADS, ML_DV, ML_DK), ml_n=zeros(depth, bp, ML_HEADS, ML_DK),
                ml_m=zeros(depth, bp, ML_HEADS), rg_h=zeros(depth, bp, LRU_WIDTH),
                rg_conv=zeros(depth, bp, CONV_W - 1, LRU_WIDTH),
                gd_S=zeros(depth, bp, GDN_HEADS, GDN_DK, GDN_DV),
                gd_conv=zeros(depth, bp, CONV_W - 1, GDN_CONV_DIM))
    st_s = dict(ml_C=state_mlstm_C, ml_n=state_mlstm_n, ml_m=state_mlstm_m, rg_h=state_rglru_h,
                rg_conv=state_rglru_conv, gd_S=state_gdn_S, gd_conv=state_gdn_conv)

    x = jnp.concatenate([x_prompt.reshape(gp.rows, d), x_sample.reshape(gs.rows, d)], axis=0).astype(F32)
    xb = x.astype(BF16)
    new_p = {k: [] for k in _NAMES}
    new_s = {k: [] for k in _NAMES}
    ident = lambda accs, extras, ids: [accs[0]]
    lane_row = lambda v, off: jnp.zeros((1, LANE), F32).at[0, off:off + v.shape[0]].set(v.astype(F32))

    for l in range(depth):
        w = w_in[l]
        w_main = jnp.concatenate([w[:, 0:2048], w[:, 2056:7176], w[:, 7184:]], axis=1)
        w_gate = jnp.pad(jnp.concatenate([w[:, 2048:2056], w[:, 7176:7184]], axis=1), ((0, 0), (0, LANE - 16)))
        hmain = _proj([xb], [w_main], [pl.BlockSpec((d, 1024), lambda n, i: (0, n))], [], [], ident,
                      [jax.ShapeDtypeStruct((m, N_MAIN), F32)], N_MAIN, 1024, 1024, "in_proj")[0]
        gates = _proj([xb], [w_gate], [pl.BlockSpec((d, LANE), lambda n, i: (0, n))], [], [], ident,
                      [jax.ShapeDtypeStruct((m, LANE), F32)], LANE, 1024, LANE, "gate_proj")[0]

        prm = dict(
            gate_bias=lane_row(ml_b_i[l], G_MLI) + lane_row(ml_b_f[l], G_MLF),
            ml_norm_w=ml_norm_w[l].astype(F32).reshape(1, -1),
            rg_conv_w=rg_conv_w[l].astype(F32), rg_conv_b=rg_conv_b[l].astype(F32).reshape(1, -1),
            rg_w_a=rg_w_a[l], rg_w_x=rg_w_x[l], rg_b_a=rg_b_a[l].astype(F32).reshape(1, -1),
            rg_b_x=rg_b_x[l].astype(F32).reshape(1, -1), rg_lambda=rg_lambda[l].astype(F32).reshape(1, -1),
            gd_conv_w=gd_conv_w[l].astype(F32), gd_a_log=lane_row(gd_A_log[l], G_GDA),
            gd_dt_bias=lane_row(gd_dt_bias[l], G_GDA), gd_norm_w=gd_norm_w[l].astype(F32).reshape(1, -1))
        ya_p, yb_p, yc_p, np_ = _run_group_mixers(gp, l, hmain, gates, st_p, prm)
        ya_s, yb_s, yc_s, ns_ = _run_group_mixers(gs, l, hmain, gates, st_s, prm)
        for k in _NAMES:
            new_p[k].append(np_[k])
            new_s[k].append(ns_[k])
        ya = jnp.concatenate([ya_p, ya_s], axis=0)
        yb = jnp.concatenate([yb_p, yb_s], axis=0)
        yc = jnp.concatenate([yc_p, yc_s], axis=0)

        tn, tm = 512, 512
        up_spec = pl.BlockSpec((ya.shape[1], tn), lambda n, i: (0, n))
        mg_spec = lambda br: pl.BlockSpec((tm, tn), lambda n, i: (i, (C_MG + br * d) // tn + n))
        merged = _proj([ya, yb, yc], [w_up_mlstm[l], w_up_rglru[l], w_up_gdn[l]], [up_spec] * 3,
                       [hmain] * 3, [mg_spec(0), mg_spec(1), mg_spec(2)],
                       lambda accs, ex, ids: [_sigmoid(ex[0]) * accs[0] + _sigmoid(ex[1]) * accs[1]
                                              + _sigmoid(ex[2]) * accs[2]],
                       [jax.ShapeDtypeStruct((m, d), BF16)], d, tm, tn, "merge")[0]
        x, xb = _rowfull_ln(merged, w_out[l].astype(BF16), x, None, ln1_g[l].astype(F32), ln1_b[l].astype(F32),
                            512, d, "out_proj_ln")

        p_l = jnp.concatenate([p_prompt[l].reshape(gp.rows, -1), p_sample[l].reshape(gs.rows, -1)], axis=0)
        ple = _proj([xb, p_l], [ple_gate_w[l], ple_w[l]],
                    [pl.BlockSpec((d, 1024), lambda n, i: (0, n)), pl.BlockSpec((P_DIM, 1024), lambda n, i: (0, n))],
                    [], [], lambda accs, ex, ids: [_sigmoid(accs[0]) * accs[1]],
                    [jax.ShapeDtypeStruct((m, d), F32)], d, 1024, 1024, "ple")[0]
        j = l // 2
        if l % 2 == 0:
            ff = ffn_w1.shape[2]
            hmid = _proj([xb], [ffn_w1[j], ffn_w3[j]], [pl.BlockSpec((d, 512), lambda n, i: (0, n))] * 2,
                         [], [], lambda accs, ex, ids: [_silu(accs[0]) * accs[1]],
                         [jax.ShapeDtypeStruct((m, ff), BF16)], ff, 1024, 512, "ffn_up", x_of_w=(0, 0))[0]
            w2 = ffn_w2[j].astype(BF16)
        else:
            fe = moe_w1.shape[3]
            ne = moe_w1.shape[1]
            rgate = _router_call(x, moe_router[j], 512)
            tn = 256
            wspec = pl.BlockSpec((None, d, tn), lambda e, n, i: (e, 0, n))

            def moe_epi(accs, ex, ids):
                lane = lax.broadcasted_iota(jnp.int32, ex[0].shape, 1)
                return [_lane_pick(ex[0], lane, ids[0]) * (_silu(accs[0]) * accs[1])]

            hmid = _proj([xb], [moe_w1[j], moe_w3[j]], [wspec, wspec], [rgate],
                         [pl.BlockSpec((1024, LANE), lambda e, n, i: (i, 0))], moe_epi,
                         [jax.ShapeDtypeStruct((m, ne * fe), BF16)], fe, 1024, tn, "moe_up",
                         lead_grid=(ne,), x_of_w=(0, 0))[0]
            w2 = moe_w2[j].reshape(ne * fe, d).astype(BF16)
        x, xb = _rowfull_ln(hmid, w2, x, ple, ln2_g[l].astype(F32), ln2_b[l].astype(F32), 512, 512, "ffn_down_ln")

    y_prompt = x[:gp.rows].reshape(bp, tp, d).astype(pd)
    y_sample = x[gp.rows:].reshape(bs, ts, d).astype(x_sample.dtype)
    sp = {k: jnp.stack(v).astype(pd) for k, v in new_p.items()}
    ss = {k: jnp.stack(new_s[k]).astype(st_s[k].dtype) for k in _NAMES}
    return (y_prompt, y_sample) + tuple(sp[k] for k in _NAMES) + tuple(ss[k] for k in _NAMES)
```

```python
import functools
import math

import jax
import jax.numpy as jnp
from jax import lax
from jax.experimental import pallas as pl
from jax.experimental.pallas import tpu as pltpu

F32 = jnp.float32
BF16 = jnp.bfloat16

D_MODEL = 2048
DEPTH = 2
P_DIM = 256
CONV_W = 4
N_BRANCH = 3
ML_HEADS = 4
ML_DK = 128
ML_DV = 256
ML_GATE_CAP = 15.0
LRU_WIDTH = 1024
LRU_BLOCKS = 4
LRU_BLOCK = LRU_WIDTH // LRU_BLOCKS
LRU_C = 8.0
GDN_HEADS = 4
GDN_DK = 128
GDN_DV = 256
GDN_QK = GDN_HEADS * GDN_DK
GDN_CONV_DIM = 2 * GDN_QK + GDN_HEADS * GDN_DV
N_EXPERTS = 8
DEEPNORM_ALPHA = (2 * DEPTH) ** 0.25
LN_EPS = 1e-5
RMS_EPS = 1e-6
L2_EPS = 1e-6

LANE = 128
SUBLANE = 8
CHUNK = 128
VMEM_LIMIT = 56 * 1024 * 1024

C_MLQ, C_MLK, C_MLV, C_MLO = 0, 512, 1024, 2048
C_RGX, C_RGY = 3072, 4096
C_GDQ, C_GDK, C_GDV, C_GDZ = 5120, 5632, 6144, 7168
C_MG = 8192
N_MAIN = C_MG + N_BRANCH * D_MODEL
W_GATES_A, W_GATES_B = 2048, 7176
G_MLI, G_MLF, G_GDB, G_GDA = 0, 4, 8, 12
PROJ_TN = 1024

HIGHEST = lax.Precision.HIGHEST


def _cparams(sem):
    return pltpu.CompilerParams(dimension_semantics=sem, vmem_limit_bytes=VMEM_LIMIT)


def _dot(a, b):
    return jnp.dot(a.astype(BF16), b.astype(BF16), preferred_element_type=F32)


def _dot_nt(a, b):
    return lax.dot_general(a.astype(BF16), b.astype(BF16), (((1,), (1,)), ((), ())),
                           preferred_element_type=F32)


def _dot_tn(a, b):
    return lax.dot_general(a.astype(BF16), b.astype(BF16), (((0,), (0,)), ((), ())),
                           preferred_element_type=F32)


def _dot_hi(a, b):
    return jnp.dot(a, b, precision=HIGHEST, preferred_element_type=F32)


def _bmm(a, b):
    return jnp.einsum('cij,cjk->cik', a.astype(BF16), b.astype(BF16), preferred_element_type=F32)


def _bmm_nt(a, b):
    return jnp.einsum('cik,cjk->cij', a.astype(BF16), b.astype(BF16), preferred_element_type=F32)


def _bmm_tn(a, b):
    return jnp.einsum('csi,csj->cij', a.astype(BF16), b.astype(BF16), preferred_element_type=F32)


def _sigmoid(x):
    return 1.0 / (1.0 + jnp.exp(-x))


def _silu(x):
    return x * _sigmoid(x)


def _softplus(x):
    return jnp.maximum(x, 0.0) + jnp.log1p(jnp.exp(-jnp.abs(x)))


def _proj(xs, ws, w_specs, extras, e_specs, epilogue, out_shapes, n_cols, tm, tn, name,
          lead_grid=(), x_of_w=None):
    m = xs[0].shape[0]
    nl = len(lead_grid)
    grid = tuple(lead_grid) + (n_cols // tn, m // tm)
    x_specs = [pl.BlockSpec((tm, x.shape[1]), lambda *g: (g[-1], 0)) for x in xs]
    o_specs = [pl.BlockSpec((tm, tn), lambda *g: (g[-1], g[-2]) if nl == 0 else
                            (g[-1], g[0] * (n_cols // tn) + g[-2])) for _ in out_shapes]
    scratch = [pltpu.VMEM(tuple(d for d in s.block_shape if d is not None), BF16) for s in w_specs]
    xw = tuple(range(len(ws))) if x_of_w is None else tuple(x_of_w)

    def body(*refs):
        nx, nw, ne, no = len(xs), len(ws), len(extras), len(out_shapes)
        x_refs = refs[:nx]
        w_refs = refs[nx:nx + nw]
        e_refs = refs[nx + nw:nx + nw + ne]
        o_refs = refs[nx + nw + ne:nx + nw + ne + no]
        wb_refs = refs[nx + nw + ne + no:]

        @pl.when(pl.program_id(nl + 1) == 0)
        def _():
            for w, wb in zip(w_refs, wb_refs):
                wb[...] = w[...].astype(BF16)

        xv = [x[...].astype(BF16) for x in x_refs]
        accs = [jnp.dot(xv[j], wb[...], preferred_element_type=F32) for j, wb in zip(xw, wb_refs)]
        outs = epilogue(accs, [e[...] for e in e_refs], [pl.program_id(i) for i in range(nl)])
        for o, v in zip(o_refs, outs):
            o[...] = v.astype(o.dtype)

    return pl.pallas_call(
        body,
        out_shape=out_shapes,
        grid=grid,
        in_specs=x_specs + list(w_specs) + list(e_specs),
        out_specs=o_specs,
        scratch_shapes=scratch,
        compiler_params=_cparams(("arbitrary",) * len(grid)),
        name=name,
    )(*xs, *ws, *extras)


def _in_proj(xb, w, tm):
    m, d = xb.shape
    tn = PROJ_TN
    n_blocks = N_MAIN // tn
    first_a = W_GATES_A // tn
    first_b = (W_GATES_B - SUBLANE) // tn

    def body(x_ref, wm_ref, wn_ref, o_ref, wb_ref):
        n = pl.program_id(0)

        @pl.when(pl.program_id(1) == 0)
        def _():
            def shifted(s):
                wcat = jnp.concatenate([wm_ref[...], wn_ref[...]], axis=1)
                wb_ref[...] = wcat[:, s:s + tn].astype(BF16)

            @pl.when(n < first_a)
            def _():
                wb_ref[...] = wm_ref[...].astype(BF16)

            @pl.when((n >= first_a) & (n < first_b))
            def _():
                shifted(SUBLANE)

            @pl.when(n >= first_b)
            def _():
                shifted(2 * SUBLANE)

        o_ref[...] = jnp.dot(x_ref[...], wb_ref[...], preferred_element_type=F32)

    return pl.pallas_call(
        body,
        out_shape=jax.ShapeDtypeStruct((m, N_MAIN), F32),
        grid=(n_blocks, m // tm),
        in_specs=[pl.BlockSpec((tm, d), lambda n, i: (i, 0)),
                  pl.BlockSpec((d, tn), lambda n, i: (0, n)),
                  pl.BlockSpec((d, LANE), lambda n, i: (0, (n + 1) * (tn // LANE)))],
        out_specs=pl.BlockSpec((tm, tn), lambda n, i: (i, n)),
        scratch_shapes=[pltpu.VMEM((d, tn), BF16)],
        compiler_params=_cparams(("arbitrary", "arbitrary")),
        name="in_proj",
    )(xb, w, w)


def _rowfull_ln(h, w_bf16, res, extra, gamma, beta, tm, tk, name, split_rows=None):
    m, k = h.shape
    n = w_bf16.shape[1]
    nk = k // tk
    has_extra = extra is not None
    n_in = 6 if has_extra else 5
    tiles_a = None if split_rows is None else split_rows // tm

    def body(*refs):
        h_ref, w_ref, r_ref = refs[:3]
        e_ref = refs[3] if has_extra else None
        g_ref, b_ref = refs[n_in - 2:n_in]
        oa_ref, ob_ref, acc_ref = refs[n_in:]
        i = pl.program_id(0)
        kk = pl.program_id(1)

        @pl.when(kk == 0)
        def _():
            acc_ref[...] = jnp.zeros_like(acc_ref)

        acc_ref[...] += jnp.dot(h_ref[...], w_ref[...], preferred_element_type=F32)

        @pl.when(kk == nk - 1)
        def _():
            y = DEEPNORM_ALPHA * r_ref[...] + acc_ref[...]
            if e_ref is not None:
                y = y + e_ref[...].astype(F32)
            mu = jnp.mean(y, axis=-1, keepdims=True)
            yc = y - mu
            var = jnp.mean(yc * yc, axis=-1, keepdims=True)
            out = yc * lax.rsqrt(var + LN_EPS) * g_ref[...] + b_ref[...]
            if split_rows is None:
                oa_ref[...] = out
                ob_ref[...] = out.astype(BF16)
            else:
                @pl.when(i < tiles_a)
                def _():
                    oa_ref[...] = out

                @pl.when(i >= tiles_a)
                def _():
                    ob_ref[...] = out

    row = lambda i, j: (i, 0)
    in_specs = [pl.BlockSpec((tm, tk), lambda i, j: (i, j)),
                pl.BlockSpec((tk, n), lambda i, j: (j, 0)),
                pl.BlockSpec((tm, n), row)]
    args = [h, w_bf16, res]
    if has_extra:
        in_specs.append(pl.BlockSpec((tm, n), row))
        args.append(extra)
    in_specs += [pl.BlockSpec((1, n), lambda i, j: (0, 0))] * 2
    args += [gamma.reshape(1, n), beta.reshape(1, n)]
    if split_rows is None:
        out_shape = [jax.ShapeDtypeStruct((m, n), F32), jax.ShapeDtypeStruct((m, n), BF16)]
        out_specs = [pl.BlockSpec((tm, n), row), pl.BlockSpec((tm, n), row)]
    else:
        out_shape = [jax.ShapeDtypeStruct((split_rows, n), F32), jax.ShapeDtypeStruct((m - split_rows, n), F32)]
        out_specs = [pl.BlockSpec((tm, n), lambda i, j: (jnp.minimum(i, tiles_a - 1), 0)),
                     pl.BlockSpec((tm, n), lambda i, j: (jnp.maximum(i - tiles_a, 0), 0))]
    return pl.pallas_call(
        body,
        out_shape=out_shape,
        grid=(m // tm, nk),
        in_specs=in_specs,
        out_specs=out_specs,
        scratch_shapes=[pltpu.VMEM((tm, n), F32)],
        compiler_params=_cparams(("arbitrary", "arbitrary")),
        name=name,
    )(*args)


def _seq_masks(g, lg_shift):
    row = lax.broadcasted_iota(jnp.int32, (CHUNK, CHUNK), 0)
    col = lax.broadcasted_iota(jnp.int32, (CHUNK, CHUNK), 1)
    if g == 1:
        return col <= row, col < row
    same = (row >> lg_shift) == (col >> lg_shift)
    return same & (col <= row), same & (col < row)


def _per_seq_rows(vals, g, lg):
    if g == 1:
        return vals[0]
    return jnp.concatenate([jnp.broadcast_to(v, (lg, v.shape[1])) for v in vals], axis=0)


def _last_rows(col, g, lg):
    if g == 1:
        return col[:, CHUNK - 1:, :]
    return _per_seq_rows([col[0, (i + 1) * lg - 1:(i + 1) * lg, :] for i in range(g)], g, lg)[None]


def _split3(x):
    x1 = x.astype(BF16).astype(F32)
    r1 = x - x1
    x2 = r1.astype(BF16).astype(F32)
    x3 = (r1 - x2).astype(BF16).astype(F32)
    return x1, x2, x3


def _split2(x):
    hi = x.astype(BF16)
    return hi, (x - hi.astype(F32)).astype(BF16)


def _lanes3(col, lane_ids, first, other):
    c1, c2, c3 = _split3(col)
    x = jnp.where(lane_ids == first, c1, jnp.where(lane_ids == first + 1, c2,
                                                   jnp.where(lane_ids == first + 2, c3, 0.0)))
    return jnp.where((lane_ids >= other) & (lane_ids < other + 3), 1.0, x).astype(BF16)


def _cumsum_rows(col, tri_b, lane_ids):
    return jnp.sum(_bmm(tri_b, _lanes3(col, lane_ids, 0, LANE)), axis=-1, keepdims=True)


def _outer_sum(a_col, b_col, lane_ids):
    return _bmm_nt(_lanes3(a_col, lane_ids, 0, 3), _lanes3(b_col, lane_ids, 3, 0))


def _bmm3(a, b):
    ah, al = _split2(a)
    bh, bl = _split2(b)
    return _bmm(ah, bh) + (_bmm(ah, bl) + _bmm(al, bh))


def _bmm2(a, b):
    ah, al = _split2(a)
    bb = b.astype(BF16)
    return _bmm(ah, bb) + _bmm(al, bb)


def _conv_rows(x, prev, w, g, lg):
    f = x.shape[1]
    xx = jnp.concatenate([prev.reshape(g, SUBLANE, f), x.reshape(g, lg, f)], axis=1)
    y = xx[:, SUBLANE:, :] * w[CONV_W - 1:CONV_W, :]
    for s in range(1, CONV_W):
        y = y + xx[:, SUBLANE - s:SUBLANE - s + lg, :] * w[CONV_W - 1 - s:CONV_W - s, :]
    return y.reshape(g * lg, f), xx[:, lg:, :].reshape(g * SUBLANE, f)


def _mlstm_body(r, g, q_ref, k_ref, v_ref, o_ref, gt_ref, gb_ref, nw_ref, c0_ref, n0_ref, m0_ref,
                y_ref, c_out, n_out, m_out, c_s, n_s, m_s):
    lg = CHUNK // g
    lg_shift = int(math.log2(lg))
    nch = r // CHUNK
    h = pl.program_id(1)
    t = pl.program_id(2)

    @pl.when(t == 0)
    def _():
        c_s[...] = c0_ref[:, 0]
        n_s[...] = n0_ref[:, 0]
        m_s[...] = jnp.broadcast_to(m0_ref[:, 0], m_s.shape)

    lane3 = lax.broadcasted_iota(jnp.int32, (nch, CHUNK, LANE), 2)
    causal, _ = _seq_masks(g, lg_shift)
    tri_b = jnp.broadcast_to(causal.astype(BF16), (nch, CHUNK, CHUNK))

    pre = gt_ref[...].reshape(nch, CHUNK, LANE) + gb_ref[...]
    capd = ML_GATE_CAP * jnp.tanh(pre * (1.0 / ML_GATE_CAP))
    logsig = jnp.minimum(capd, 0.0) - jnp.log1p(jnp.exp(-jnp.abs(capd)))
    ic = jnp.sum(jnp.where(lane3 == G_MLI + h, capd, 0.0), axis=2, keepdims=True)
    lf = jnp.sum(jnp.where(lane3 == G_MLF + h, logsig, 0.0), axis=2, keepdims=True)
    bc = _cumsum_rows(lf, tri_b, lane3)
    dm = jnp.where(causal, _outer_sum(bc, ic - bc, lane3), -jnp.inf)
    rmax = jnp.max(dm, axis=2, keepdims=True)

    if g == 1:
        mp = m_s[0][:, 0:1]
        m_prevs = []
        for c in range(nch):
            m_prevs.append(mp.reshape(1, 1, 1))
            mp = jnp.maximum(mp + bc[c, CHUNK - 1:, :], rmax[c, CHUNK - 1:, :])
        m_prev = jnp.concatenate(m_prevs, axis=0) if nch > 1 else m_prevs[0]
        m_s[0] = jnp.broadcast_to(mp, (1, LANE))
    else:
        m_prev = _per_seq_rows([m_s[i][:, 0:1] for i in range(g)], g, lg)[None]
    m_t = jnp.maximum(m_prev + bc, rmax)
    dmat = jnp.exp(dm - m_t)
    q3 = q_ref[...].reshape(nch, CHUNK, ML_DK) * (ML_DK ** -0.5)
    k3 = k_ref[...].reshape(nch, CHUNK, ML_DK)
    v3 = v_ref[...].reshape(nch, CHUNK, ML_DV)
    s = _bmm_nt(q3, k3) * dmat
    inter = jnp.exp(bc + m_prev - m_t)
    bc_end = _last_rows(bc, g, lg)
    m_new = _last_rows(m_t, g, lg)
    w_col = jnp.exp(ic - bc + bc_end - m_new)
    dec = jnp.exp(bc_end + m_prev - m_new)
    sv = _bmm(s, v3)
    ssum = jnp.sum(s, axis=2, keepdims=True)
    vw = v3 * w_col
    kw = k3 * w_col
    floor = jnp.exp(-m_t)

    def emit(rows, hh):
        hn = hh * lax.rsqrt(jnp.mean(hh * hh, axis=1, keepdims=True) + RMS_EPS) * nw_ref[...]
        y_ref[rows, :] = (_sigmoid(o_ref[rows, :]) * hn).astype(y_ref.dtype)

    if g == 1:
        upd = _bmm_tn(vw, k3)
        ksum = jnp.sum(kw, axis=1, keepdims=True)
        cc, nn = c_s[0], n_s[0]
        for c in range(nch):
            num = sv[c] + inter[c] * _dot_nt(q3[c], cc)
            den = ssum[c] + inter[c] * jnp.sum(q3[c] * nn, axis=1, keepdims=True)
            emit(slice(c * CHUNK, (c + 1) * CHUNK), num / jnp.maximum(jnp.abs(den), floor[c]))
            cc = dec[c] * cc + upd[c]
            nn = dec[c] * nn + ksum[c]
        c_s[0] = cc
        n_s[0] = nn
    else:
        rows_i = lax.broadcasted_iota(jnp.int32, (CHUNK, 1), 0)
        q2, k2 = q3[0], k3[0]
        qc = jnp.concatenate([_dot_nt(q2[i * lg:(i + 1) * lg], c_s[i]) for i in range(g)], axis=0)
        qn = jnp.concatenate([jnp.sum(q2[i * lg:(i + 1) * lg] * n_s[i], axis=1, keepdims=True)
                              for i in range(g)], axis=0)
        num = sv[0] + inter[0] * qc
        den = ssum[0] + inter[0] * qn
        emit(slice(0, CHUNK), num / jnp.maximum(jnp.abs(den), floor[0]))
        for i in range(g):
            dec_i = dec[0, i * lg:i * lg + 1, :]
            c_s[i] = dec_i * c_s[i] + _dot_tn(jnp.where((rows_i >> lg_shift) == i, vw[0], 0.0), k2)
            n_s[i] = dec_i * n_s[i] + jnp.sum(kw[0, i * lg:(i + 1) * lg], axis=0, keepdims=True)
            m_s[i] = jnp.broadcast_to(m_new[0, i * lg:i * lg + 1, :], (1, LANE))

    @pl.when(t == pl.num_programs(2) - 1)
    def _():
        c_out[:, 0] = c_s[...]
        n_out[:, 0] = n_s[...]
        m_out[:, 0] = m_s[:, :, 0:1]


def _gdn_body(r, g, q_ref, k_ref, v_ref, z_ref, gt_ref, cwq_ref, cwk_ref, cwv_ref, al_ref, dtb_ref,
              nw_ref, bq_ref, bk_ref, bv_ref, s0_ref, y_ref, s_out, tq_out, tk_out, tv_out,
              s_s, pq_s, pk_s, pv_s):
    lg = CHUNK // g
    lg_shift = int(math.log2(lg))
    lseq = r if g == 1 else lg
    nch = r // CHUNK
    h = pl.program_id(1)
    t = pl.program_id(2)

    @pl.when(t == 0)
    def _():
        s_s[...] = s0_ref[:, 0]
        pq_s[...] = bq_ref[...]
        pk_s[...] = bk_ref[...]
        pv_s[...] = bv_ref[...]

    qa, tq = _conv_rows(q_ref[...], pq_s[...], cwq_ref[...], g, lseq)
    ka, tk = _conv_rows(k_ref[...], pk_s[...], cwk_ref[...], g, lseq)
    va, tv = _conv_rows(v_ref[...], pv_s[...], cwv_ref[...], g, lseq)
    pq_s[...] = tq
    pk_s[...] = tk
    pv_s[...] = tv
    qa, ka, va = _silu(qa), _silu(ka), _silu(va)
    qn = qa * lax.rsqrt(jnp.sum(qa * qa, axis=1, keepdims=True) + L2_EPS) * (GDN_DK ** -0.5)
    kn = ka * lax.rsqrt(jnp.sum(ka * ka, axis=1, keepdims=True) + L2_EPS)

    lane3 = lax.broadcasted_iota(jnp.int32, (nch, CHUNK, LANE), 2)
    causal, strict = _seq_masks(g, lg_shift)
    tri_b = jnp.broadcast_to(causal.astype(BF16), (nch, CHUNK, CHUNK))
    eye = (lax.broadcasted_iota(jnp.int32, (CHUNK, CHUNK), 0)
           == lax.broadcasted_iota(jnp.int32, (CHUNK, CHUNK), 1)).astype(F32)

    q3 = qn.reshape(nch, CHUNK, GDN_DK)
    k3 = kn.reshape(nch, CHUNK, GDN_DK)
    v3 = va.reshape(nch, CHUNK, GDN_DV)
    gt = gt_ref[...].reshape(nch, CHUNK, LANE)
    beta = jnp.sum(jnp.where(lane3 == G_GDB + h, _sigmoid(gt), 0.0), axis=2, keepdims=True)
    gval = -jnp.exp(al_ref[...]) * _softplus(gt + dtb_ref[...])
    gc = _cumsum_rows(jnp.sum(jnp.where(lane3 == G_GDA + h, gval, 0.0), axis=2, keepdims=True),
                      tri_b, lane3)
    decay = jnp.exp(jnp.where(causal, _outer_sum(gc, -gc, lane3), -jnp.inf))
    p = jnp.where(strict, -(beta * _bmm_nt(k3, k3) * decay), 0.0)
    inv = eye + p
    pk = p
    for _ in range(1, lg_shift):
        pk = _bmm3(pk, pk)
        inv = inv + _bmm3(inv, pk)
    egc = jnp.exp(gc)
    u_pre = _bmm2(inv, beta * v3)
    w = _bmm2(inv, (beta * egc) * k3)
    qk = _bmm_nt(q3, k3) * decay
    q_dec = q3 * egc
    g_end = _last_rows(gc, g, lg)
    k_dec = k3 * jnp.exp(g_end - gc)

    def emit(rows, o):
        hn = o * lax.rsqrt(jnp.mean(o * o, axis=1, keepdims=True) + RMS_EPS) * nw_ref[...]
        y_ref[rows, :] = (hn * _silu(z_ref[rows, :])).astype(y_ref.dtype)

    if g == 1:
        st = s_s[0]
        for c in range(nch):
            u = u_pre[c] - _dot(w[c], st)
            emit(slice(c * CHUNK, (c + 1) * CHUNK), _dot(q_dec[c], st) + _dot(qk[c], u))
            st = jnp.exp(g_end[c]) * st + _dot_tn(k_dec[c], u)
        s_s[0] = st
    else:
        rows_i = lax.broadcasted_iota(jnp.int32, (CHUNK, 1), 0)
        us, os_ = [], []
        for i in range(g):
            sl = slice(i * lg, (i + 1) * lg)
            us.append(u_pre[0, sl] - _dot(w[0, sl], s_s[i]))
            os_.append(_dot(q_dec[0, sl], s_s[i]))
        u = jnp.concatenate(us, axis=0)
        emit(slice(0, CHUNK), jnp.concatenate(os_, axis=0) + _dot(qk[0], u))
        for i in range(g):
            kdi = jnp.where((rows_i >> lg_shift) == i, k_dec[0], 0.0)
            s_s[i] = jnp.exp(g_end[0, i * lg:i * lg + 1, :]) * s_s[i] + _dot_tn(kdi, u)

    @pl.when(t == pl.num_programs(2) - 1)
    def _():
        s_out[:, 0] = s_s[...]
        tq_out[...] = tq
        tk_out[...] = tk
        tv_out[...] = tv


def _rglru_body(r, g, x_ref, yg_ref, cw_ref, cb_ref, wa_ref, wx_ref, ba_ref, bx_ref, lam_ref,
                buf_ref, h0_ref, y_ref, h_out, tail_out, h_s, px_s):
    lseq = r if g == 1 else CHUNK // g
    t = pl.program_id(2)

    @pl.when(t == 0)
    def _():
        h_s[...] = h0_ref[:, 0]
        px_s[...] = buf_ref[...]

    xc, tail = _conv_rows(x_ref[...], px_s[...], cw_ref[...], g, lseq)
    px_s[...] = tail
    xc = xc + cb_ref[...]
    rg = _sigmoid(_dot(xc, wa_ref[0]) + ba_ref[...])
    ig = _sigmoid(_dot(xc, wx_ref[0]) + bx_ref[...])
    log_a = -LRU_C * rg * _softplus(-lam_ref[...])
    a = jnp.exp(log_a)
    u = jnp.sqrt(1.0 - jnp.exp(2.0 * log_a)) * (ig * xc)
    pos = lax.broadcasted_iota(jnp.int32, (r, 1), 0) & (lseq - 1)
    sh = 1
    while sh < lseq:
        a_sh = pltpu.roll(a, sh, 0)
        u_sh = pltpu.roll(u, sh, 0)
        msk = pos >= sh
        u = jnp.where(msk, a * u_sh + u, u)
        a = jnp.where(msk, a * a_sh, a)
        sh *= 2
    hh = u + a * _per_seq_rows([h_s[i] for i in range(g)], g, lseq)
    for i in range(g):
        h_s[i] = hh[(i + 1) * lseq - 1:(i + 1) * lseq, :]
    yg = yg_ref[...]
    gelu = 0.5 * yg * (1.0 + jnp.tanh(math.sqrt(2.0 / math.pi) * (yg + 0.044715 * (yg * yg * yg))))
    y_ref[...] = (gelu * hh).astype(y_ref.dtype)

    @pl.when(t == pl.num_programs(2) - 1)
    def _():
        h_out[:, 0] = h_s[...]
        tail_out[...] = tail


class _Group:
    def __init__(self, n_seq, seq_len, row0):
        self.n_seq, self.seq_len, self.row0 = n_seq, seq_len, row0
        self.rows = n_seq * seq_len
        if seq_len >= 4 * CHUNK:
            self.r, self.g = 4 * CHUNK, 1
        elif seq_len >= CHUNK:
            self.r, self.g = CHUNK, 1
        else:
            self.r, self.g = CHUNK, CHUNK // seq_len
        self.s = self.g
        self.nb = n_seq // self.s
        self.nt = seq_len // self.r if self.g == 1 else 1
        assert seq_len >= CONV_W - 1 and row0 % self.r == 0 and self.rows % self.r == 0
        assert n_seq % self.s == 0 and self.r & (self.r - 1) == 0
        self.rb0 = row0 // self.r

    def rb(self, b, t):
        return self.rb0 + b * self.nt + t


def _mixer_call(body, grp, l, depth, m_total, in_specs, args, y_width, y_block, state_shapes, extra_outs,
                scratch, prev, name):
    s = grp.s
    out_shape = [jax.ShapeDtypeStruct((m_total, y_width), BF16)]
    out_specs = [pl.BlockSpec((grp.r, y_block), lambda b, h, t: (grp.rb(b, t), h))]
    for tail in state_shapes:
        out_shape.append(jax.ShapeDtypeStruct((depth, grp.n_seq, y_width // y_block) + tail, F32))
        out_specs.append(pl.BlockSpec((None, s, 1) + tail, lambda b, h, t: (l, b, h) + (0,) * len(tail)))
    for shp, spec in extra_outs:
        out_shape.append(shp)
        out_specs.append(spec)
    n_in = len(args)
    aliases = {}
    in_specs = list(in_specs)
    args = list(args)
    for j, arr in enumerate(prev):
        if arr is not None:
            aliases[len(args)] = j
            in_specs.append(pl.BlockSpec(memory_space=pl.ANY))
            args.append(arr)
    n_alias = len(args) - n_in

    def wrapped(*refs):
        body(*refs[:n_in], *refs[n_in + n_alias:])

    return pl.pallas_call(
        wrapped,
        out_shape=out_shape,
        grid=(grp.nb, y_width // y_block, grp.nt),
        in_specs=in_specs,
        out_specs=out_specs,
        scratch_shapes=scratch,
        input_output_aliases=aliases,
        compiler_params=_cparams(("arbitrary",) * 3),
        name=name,
    )(*args)


def _mlstm_call(grp, l, depth, m_total, hmain, gates, gate_bias, norm_w, c0, n0, m0, prev):
    r, s = grp.r, grp.s
    row = lambda off, w: pl.BlockSpec((r, w), lambda b, h, t: (grp.rb(b, t), off // w + h))
    st = lambda *tail: pl.BlockSpec((s, 1) + tail, lambda b, h, t: (b, h, 0, 0))
    in_specs = [row(C_MLQ, ML_DK), row(C_MLK, ML_DK), row(C_MLV, ML_DV), row(C_MLO, ML_DV),
                pl.BlockSpec((r, LANE), lambda b, h, t: (grp.rb(b, t), 0)),
                pl.BlockSpec((1, LANE), lambda b, h, t: (0, 0)),
                pl.BlockSpec((1, ML_DV), lambda b, h, t: (0, h)),
                st(ML_DV, ML_DK), st(1, ML_DK), st(1, 1)]
    scratch = [pltpu.VMEM((s, ML_DV, ML_DK), F32), pltpu.VMEM((s, 1, ML_DK), F32),
               pltpu.VMEM((s, 1, LANE), F32)]
    return _mixer_call(functools.partial(_mlstm_body, r, grp.g), grp, l, depth, m_total, in_specs,
                       [hmain, hmain, hmain, hmain, gates, gate_bias, norm_w, c0, n0, m0],
                       ML_HEADS * ML_DV, ML_DV, [(ML_DV, ML_DK), (1, ML_DK), (1, 1)], [], scratch, prev, "mlstm")


def _gdn_call(grp, l, depth, m_total, hmain, gates, conv_w, a_log_row, dt_bias_row, norm_w, bufpad, s0, prev):
    r, s = grp.r, grp.s
    row = lambda off, w: pl.BlockSpec((r, w), lambda b, h, t: (grp.rb(b, t), off // w + h))
    cw = lambda off, w: pl.BlockSpec((CONV_W, w), lambda b, h, t: (0, off // w + h))
    bf = lambda off, w: pl.BlockSpec((s * SUBLANE, w), lambda b, h, t: (b, off // w + h))
    one = pl.BlockSpec((1, LANE), lambda b, h, t: (0, 0))
    in_specs = [row(C_GDQ, GDN_DK), row(C_GDK, GDN_DK), row(C_GDV, GDN_DV), row(C_GDZ, GDN_DV),
                pl.BlockSpec((r, LANE), lambda b, h, t: (grp.rb(b, t), 0)),
                cw(0, GDN_DK), cw(GDN_QK, GDN_DK), cw(2 * GDN_QK, GDN_DV),
                one, one, pl.BlockSpec((1, GDN_DV), lambda b, h, t: (0, 0)),
                bf(0, GDN_DK), bf(GDN_QK, GDN_DK), bf(2 * GDN_QK, GDN_DV),
                pl.BlockSpec((s, 1, GDN_DK, GDN_DV), lambda b, h, t: (b, h, 0, 0))]
    nrow = grp.n_seq * SUBLANE
    tail = lambda w: (jax.ShapeDtypeStruct((nrow, GDN_HEADS * w), F32),
                      pl.BlockSpec((s * SUBLANE, w), lambda b, h, t: (b, h)))
    scratch = [pltpu.VMEM((s, GDN_DK, GDN_DV), F32),
               pltpu.VMEM((s * SUBLANE, GDN_DK), F32), pltpu.VMEM((s * SUBLANE, GDN_DK), F32),
               pltpu.VMEM((s * SUBLANE, GDN_DV), F32)]
    return _mixer_call(functools.partial(_gdn_body, r, grp.g), grp, l, depth, m_total, in_specs,
                       [hmain, hmain, hmain, hmain, gates, conv_w, conv_w, conv_w, a_log_row, dt_bias_row,
                        norm_w, bufpad, bufpad, bufpad, s0],
                       GDN_HEADS * GDN_DV, GDN_DV, [(GDN_DK, GDN_DV)],
                       [tail(GDN_DK), tail(GDN_DK), tail(GDN_DV)], scratch, prev, "gdn")


def _rglru_call(grp, l, depth, m_total, hmain, conv_w, conv_b, w_a, w_x, b_a, b_x, lam, bufpad, h0, prev):
    r, s = grp.r, grp.s
    w = LRU_BLOCK
    row = lambda off: pl.BlockSpec((r, w), lambda b, h, t: (grp.rb(b, t), off // w + h))
    vec = pl.BlockSpec((1, w), lambda b, h, t: (0, h))
    blk = pl.BlockSpec((1, w, w), lambda b, h, t: (h, 0, 0))
    in_specs = [row(C_RGX), row(C_RGY),
                pl.BlockSpec((CONV_W, w), lambda b, h, t: (0, h)), vec, blk, blk, vec, vec, vec,
                pl.BlockSpec((s * SUBLANE, w), lambda b, h, t: (b, h)),
                pl.BlockSpec((s, 1, 1, w), lambda b, h, t: (b, h, 0, 0))]
    tail = (jax.ShapeDtypeStruct((grp.n_seq * SUBLANE, LRU_WIDTH), F32),
            pl.BlockSpec((s * SUBLANE, w), lambda b, h, t: (b, h)))
    scratch = [pltpu.VMEM((s, 1, w), F32), pltpu.VMEM((s * SUBLANE, w), F32)]
    return _mixer_call(functools.partial(_rglru_body, r, grp.g), grp, l, depth, m_total, in_specs,
                       [hmain, hmain, conv_w, conv_b, w_a, w_x, b_a, b_x, lam, bufpad, h0],
                       LRU_WIDTH, w, [(1, w)], [tail], scratch, prev, "rglru")


def _pad_conv_state(buf):
    b, k, c = buf.shape
    return jnp.pad(buf.astype(F32), ((0, 0), (SUBLANE - k, 0), (0, 0))).reshape(b * SUBLANE, c)


def _unpad_conv_state(tail, n_seq):
    return tail.reshape(n_seq, SUBLANE, -1)[:, SUBLANE - (CONV_W - 1):, :]


def _router_call(x, router, tm):
    m, d = x.shape
    wr = jnp.pad(router.astype(F32), ((0, 0), (0, LANE - N_EXPERTS)))

    def body(x_ref, w_ref, o_ref):
        logits = _dot_hi(x_ref[...], w_ref[...])
        lane = lax.broadcasted_iota(jnp.int32, logits.shape, 1)
        lg = jnp.where(lane < N_EXPERTS, logits, -jnp.inf)
        v1 = jnp.max(lg, axis=1, keepdims=True)
        i1 = jnp.min(jnp.where(lg == v1, lane, LANE), axis=1, keepdims=True)
        lg2 = jnp.where(lane == i1, -jnp.inf, lg)
        v2 = jnp.max(lg2, axis=1, keepdims=True)
        i2 = jnp.min(jnp.where(lg2 == v2, lane, LANE), axis=1, keepdims=True)
        e2 = jnp.exp(v2 - v1)
        w1 = 1.0 / (1.0 + e2)
        w2 = e2 / (1.0 + e2)
        o_ref[...] = jnp.where(lane == i1, w1, jnp.where(lane == i2, w2, 0.0))

    return pl.pallas_call(
        body,
        out_shape=jax.ShapeDtypeStruct((m, LANE), F32),
        grid=(m // tm,),
        in_specs=[pl.BlockSpec((tm, d), lambda i: (i, 0)), pl.BlockSpec((d, LANE), lambda i: (0, 0))],
        out_specs=pl.BlockSpec((tm, LANE), lambda i: (i, 0)),
        compiler_params=_cparams(("arbitrary",)),
        name="router",
    )(x, wr)


def _run_group_mixers(grp, l, depth, m_total, hmain, gates, st, prm, prev):
    b = grp.n_seq
    ya, ml_c, ml_n, ml_m = _mlstm_call(
        grp, l, depth, m_total, hmain, gates, prm["gate_bias"], prm["ml_norm_w"],
        st["ml_C"][l].astype(F32), st["ml_n"][l].astype(F32).reshape(b, ML_HEADS, 1, ML_DK),
        st["ml_m"][l].astype(F32).reshape(b, ML_HEADS, 1, 1),
        [prev["ya"], prev["ml_C"], prev["ml_n"], prev["ml_m"]])
    yb, rg_h, rg_tail = _rglru_call(
        grp, l, depth, m_total, hmain, prm["rg_conv_w"], prm["rg_conv_b"], prm["rg_w_a"], prm["rg_w_x"],
        prm["rg_b_a"], prm["rg_b_x"], prm["rg_lambda"], _pad_conv_state(st["rg_conv"][l]),
        st["rg_h"][l].astype(F32).reshape(b, LRU_BLOCKS, 1, LRU_BLOCK), [prev["yb"], prev["rg_h"]])
    yc, gd_s, tq, tk, tv = _gdn_call(
        grp, l, depth, m_total, hmain, gates, prm["gd_conv_w"], prm["gd_a_log"], prm["gd_dt_bias"],
        prm["gd_norm_w"], _pad_conv_state(st["gd_conv"][l]), st["gd_S"][l].astype(F32),
        [prev["yc"], prev["gd_S"]])
    out = dict(ya=ya, yb=yb, yc=yc, ml_C=ml_c, ml_n=ml_n, ml_m=ml_m, rg_h=rg_h, gd_S=gd_s)
    tails = dict(rg_conv=_unpad_conv_state(rg_tail, b),
                 gd_conv=jnp.concatenate([_unpad_conv_state(x, b) for x in (tq, tk, tv)], axis=-1))
    return out, tails


_STATE_KEYS = ("ml_C", "ml_n", "ml_m", "rg_h", "gd_S")


def kernel(x_prompt, x_sample, state_mlstm_C, state_mlstm_n, state_mlstm_m, state_rglru_h, state_rglru_conv, state_gdn_S, state_gdn_conv, p_prompt, p_sample, w_in, ml_b_i, ml_b_f, ml_norm_w, rg_conv_w, rg_conv_b, rg_w_a, rg_b_a, rg_w_x, rg_b_x, rg_lambda, gd_conv_w, gd_A_log, gd_dt_bias, gd_norm_w, w_up_mlstm, w_up_rglru, w_up_gdn, w_out, ln1_g, ln1_b, ffn_w1, ffn_w3, ffn_w2, moe_router, moe_w1, moe_w3, moe_w2, ple_w, ple_gate_w, ln2_g, ln2_b):
    bp, tp, d = x_prompt.shape
    bs, ts, _ = x_sample.shape
    depth = w_in.shape[0]
    gp = _Group(bp, tp, 0)
    gs = _Group(bs, ts, bp * tp)
    m = gp.rows + gs.rows
    pd = x_prompt.dtype

    zeros = lambda *shape: jnp.zeros(shape, F32)
    st_p = dict(ml_C=zeros(depth, bp, ML_HEADS, ML_DV, ML_DK), ml_n=zeros(depth, bp, ML_HEADS, ML_DK),
                ml_m=zeros(depth, bp, ML_HEADS), rg_h=zeros(depth, bp, LRU_WIDTH),
                rg_conv=zeros(depth, bp, CONV_W - 1, LRU_WIDTH),
                gd_S=zeros(depth, bp, GDN_HEADS, GDN_DK, GDN_DV),
                gd_conv=zeros(depth, bp, CONV_W - 1, GDN_CONV_DIM))
    st_s = dict(ml_C=state_mlstm_C, ml_n=state_mlstm_n, ml_m=state_mlstm_m, rg_h=state_rglru_h,
                rg_conv=state_rglru_conv, gd_S=state_gdn_S, gd_conv=state_gdn_conv)

    x = jnp.concatenate([x_prompt.reshape(gp.rows, d), x_sample.reshape(gs.rows, d)], axis=0).astype(F32)
    xb = x.astype(BF16)
    ident = lambda accs, extras, ids: [accs[0]]
    lane_row = lambda v, off: jnp.zeros((1, LANE), F32).at[0, off:off + v.shape[0]].set(v.astype(F32))
    acc_p = {k: None for k in _STATE_KEYS}
    acc_s = {k: None for k in _STATE_KEYS}
    tails_p, tails_s = [], []

    for l in range(depth):
        w = w_in[l]
        hmain = _in_proj(xb, w, 1024)
        w_gate = jnp.pad(jnp.concatenate([w[:, W_GATES_A:W_GATES_A + SUBLANE], w[:, W_GATES_B:W_GATES_B + SUBLANE]],
                                         axis=1), ((0, 0), (0, LANE - 2 * SUBLANE)))
        gates = _proj([xb], [w_gate], [pl.BlockSpec((d, LANE), lambda n, i: (0, n))], [], [], ident,
                      [jax.ShapeDtypeStruct((m, LANE), F32)], LANE, 1024, LANE, "gate_proj")[0]

        prm = dict(
            gate_bias=lane_row(ml_b_i[l], G_MLI) + lane_row(ml_b_f[l], G_MLF),
            ml_norm_w=ml_norm_w[l].astype(F32).reshape(1, -1),
            rg_conv_w=rg_conv_w[l].astype(F32), rg_conv_b=rg_conv_b[l].astype(F32).reshape(1, -1),
            rg_w_a=rg_w_a[l], rg_w_x=rg_w_x[l], rg_b_a=rg_b_a[l].astype(F32).reshape(1, -1),
            rg_b_x=rg_b_x[l].astype(F32).reshape(1, -1), rg_lambda=rg_lambda[l].astype(F32).reshape(1, -1),
            gd_conv_w=gd_conv_w[l].astype(F32), gd_a_log=lane_row(gd_A_log[l], G_GDA),
            gd_dt_bias=lane_row(gd_dt_bias[l], G_GDA), gd_norm_w=gd_norm_w[l].astype(F32).reshape(1, -1))
        out_p, tl_p = _run_group_mixers(gp, l, depth, m, hmain, gates, st_p, prm,
                                        dict(acc_p, ya=None, yb=None, yc=None))
        out_s, tl_s = _run_group_mixers(gs, l, depth, m, hmain, gates, st_s, prm,
                                        dict(acc_s, ya=out_p["ya"], yb=out_p["yb"], yc=out_p["yc"]))
        acc_p = {k: out_p[k] for k in _STATE_KEYS}
        acc_s = {k: out_s[k] for k in _STATE_KEYS}
        tails_p.append(tl_p)
        tails_s.append(tl_s)
        ya, yb, yc = out_s["ya"], out_s["yb"], out_s["yc"]

        tn, tm = 512, 512
        up_spec = pl.BlockSpec((ya.shape[1], tn), lambda n, i: (0, n))
        mg_spec = lambda br: pl.BlockSpec((tm, tn), lambda n, i: (i, (C_MG + br * d) // tn + n))
        merged = _proj([ya, yb, yc], [w_up_mlstm[l], w_up_rglru[l], w_up_gdn[l]], [up_spec] * 3,
                       [hmain] * 3, [mg_spec(0), mg_spec(1), mg_spec(2)],
                       lambda accs, ex, ids: [_sigmoid(ex[0]) * accs[0] + _sigmoid(ex[1]) * accs[1]
                                              + _sigmoid(ex[2]) * accs[2]],
                       [jax.ShapeDtypeStruct((m, d), BF16)], d, tm, tn, "merge")[0]
        x, xb = _rowfull_ln(merged, w_out[l].astype(BF16), x, None, ln1_g[l].astype(F32), ln1_b[l].astype(F32),
                            512, d, "out_proj_ln")

        p_l = jnp.concatenate([p_prompt[l].reshape(gp.rows, -1), p_sample[l].reshape(gs.rows, -1)], axis=0)
        ple = _proj([xb, p_l], [ple_gate_w[l], ple_w[l]],
                    [pl.BlockSpec((d, 1024), lambda n, i: (0, n)), pl.BlockSpec((P_DIM, 1024), lambda n, i: (0, n))],
                    [], [], lambda accs, ex, ids: [_sigmoid(accs[0]) * accs[1]],
                    [jax.ShapeDtypeStruct((m, d), F32)], d, 1024, 1024, "ple")[0]
        j = l // 2
        if l % 2 == 0:
            ff = ffn_w1.shape[2]
            hmid = _proj([xb], [ffn_w1[j], ffn_w3[j]], [pl.BlockSpec((d, 512), lambda n, i: (0, n))] * 2,
                         [], [], lambda accs, ex, ids: [_silu(accs[0]) * accs[1]],
                         [jax.ShapeDtypeStruct((m, ff), BF16)], ff, 1024, 512, "ffn_up", x_of_w=(0, 0))[0]
            w2 = ffn_w2[j].astype(BF16)
        else:
            fe = moe_w1.shape[3]
            ne = moe_w1.shape[1]
            rgate = _router_call(x, moe_router[j], 512)
            tn = 256
            wspec = pl.BlockSpec((None, d, tn), lambda e, n, i: (e, 0, n))

            def moe_epi(accs, ex, ids):
                lane = lax.broadcasted_iota(jnp.int32, ex[0].shape, 1)
                gate = jnp.sum(jnp.where(lane == ids[0], ex[0], 0.0), axis=1, keepdims=True)
                return [gate * (_silu(accs[0]) * accs[1])]

            hmid = _proj([xb], [moe_w1[j], moe_w3[j]], [wspec, wspec], [rgate],
                         [pl.BlockSpec((1024, LANE), lambda e, n, i: (i, 0))], moe_epi,
                         [jax.ShapeDtypeStruct((m, ne * fe), BF16)], fe, 1024, tn, "moe_up",
                         lead_grid=(ne,), x_of_w=(0, 0))[0]
            w2 = moe_w2[j].reshape(ne * fe, d).astype(BF16)
        last = l == depth - 1
        x, xb = _rowfull_ln(hmid, w2, x, ple, ln2_g[l].astype(F32), ln2_b[l].astype(F32), 512, 1408,
                            "ffn_down_ln", split_rows=gp.rows if last else None)

    y_prompt = x.reshape(bp, tp, d).astype(pd)
    y_sample = xb.reshape(bs, ts, d).astype(x_sample.dtype)

    def finish(acc, tails, n_seq, dtypes):
        conv = {k: jnp.stack([tl[k] for tl in tails]) for k in ("rg_conv", "gd_conv")}
        vals = dict(ml_C=acc["ml_C"], ml_n=acc["ml_n"].reshape(depth, n_seq, ML_HEADS, ML_DK),
                    ml_m=acc["ml_m"].reshape(depth, n_seq, ML_HEADS),
                    rg_h=acc["rg_h"].reshape(depth, n_seq, LRU_WIDTH), rg_conv=conv["rg_conv"],
                    gd_S=acc["gd_S"], gd_conv=conv["gd_conv"])
        names = ("ml_C", "ml_n", "ml_m", "rg_h", "rg_conv", "gd_S", "gd_conv")
        return tuple(vals[k].astype(dtypes[k]) for k in names)

    return ((y_prompt, y_sample) + finish(acc_p, tails_p, bp, {k: pd for k in st_p})
            + finish(acc_s, tails_s, bs, {k: v.dtype for k, v in st_s.items()}))
```

```python
import functools
import math

import jax
import jax.numpy as jnp
from jax import lax
from jax.experimental import pallas as pl
from jax.experimental.pallas import tpu as pltpu

F32 = jnp.float32
BF16 = jnp.bfloat16

D_MODEL = 2048
DEPTH = 2
P_DIM = 256
CONV_W = 4
N_BRANCH = 3
ML_HEADS = 4
ML_DK = 128
ML_DV = 256
ML_GATE_CAP = 15.0
LRU_WIDTH = 1024
LRU_BLOCKS = 4
LRU_BLOCK = LRU_WIDTH // LRU_BLOCKS
LRU_C = 8.0
GDN_HEADS = 4
GDN_DK = 128
GDN_DV = 256
GDN_QK = GDN_HEADS * GDN_DK
GDN_CONV_DIM = 2 * GDN_QK + GDN_HEADS * GDN_DV
N_EXPERTS = 8
DEEPNORM_ALPHA = (2 * DEPTH) ** 0.25
LN_EPS = 1e-5
RMS_EPS = 1e-6
L2_EPS = 1e-6

LANE = 128
SUBLANE = 8
CHUNK = 128
VMEM_LIMIT = 56 * 1024 * 1024

C_MLQ, C_MLK, C_MLV, C_MLO = 0, 512, 1024, 2048
C_RGX, C_RGY = 3072, 4096
C_GDQ, C_GDK, C_GDV, C_GDZ = 5120, 5632, 6144, 7168
C_MG = 8192
N_MAIN = C_MG + N_BRANCH * D_MODEL
W_GATES_A, W_GATES_B = 2048, 7176
G_MLI, G_MLF, G_GDB, G_GDA = 0, 4, 8, 12
PROJ_TN = 1024

HIGHEST = lax.Precision.HIGHEST


def _cparams(sem):
    return pltpu.CompilerParams(dimension_semantics=sem, vmem_limit_bytes=VMEM_LIMIT)


def _dot(a, b):
    return jnp.dot(a.astype(BF16), b.astype(BF16), preferred_element_type=F32)


def _dot_nt(a, b):
    return lax.dot_general(a.astype(BF16), b.astype(BF16), (((1,), (1,)), ((), ())),
                           preferred_element_type=F32)


def _dot_tn(a, b):
    return lax.dot_general(a.astype(BF16), b.astype(BF16), (((0,), (0,)), ((), ())),
                           preferred_element_type=F32)


def _dot_hi(a, b):
    return jnp.dot(a, b, precision=HIGHEST, preferred_element_type=F32)


def _bmm(a, b):
    return jnp.einsum('cij,cjk->cik', a.astype(BF16), b.astype(BF16), preferred_element_type=F32)


def _bmm_nt(a, b):
    return jnp.einsum('cik,cjk->cij', a.astype(BF16), b.astype(BF16), preferred_element_type=F32)


def _bmm_tn(a, b):
    return jnp.einsum('csi,csj->cij', a.astype(BF16), b.astype(BF16), preferred_element_type=F32)


def _sigmoid(x):
    return 1.0 / (1.0 + jnp.exp(-x))


def _silu(x):
    return x * _sigmoid(x)


def _softplus(x):
    return jnp.maximum(x, 0.0) + jnp.log1p(jnp.exp(-jnp.abs(x)))


def _proj(xs, ws, w_specs, extras, e_specs, epilogue, out_shapes, n_cols, tm, tn, name,
          lead_grid=(), x_of_w=None):
    m = xs[0].shape[0]
    nl = len(lead_grid)
    grid = tuple(lead_grid) + (n_cols // tn, m // tm)
    x_specs = [pl.BlockSpec((tm, x.shape[1]), lambda *g: (g[-1], 0)) for x in xs]
    o_specs = [pl.BlockSpec((tm, tn), lambda *g: (g[-1], g[-2]) if nl == 0 else
                            (g[-1], g[0] * (n_cols // tn) + g[-2])) for _ in out_shapes]
    scratch = [pltpu.VMEM(tuple(d for d in s.block_shape if d is not None), BF16) for s in w_specs]
    xw = tuple(range(len(ws))) if x_of_w is None else tuple(x_of_w)

    def body(*refs):
        nx, nw, ne, no = len(xs), len(ws), len(extras), len(out_shapes)
        x_refs = refs[:nx]
        w_refs = refs[nx:nx + nw]
        e_refs = refs[nx + nw:nx + nw + ne]
        o_refs = refs[nx + nw + ne:nx + nw + ne + no]
        wb_refs = refs[nx + nw + ne + no:]

        @pl.when(pl.program_id(nl + 1) == 0)
        def _():
            for w, wb in zip(w_refs, wb_refs):
                wb[...] = w[...].astype(BF16)

        xv = [x[...].astype(BF16) for x in x_refs]
        accs = [jnp.dot(xv[j], wb[...], preferred_element_type=F32) for j, wb in zip(xw, wb_refs)]
        outs = epilogue(accs, [e[...] for e in e_refs], [pl.program_id(i) for i in range(nl)])
        for o, v in zip(o_refs, outs):
            o[...] = v.astype(o.dtype)

    return pl.pallas_call(
        body,
        out_shape=out_shapes,
        grid=grid,
        in_specs=x_specs + list(w_specs) + list(e_specs),
        out_specs=o_specs,
        scratch_shapes=scratch,
        compiler_params=_cparams(("arbitrary",) * len(grid)),
        name=name,
    )(*xs, *ws, *extras)


def _in_proj(xb, w, l, tm):
    m, d = xb.shape
    tn = PROJ_TN
    n_blocks = N_MAIN // tn
    first_a = W_GATES_A // tn
    first_b = (W_GATES_B - SUBLANE) // tn

    def body(x_ref, wm_ref, wn_ref, o_ref, wb_ref):
        n = pl.program_id(0)

        @pl.when(pl.program_id(1) == 0)
        def _():
            def shifted(s):
                wcat = jnp.concatenate([wm_ref[...], wn_ref[...]], axis=1)
                wb_ref[...] = wcat[:, s:s + tn].astype(BF16)

            @pl.when(n < first_a)
            def _():
                wb_ref[...] = wm_ref[...].astype(BF16)

            @pl.when((n >= first_a) & (n < first_b))
            def _():
                shifted(SUBLANE)

            @pl.when(n >= first_b)
            def _():
                shifted(2 * SUBLANE)

        o_ref[...] = jnp.dot(x_ref[...], wb_ref[...], preferred_element_type=F32)

    return pl.pallas_call(
        body,
        out_shape=jax.ShapeDtypeStruct((m, N_MAIN), F32),
        grid=(n_blocks, m // tm),
        in_specs=[pl.BlockSpec((tm, d), lambda n, i: (i, 0)),
                  pl.BlockSpec((None, d, tn), lambda n, i: (l, 0, n)),
                  pl.BlockSpec((None, d, LANE), lambda n, i: (l, 0, (n + 1) * (tn // LANE)))],
        out_specs=pl.BlockSpec((tm, tn), lambda n, i: (i, n)),
        scratch_shapes=[pltpu.VMEM((d, tn), BF16)],
        compiler_params=_cparams(("arbitrary", "arbitrary")),
        name="in_proj",
    )(xb, w, w)


def _rowfull_ln(h, w_bf16, res, extra, gamma, beta, tm, tk, name, split_rows=None):
    m, k = h.shape
    n = w_bf16.shape[1]
    nk = k // tk
    has_extra = extra is not None
    n_in = 6 if has_extra else 5
    tiles_a = None if split_rows is None else split_rows // tm

    def body(*refs):
        h_ref, w_ref, r_ref = refs[:3]
        e_ref = refs[3] if has_extra else None
        g_ref, b_ref = refs[n_in - 2:n_in]
        oa_ref, ob_ref, acc_ref = refs[n_in:]
        i = pl.program_id(0)
        kk = pl.program_id(1)

        @pl.when(kk == 0)
        def _():
            acc_ref[...] = jnp.zeros_like(acc_ref)

        acc_ref[...] += jnp.dot(h_ref[...], w_ref[...], preferred_element_type=F32)

        @pl.when(kk == nk - 1)
        def _():
            y = DEEPNORM_ALPHA * r_ref[...] + acc_ref[...]
            if e_ref is not None:
                y = y + e_ref[...].astype(F32)
            mu = jnp.mean(y, axis=-1, keepdims=True)
            yc = y - mu
            var = jnp.mean(yc * yc, axis=-1, keepdims=True)
            out = yc * lax.rsqrt(var + LN_EPS) * g_ref[...] + b_ref[...]
            if split_rows is None:
                oa_ref[...] = out
                ob_ref[...] = out.astype(BF16)
            else:
                @pl.when(i < tiles_a)
                def _():
                    oa_ref[...] = out

                @pl.when(i >= tiles_a)
                def _():
                    ob_ref[...] = out

    row = lambda i, j: (i, 0)
    in_specs = [pl.BlockSpec((tm, tk), lambda i, j: (i, j)),
                pl.BlockSpec((tk, n), lambda i, j: (j, 0)),
                pl.BlockSpec((tm, n), row)]
    args = [h, w_bf16, res]
    if has_extra:
        in_specs.append(pl.BlockSpec((tm, n), row))
        args.append(extra)
    in_specs += [pl.BlockSpec((1, n), lambda i, j: (0, 0))] * 2
    args += [gamma.reshape(1, n), beta.reshape(1, n)]
    if split_rows is None:
        out_shape = [jax.ShapeDtypeStruct((m, n), F32), jax.ShapeDtypeStruct((m, n), BF16)]
        out_specs = [pl.BlockSpec((tm, n), row), pl.BlockSpec((tm, n), row)]
    else:
        out_shape = [jax.ShapeDtypeStruct((split_rows, n), F32), jax.ShapeDtypeStruct((m - split_rows, n), F32)]
        out_specs = [pl.BlockSpec((tm, n), lambda i, j: (jnp.minimum(i, tiles_a - 1), 0)),
                     pl.BlockSpec((tm, n), lambda i, j: (jnp.maximum(i - tiles_a, 0), 0))]
    return pl.pallas_call(
        body,
        out_shape=out_shape,
        grid=(m // tm, nk),
        in_specs=in_specs,
        out_specs=out_specs,
        scratch_shapes=[pltpu.VMEM((tm, n), F32)],
        compiler_params=_cparams(("arbitrary", "arbitrary")),
        name=name,
    )(*args)


def _seq_masks(g, lg_shift):
    row = lax.broadcasted_iota(jnp.int32, (CHUNK, CHUNK), 0)
    col = lax.broadcasted_iota(jnp.int32, (CHUNK, CHUNK), 1)
    if g == 1:
        return col <= row, col < row
    same = (row >> lg_shift) == (col >> lg_shift)
    return same & (col <= row), same & (col < row)


def _per_seq_rows(vals, g, lg):
    if g == 1:
        return vals[0]
    return jnp.concatenate([jnp.broadcast_to(v, (lg, v.shape[1])) for v in vals], axis=0)


def _last_rows(col, g, lg):
    if g == 1:
        return col[:, CHUNK - 1:, :]
    return _per_seq_rows([col[0, (i + 1) * lg - 1:(i + 1) * lg, :] for i in range(g)], g, lg)[None]


def _split3(x):
    x1 = x.astype(BF16).astype(F32)
    r1 = x - x1
    x2 = r1.astype(BF16).astype(F32)
    x3 = (r1 - x2).astype(BF16).astype(F32)
    return x1, x2, x3


def _split2(x):
    hi = x.astype(BF16)
    return hi, (x - hi.astype(F32)).astype(BF16)


def _lanes3(col, lane_ids, first, other):
    c1, c2, c3 = _split3(col)
    x = jnp.where(lane_ids == first, c1, jnp.where(lane_ids == first + 1, c2,
                                                   jnp.where(lane_ids == first + 2, c3, 0.0)))
    return jnp.where((lane_ids >= other) & (lane_ids < other + 3), 1.0, x).astype(BF16)


def _cumsum_rows(col, tri_b, lane_ids):
    return jnp.sum(_bmm(tri_b, _lanes3(col, lane_ids, 0, LANE)), axis=-1, keepdims=True)


def _outer_sum(a_col, b_col, lane_ids):
    return _bmm_nt(_lanes3(a_col, lane_ids, 0, 3), _lanes3(b_col, lane_ids, 3, 0))


def _bmm3(a, b):
    ah, al = _split2(a)
    bh, bl = _split2(b)
    return _bmm(ah, bh) + (_bmm(ah, bl) + _bmm(al, bh))


def _bmm2(a, b):
    ah, al = _split2(a)
    bb = b.astype(BF16)
    return _bmm(ah, bb) + _bmm(al, bb)


def _conv_rows(x, prev, w, g, lg):
    f = x.shape[1]
    xx = jnp.concatenate([prev.reshape(g, SUBLANE, f), x.reshape(g, lg, f)], axis=1)
    y = xx[:, SUBLANE:, :] * w[CONV_W - 1:CONV_W, :]
    for s in range(1, CONV_W):
        y = y + xx[:, SUBLANE - s:SUBLANE - s + lg, :] * w[CONV_W - 1 - s:CONV_W - s, :]
    return y.reshape(g * lg, f), xx[:, lg:, :].reshape(g * SUBLANE, f)


def _mlstm_body(r, g, q_ref, k_ref, v_ref, o_ref, gt_ref, gb_ref, nw_ref, c0_ref, n0_ref, m0_ref,
                y_ref, c_out, n_out, m_out, c_s, n_s, m_s):
    lg = CHUNK // g
    lg_shift = int(math.log2(lg))
    nch = r // CHUNK
    h = pl.program_id(1)
    t = pl.program_id(2)

    @pl.when(t == 0)
    def _():
        c_s[...] = c0_ref[:, 0]
        n_s[...] = n0_ref[:, 0]
        m_s[...] = jnp.broadcast_to(m0_ref[:, 0], m_s.shape)

    lane3 = lax.broadcasted_iota(jnp.int32, (nch, CHUNK, LANE), 2)
    causal, _ = _seq_masks(g, lg_shift)
    tri_b = jnp.broadcast_to(causal.astype(BF16), (nch, CHUNK, CHUNK))

    pre = gt_ref[...].reshape(nch, CHUNK, LANE) + gb_ref[...]
    capd = ML_GATE_CAP * jnp.tanh(pre * (1.0 / ML_GATE_CAP))
    logsig = jnp.minimum(capd, 0.0) - jnp.log1p(jnp.exp(-jnp.abs(capd)))
    ic = jnp.sum(jnp.where(lane3 == G_MLI + h, capd, 0.0), axis=2, keepdims=True)
    lf = jnp.sum(jnp.where(lane3 == G_MLF + h, logsig, 0.0), axis=2, keepdims=True)
    bc = _cumsum_rows(lf, tri_b, lane3)
    dm = jnp.where(causal, _outer_sum(bc, ic - bc, lane3), -jnp.inf)
    rmax = jnp.max(dm, axis=2, keepdims=True)

    if g == 1:
        mp = m_s[0][:, 0:1]
        m_prevs = []
        for c in range(nch):
            m_prevs.append(mp.reshape(1, 1, 1))
            mp = jnp.maximum(mp + bc[c, CHUNK - 1:, :], rmax[c, CHUNK - 1:, :])
        m_prev = jnp.concatenate(m_prevs, axis=0) if nch > 1 else m_prevs[0]
        m_s[0] = jnp.broadcast_to(mp, (1, LANE))
    else:
        m_prev = _per_seq_rows([m_s[i][:, 0:1] for i in range(g)], g, lg)[None]
    m_t = jnp.maximum(m_prev + bc, rmax)
    dmat = jnp.exp(dm - m_t)
    q3 = q_ref[...].reshape(nch, CHUNK, ML_DK) * (ML_DK ** -0.5)
    k3 = k_ref[...].reshape(nch, CHUNK, ML_DK)
    v3 = v_ref[...].reshape(nch, CHUNK, ML_DV)
    s = _bmm_nt(q3, k3) * dmat
    inter = jnp.exp(bc + m_prev - m_t)
    bc_end = _last_rows(bc, g, lg)
    m_new = _last_rows(m_t, g, lg)
    w_col = jnp.exp(ic - bc + bc_end - m_new)
    dec = jnp.exp(bc_end + m_prev - m_new)
    sv = _bmm(s, v3)
    ssum = jnp.sum(s, axis=2, keepdims=True)
    vw = v3 * w_col
    kw = k3 * w_col
    floor = jnp.exp(-m_t)

    def emit(rows, hh):
        hn = hh * lax.rsqrt(jnp.mean(hh * hh, axis=1, keepdims=True) + RMS_EPS) * nw_ref[...]
        y_ref[rows, :] = (_sigmoid(o_ref[rows, :]) * hn).astype(y_ref.dtype)

    if g == 1:
        upd = _bmm_tn(vw, k3)
        ksum = jnp.sum(kw, axis=1, keepdims=True)
        cc, nn = c_s[0], n_s[0]
        for c in range(nch):
            num = sv[c] + inter[c] * _dot_nt(q3[c], cc)
            den = ssum[c] + inter[c] * jnp.sum(q3[c] * nn, axis=1, keepdims=True)
            emit(slice(c * CHUNK, (c + 1) * CHUNK), num / jnp.maximum(jnp.abs(den), floor[c]))
            cc = dec[c] * cc + upd[c]
            nn = dec[c] * nn + ksum[c]
        c_s[0] = cc
        n_s[0] = nn
    else:
        rows_i = lax.broadcasted_iota(jnp.int32, (CHUNK, 1), 0)
        q2, k2 = q3[0], k3[0]
        qc = jnp.concatenate([_dot_nt(q2[i * lg:(i + 1) * lg], c_s[i]) for i in range(g)], axis=0)
        qn = jnp.concatenate([jnp.sum(q2[i * lg:(i + 1) * lg] * n_s[i], axis=1, keepdims=True)
                              for i in range(g)], axis=0)
        num = sv[0] + inter[0] * qc
        den = ssum[0] + inter[0] * qn
        emit(slice(0, CHUNK), num / jnp.maximum(jnp.abs(den), floor[0]))
        for i in range(g):
            dec_i = dec[0, i * lg:i * lg + 1, :]
            c_s[i] = dec_i * c_s[i] + _dot_tn(jnp.where((rows_i >> lg_shift) == i, vw[0], 0.0), k2)
            n_s[i] = dec_i * n_s[i] + jnp.sum(kw[0, i * lg:(i + 1) * lg], axis=0, keepdims=True)
            m_s[i] = jnp.broadcast_to(m_new[0, i * lg:i * lg + 1, :], (1, LANE))

    @pl.when(t == pl.num_programs(2) - 1)
    def _():
        c_out[:, 0] = c_s[...]
        n_out[:, 0] = n_s[...]
        m_out[:, 0] = m_s[:, :, 0:1]


def _gdn_body(r, g, q_ref, k_ref, v_ref, z_ref, gt_ref, cwq_ref, cwk_ref, cwv_ref, al_ref, dtb_ref,
              nw_ref, bq_ref, bk_ref, bv_ref, s0_ref, y_ref, s_out, tq_out, tk_out, tv_out,
              s_s, pq_s, pk_s, pv_s):
    lg = CHUNK // g
    lg_shift = int(math.log2(lg))
    lseq = r if g == 1 else lg
    nch = r // CHUNK
    h = pl.program_id(1)
    t = pl.program_id(2)

    @pl.when(t == 0)
    def _():
        s_s[...] = s0_ref[:, 0]
        pq_s[...] = bq_ref[...]
        pk_s[...] = bk_ref[...]
        pv_s[...] = bv_ref[...]

    qa, tq = _conv_rows(q_ref[...], pq_s[...], cwq_ref[...], g, lseq)
    ka, tk = _conv_rows(k_ref[...], pk_s[...], cwk_ref[...], g, lseq)
    va, tv = _conv_rows(v_ref[...], pv_s[...], cwv_ref[...], g, lseq)
    pq_s[...] = tq
    pk_s[...] = tk
    pv_s[...] = tv
    qa, ka, va = _silu(qa), _silu(ka), _silu(va)
    qn = qa * lax.rsqrt(jnp.sum(qa * qa, axis=1, keepdims=True) + L2_EPS) * (GDN_DK ** -0.5)
    kn = ka * lax.rsqrt(jnp.sum(ka * ka, axis=1, keepdims=True) + L2_EPS)

    lane3 = lax.broadcasted_iota(jnp.int32, (nch, CHUNK, LANE), 2)
    causal, strict = _seq_masks(g, lg_shift)
    tri_b = jnp.broadcast_to(causal.astype(BF16), (nch, CHUNK, CHUNK))
    eye = (lax.broadcasted_iota(jnp.int32, (CHUNK, CHUNK), 0)
           == lax.broadcasted_iota(jnp.int32, (CHUNK, CHUNK), 1)).astype(F32)

    q3 = qn.reshape(nch, CHUNK, GDN_DK)
    k3 = kn.reshape(nch, CHUNK, GDN_DK)
    v3 = va.reshape(nch, CHUNK, GDN_DV)
    gt = gt_ref[...].reshape(nch, CHUNK, LANE)
    beta = jnp.sum(jnp.where(lane3 == G_GDB + h, _sigmoid(gt), 0.0), axis=2, keepdims=True)
    gval = -jnp.exp(al_ref[...]) * _softplus(gt + dtb_ref[...])
    gc = _cumsum_rows(jnp.sum(jnp.where(lane3 == G_GDA + h, gval, 0.0), axis=2, keepdims=True),
                      tri_b, lane3)
    decay = jnp.exp(jnp.where(causal, _outer_sum(gc, -gc, lane3), -jnp.inf))
    p = jnp.where(strict, -(beta * _bmm_nt(k3, k3) * decay), 0.0)
    inv = eye + p
    pk = p
    for _ in range(1, lg_shift):
        pk = _bmm3(pk, pk)
        inv = inv + _bmm3(inv, pk)
    egc = jnp.exp(gc)
    u_pre = _bmm2(inv, beta * v3)
    w = _bmm2(inv, (beta * egc) * k3)
    qk = _bmm_nt(q3, k3) * decay
    q_dec = q3 * egc
    g_end = _last_rows(gc, g, lg)
    k_dec = k3 * jnp.exp(g_end - gc)

    def emit(rows, o):
        hn = o * lax.rsqrt(jnp.mean(o * o, axis=1, keepdims=True) + RMS_EPS) * nw_ref[...]
        y_ref[rows, :] = (hn * _silu(z_ref[rows, :])).astype(y_ref.dtype)

    if g == 1:
        st = s_s[0]
        for c in range(nch):
            u = u_pre[c] - _dot(w[c], st)
            emit(slice(c * CHUNK, (c + 1) * CHUNK), _dot(q_dec[c], st) + _dot(qk[c], u))
            st = jnp.exp(g_end[c]) * st + _dot_tn(k_dec[c], u)
        s_s[0] = st
    else:
        rows_i = lax.broadcasted_iota(jnp.int32, (CHUNK, 1), 0)
        us, os_ = [], []
        for i in range(g):
            sl = slice(i * lg, (i + 1) * lg)
            us.append(u_pre[0, sl] - _dot(w[0, sl], s_s[i]))
            os_.append(_dot(q_dec[0, sl], s_s[i]))
        u = jnp.concatenate(us, axis=0)
        emit(slice(0, CHUNK), jnp.concatenate(os_, axis=0) + _dot(qk[0], u))
        for i in range(g):
            kdi = jnp.where((rows_i >> lg_shift) == i, k_dec[0], 0.0)
            s_s[i] = jnp.exp(g_end[0, i * lg:i * lg + 1, :]) * s_s[i] + _dot_tn(kdi, u)

    @pl.when(t == pl.num_programs(2) - 1)
    def _():
        s_out[:, 0] = s_s[...]
        tq_out[...] = tq
        tk_out[...] = tk
        tv_out[...] = tv


def _rglru_body(r, g, x_ref, yg_ref, cw_ref, cb_ref, wa_ref, wx_ref, ba_ref, bx_ref, lam_ref,
                buf_ref, h0_ref, y_ref, h_out, tail_out, h_s, px_s):
    lseq = r if g == 1 else CHUNK // g
    t = pl.program_id(2)

    @pl.when(t == 0)
    def _():
        h_s[...] = h0_ref[:, 0]
        px_s[...] = buf_ref[...]

    xc, tail = _conv_rows(x_ref[...], px_s[...], cw_ref[...], g, lseq)
    px_s[...] = tail
    xc = xc + cb_ref[...]
    rg = _sigmoid(_dot(xc, wa_ref[0]) + ba_ref[...])
    ig = _sigmoid(_dot(xc, wx_ref[0]) + bx_ref[...])
    log_a = -LRU_C * rg * _softplus(-lam_ref[...])
    a = jnp.exp(log_a)
    u = jnp.sqrt(1.0 - jnp.exp(2.0 * log_a)) * (ig * xc)
    pos = lax.broadcasted_iota(jnp.int32, (r, 1), 0) & (lseq - 1)
    sh = 1
    while sh < lseq:
        a_sh = pltpu.roll(a, sh, 0)
        u_sh = pltpu.roll(u, sh, 0)
        msk = pos >= sh
        u = jnp.where(msk, a * u_sh + u, u)
        a = jnp.where(msk, a * a_sh, a)
        sh *= 2
    hh = u + a * _per_seq_rows([h_s[i] for i in range(g)], g, lseq)
    for i in range(g):
        h_s[i] = hh[(i + 1) * lseq - 1:(i + 1) * lseq, :]
    yg = yg_ref[...]
    gelu = 0.5 * yg * (1.0 + jnp.tanh(math.sqrt(2.0 / math.pi) * (yg + 0.044715 * (yg * yg * yg))))
    y_ref[...] = (gelu * hh).astype(y_ref.dtype)

    @pl.when(t == pl.num_programs(2) - 1)
    def _():
        h_out[:, 0] = h_s[...]
        tail_out[...] = tail


class _Group:
    def __init__(self, n_seq, seq_len, row0):
        self.n_seq, self.seq_len, self.row0 = n_seq, seq_len, row0
        self.rows = n_seq * seq_len
        if seq_len >= 4 * CHUNK:
            self.r, self.g = 4 * CHUNK, 1
        elif seq_len >= CHUNK:
            self.r, self.g = CHUNK, 1
        else:
            self.r, self.g = CHUNK, CHUNK // seq_len
        self.s = self.g
        self.nb = n_seq // self.s
        self.nt = seq_len // self.r if self.g == 1 else 1
        assert seq_len >= CONV_W - 1 and row0 % self.r == 0 and self.rows % self.r == 0
        assert n_seq % self.s == 0 and self.r & (self.r - 1) == 0
        self.rb0 = row0 // self.r

    def rb(self, b, t):
        return self.rb0 + b * self.nt + t


def _mixer_call(body, grp, l, depth, m_total, in_specs, args, y_width, y_block, state_shapes, extra_outs,
                scratch, prev, name):
    s = grp.s
    out_shape = [jax.ShapeDtypeStruct((m_total, y_width), BF16)]
    out_specs = [pl.BlockSpec((grp.r, y_block), lambda b, h, t: (grp.rb(b, t), h))]
    for tail in state_shapes:
        out_shape.append(jax.ShapeDtypeStruct((depth, grp.n_seq, y_width // y_block) + tail, F32))
        out_specs.append(pl.BlockSpec((None, s, 1) + tail, lambda b, h, t: (l, b, h) + (0,) * len(tail)))
    for shp, spec in extra_outs:
        out_shape.append(shp)
        out_specs.append(spec)
    n_in = len(args)
    aliases = {}
    in_specs = list(in_specs)
    args = list(args)
    for j, arr in enumerate(prev):
        if arr is not None:
            aliases[len(args)] = j
            in_specs.append(pl.BlockSpec(memory_space=pl.ANY))
            args.append(arr)
    n_alias = len(args) - n_in

    def wrapped(*refs):
        body(*refs[:n_in], *refs[n_in + n_alias:])

    return pl.pallas_call(
        wrapped,
        out_shape=out_shape,
        grid=(grp.nb, y_width // y_block, grp.nt),
        in_specs=in_specs,
        out_specs=out_specs,
        scratch_shapes=scratch,
        input_output_aliases=aliases,
        compiler_params=_cparams(("arbitrary",) * 3),
        name=name,
    )(*args)


def _mlstm_call(grp, l, depth, m_total, hmain, gates, gate_bias, norm_w, c0, n0, m0, prev):
    r, s = grp.r, grp.s
    row = lambda off, w: pl.BlockSpec((r, w), lambda b, h, t: (grp.rb(b, t), off // w + h))
    st = lambda *tail: pl.BlockSpec((None, s, 1) + tail, lambda b, h, t: (l, b, h, 0, 0))
    in_specs = [row(C_MLQ, ML_DK), row(C_MLK, ML_DK), row(C_MLV, ML_DV), row(C_MLO, ML_DV),
                pl.BlockSpec((r, LANE), lambda b, h, t: (grp.rb(b, t), 0)),
                pl.BlockSpec((1, LANE), lambda b, h, t: (0, 0)),
                pl.BlockSpec((1, ML_DV), lambda b, h, t: (0, h)),
                st(ML_DV, ML_DK), st(1, ML_DK), st(1, 1)]
    scratch = [pltpu.VMEM((s, ML_DV, ML_DK), F32), pltpu.VMEM((s, 1, ML_DK), F32),
               pltpu.VMEM((s, 1, LANE), F32)]
    return _mixer_call(functools.partial(_mlstm_body, r, grp.g), grp, l, depth, m_total, in_specs,
                       [hmain, hmain, hmain, hmain, gates, gate_bias, norm_w, c0, n0, m0],
                       ML_HEADS * ML_DV, ML_DV, [(ML_DV, ML_DK), (1, ML_DK), (1, 1)], [], scratch, prev, "mlstm")


def _gdn_call(grp, l, depth, m_total, hmain, gates, conv_w, a_log_row, dt_bias_row, norm_w, bufpad, s0, prev):
    r, s = grp.r, grp.s
    row = lambda off, w: pl.BlockSpec((r, w), lambda b, h, t: (grp.rb(b, t), off // w + h))
    cw = lambda off, w: pl.BlockSpec((CONV_W, w), lambda b, h, t: (0, off // w + h))
    bf = lambda off, w: pl.BlockSpec((s * SUBLANE, w), lambda b, h, t: (b, off // w + h))
    one = pl.BlockSpec((1, LANE), lambda b, h, t: (0, 0))
    in_specs = [row(C_GDQ, GDN_DK), row(C_GDK, GDN_DK), row(C_GDV, GDN_DV), row(C_GDZ, GDN_DV),
                pl.BlockSpec((r, LANE), lambda b, h, t: (grp.rb(b, t), 0)),
                cw(0, GDN_DK), cw(GDN_QK, GDN_DK), cw(2 * GDN_QK, GDN_DV),
                one, one, pl.BlockSpec((1, GDN_DV), lambda b, h, t: (0, 0)),
                bf(0, GDN_DK), bf(GDN_QK, GDN_DK), bf(2 * GDN_QK, GDN_DV),
                pl.BlockSpec((None, s, 1, GDN_DK, GDN_DV), lambda b, h, t: (l, b, h, 0, 0))]
    nrow = grp.n_seq * SUBLANE
    tail = lambda w: (jax.ShapeDtypeStruct((nrow, GDN_HEADS * w), F32),
                      pl.BlockSpec((s * SUBLANE, w), lambda b, h, t: (b, h)))
    scratch = [pltpu.VMEM((s, GDN_DK, GDN_DV), F32),
               pltpu.VMEM((s * SUBLANE, GDN_DK), F32), pltpu.VMEM((s * SUBLANE, GDN_DK), F32),
               pltpu.VMEM((s * SUBLANE, GDN_DV), F32)]
    return _mixer_call(functools.partial(_gdn_body, r, grp.g), grp, l, depth, m_total, in_specs,
                       [hmain, hmain, hmain, hmain, gates, conv_w, conv_w, conv_w, a_log_row, dt_bias_row,
                        norm_w, bufpad, bufpad, bufpad, s0],
                       GDN_HEADS * GDN_DV, GDN_DV, [(GDN_DK, GDN_DV)],
                       [tail(GDN_DK), tail(GDN_DK), tail(GDN_DV)], scratch, prev, "gdn")


def _rglru_call(grp, l, depth, m_total, hmain, conv_w, conv_b, w_a, w_x, b_a, b_x, lam, bufpad, h0, prev):
    r, s = grp.r, grp.s
    w = LRU_BLOCK
    row = lambda off: pl.BlockSpec((r, w), lambda b, h, t: (grp.rb(b, t), off // w + h))
    vec = pl.BlockSpec((1, w), lambda b, h, t: (0, h))
    blk = pl.BlockSpec((1, w, w), lambda b, h, t: (h, 0, 0))
    in_specs = [row(C_RGX), row(C_RGY),
                pl.BlockSpec((CONV_W, w), lambda b, h, t: (0, h)), vec, blk, blk, vec, vec, vec,
                pl.BlockSpec((s * SUBLANE, w), lambda b, h, t: (b, h)),
                pl.BlockSpec((None, s, 1, 1, w), lambda b, h, t: (l, b, h, 0, 0))]
    tail = (jax.ShapeDtypeStruct((grp.n_seq * SUBLANE, LRU_WIDTH), F32),
            pl.BlockSpec((s * SUBLANE, w), lambda b, h, t: (b, h)))
    scratch = [pltpu.VMEM((s, 1, w), F32), pltpu.VMEM((s * SUBLANE, w), F32)]
    return _mixer_call(functools.partial(_rglru_body, r, grp.g), grp, l, depth, m_total, in_specs,
                       [hmain, hmain, conv_w, conv_b, w_a, w_x, b_a, b_x, lam, bufpad, h0],
                       LRU_WIDTH, w, [(1, w)], [tail], scratch, prev, "rglru")


def _pad_conv_state(buf):
    b, k, c = buf.shape
    return jnp.pad(buf.astype(F32), ((0, 0), (SUBLANE - k, 0), (0, 0))).reshape(b * SUBLANE, c)


def _unpad_conv_state(tail, n_seq):
    return tail.reshape(n_seq, SUBLANE, -1)[:, SUBLANE - (CONV_W - 1):, :]


R_I1, R_I2, R_W1, R_W2, R_P1, R_P2 = 0, 1, 2, 3, 4, 5
MOE_TM = 512
MOE_TD = 512


def _router_call(x, router, tm):
    m, d = x.shape
    n_exp = router.shape[1]
    wr = jnp.pad(router.astype(F32), ((0, 0), (0, LANE - n_exp)))

    def body(x_ref, w_ref, o_ref, cnt_ref, carry):
        i = pl.program_id(0)

        @pl.when(i == 0)
        def _():
            carry[...] = jnp.zeros_like(carry)

        logits = _dot_hi(x_ref[...], w_ref[...])
        lane = lax.broadcasted_iota(jnp.int32, logits.shape, 1)
        lg = jnp.where(lane < n_exp, logits, -jnp.inf)
        v1 = jnp.max(lg, axis=1, keepdims=True)
        i1 = jnp.min(jnp.where(lg == v1, lane, LANE), axis=1, keepdims=True)
        lg2 = jnp.where(lane == i1, -jnp.inf, lg)
        v2 = jnp.max(lg2, axis=1, keepdims=True)
        i2 = jnp.min(jnp.where(lg2 == v2, lane, LANE), axis=1, keepdims=True)
        e2 = jnp.exp(v2 - v1)
        w1 = 1.0 / (1.0 + e2)
        w2 = e2 / (1.0 + e2)
        sel = ((lane == i1) | (lane == i2)).astype(F32)
        below = (lax.broadcasted_iota(jnp.int32, (tm, tm), 1)
                 < lax.broadcasted_iota(jnp.int32, (tm, tm), 0)).astype(BF16)
        rank = jnp.dot(below, sel.astype(BF16), preferred_element_type=F32) + carry[0:1, :]
        p1 = jnp.sum(jnp.where(lane == i1, rank, 0.0), axis=1, keepdims=True)
        p2 = jnp.sum(jnp.where(lane == i2, rank, 0.0), axis=1, keepdims=True)
        rec = jnp.where(lane == R_I1, i1.astype(F32), jnp.where(lane == R_I2, i2.astype(F32), 0.0))
        rec = jnp.where(lane == R_W1, w1, jnp.where(lane == R_W2, w2, rec))
        o_ref[...] = jnp.where(lane == R_P1, p1, jnp.where(lane == R_P2, p2, rec))
        carry[...] = carry[...] + jnp.sum(sel, axis=0, keepdims=True)
        cnt_ref[...] = carry[...]

    return pl.pallas_call(
        body,
        out_shape=[jax.ShapeDtypeStruct((m, LANE), F32), jax.ShapeDtypeStruct((SUBLANE, LANE), F32)],
        grid=(m // tm,),
        in_specs=[pl.BlockSpec((tm, d), lambda i: (i, 0)), pl.BlockSpec((d, LANE), lambda i: (0, 0))],
        out_specs=[pl.BlockSpec((tm, LANE), lambda i: (i, 0)), pl.BlockSpec((SUBLANE, LANE), lambda i: (0, 0))],
        scratch_shapes=[pltpu.VMEM((SUBLANE, LANE), F32)],
        compiler_params=_cparams(("arbitrary",)),
        name="router",
    )(x, wr)


def _row_copy(to_sorted, starts, e_arr, p_arr, src, dst_init, dst_shape, m):
    td = MOE_TD
    n_in = 5 if to_sorted else 4

    def body(*refs):
        start_ref, e_ref, p_ref, src_ref = refs[:4]
        dst_ref, sem = refs[n_in], refs[n_in + 1]
        base = pl.program_id(0) * td

        def copy(j):
            srt = start_ref[e_ref[0, j]] + p_ref[0, j]
            tok = base + (j >> 1)
            if to_sorted:
                return pltpu.make_async_copy(src_ref.at[pl.ds(tok, 1)], dst_ref.at[pl.ds(srt, 1)], sem)
            return pltpu.make_async_copy(src_ref.at[pl.ds(srt, 1)], dst_ref.at[pl.ds((j & 1) * m + tok, 1)], sem)

        def issue(j, c):
            copy(j).start()
            return c

        def drain(j, c):
            copy(j).wait()
            return c

        lax.fori_loop(0, 2 * td, issue, 0, unroll=8)
        lax.fori_loop(0, 2 * td, drain, 0, unroll=8)

    smem = pl.BlockSpec((None, 1, 2 * td), lambda i, s: (i, 0, 0), memory_space=pltpu.SMEM)
    any_ = pl.BlockSpec(memory_space=pl.ANY)
    args = [starts, e_arr, p_arr, src] + ([dst_init] if to_sorted else [])
    return pl.pallas_call(
        body,
        grid_spec=pltpu.PrefetchScalarGridSpec(
            num_scalar_prefetch=1, grid=(m // td,),
            in_specs=[smem, smem, any_] + ([any_] if to_sorted else []),
            out_specs=any_,
            scratch_shapes=[pltpu.SemaphoreType.DMA(())]),
        out_shape=jax.ShapeDtypeStruct(dst_shape, src.dtype),
        input_output_aliases={4: 0} if to_sorted else {},
        compiler_params=_cparams(("arbitrary",)),
        name="moe_dispatch" if to_sorted else "moe_gather",
    )(*args)


def _grouped(rows, ws, tile_expert, tile_first, n_used, epilogue, out_dtype, n_cols, tn, name):
    ns, k = rows.shape
    nt = ns // MOE_TM
    used = lambda i, nu: jnp.minimum(i, nu[0] - 1)
    w_spec = pl.BlockSpec((None, k, tn), lambda n, i, te, tf, nu: (te[i], 0, n))

    def body(te_ref, tf_ref, nu_ref, x_ref, *refs):
        w_refs, o_ref, wb_refs = refs[:len(ws)], refs[len(ws)], refs[len(ws) + 1:]
        i = pl.program_id(1)

        @pl.when(tf_ref[i] == 1)
        def _():
            for w, wb in zip(w_refs, wb_refs):
                wb[...] = w[...].astype(BF16)

        @pl.when(i < nu_ref[0])
        def _():
            xv = x_ref[...].astype(BF16)
            accs = [jnp.dot(xv, wb[...], preferred_element_type=F32) for wb in wb_refs]
            o_ref[...] = epilogue(accs).astype(o_ref.dtype)

    return pl.pallas_call(
        body,
        grid_spec=pltpu.PrefetchScalarGridSpec(
            num_scalar_prefetch=3, grid=(n_cols // tn, nt),
            in_specs=[pl.BlockSpec((MOE_TM, k), lambda n, i, te, tf, nu: (used(i, nu), 0))] + [w_spec] * len(ws),
            out_specs=pl.BlockSpec((MOE_TM, tn), lambda n, i, te, tf, nu: (used(i, nu), n)),
            scratch_shapes=[pltpu.VMEM((k, tn), BF16) for _ in ws]),
        out_shape=jax.ShapeDtypeStruct((ns, n_cols), out_dtype),
        compiler_params=_cparams(("arbitrary", "arbitrary")),
        name=name,
    )(tile_expert, tile_first, n_used, rows, *ws)


def _combine_ln(x, ycat, rec, ple, gamma, beta, tm, split_rows):
    m, n = x.shape
    tiles_a = None if split_rows is None else split_rows // tm

    def body(x_ref, y1_ref, y2_ref, r_ref, e_ref, g_ref, b_ref, oa_ref, ob_ref):
        i = pl.program_id(0)
        r = r_ref[...]
        lane = lax.broadcasted_iota(jnp.int32, r.shape, 1)
        w1 = jnp.sum(jnp.where(lane == R_W1, r, 0.0), axis=1, keepdims=True)
        w2 = jnp.sum(jnp.where(lane == R_W2, r, 0.0), axis=1, keepdims=True)
        y = DEEPNORM_ALPHA * x_ref[...] + (w1 * y1_ref[...] + w2 * y2_ref[...]) + e_ref[...]
        mu = jnp.mean(y, axis=-1, keepdims=True)
        yc = y - mu
        var = jnp.mean(yc * yc, axis=-1, keepdims=True)
        out = yc * lax.rsqrt(var + LN_EPS) * g_ref[...] + b_ref[...]
        if split_rows is None:
            oa_ref[...] = out
            ob_ref[...] = out.astype(BF16)
        else:
            @pl.when(i < tiles_a)
            def _():
                oa_ref[...] = out

            @pl.when(i >= tiles_a)
            def _():
                ob_ref[...] = out

    row = lambda i: (i, 0)
    vec = pl.BlockSpec((1, n), lambda i: (0, 0))
    if split_rows is None:
        out_shape = [jax.ShapeDtypeStruct((m, n), F32), jax.ShapeDtypeStruct((m, n), BF16)]
        out_specs = [pl.BlockSpec((tm, n), row), pl.BlockSpec((tm, n), row)]
    else:
        out_shape = [jax.ShapeDtypeStruct((split_rows, n), F32), jax.ShapeDtypeStruct((m - split_rows, n), F32)]
        out_specs = [pl.BlockSpec((tm, n), lambda i: (jnp.minimum(i, tiles_a - 1), 0)),
                     pl.BlockSpec((tm, n), lambda i: (jnp.maximum(i - tiles_a, 0), 0))]
    return pl.pallas_call(
        body,
        out_shape=out_shape,
        grid=(m // tm,),
        in_specs=[pl.BlockSpec((tm, n), row), pl.BlockSpec((tm, n), row),
                  pl.BlockSpec((tm, n), lambda i: (m // tm + i, 0)),
                  pl.BlockSpec((tm, LANE), row), pl.BlockSpec((tm, n), row), vec, vec],
        out_specs=out_specs,
        compiler_params=_cparams(("arbitrary",)),
        name="moe_combine_ln",
    )(x, ycat, ycat, rec, ple, gamma.reshape(1, n), beta.reshape(1, n))


def _moe_ln(x, ple, router, w1, w3, w2, gamma, beta, split_rows):
    m, d = x.shape
    ne, _, fe = w1.shape
    rec, counts = _router_call(x, router, 512)
    cnt = counts[0, :ne].astype(jnp.int32)
    tiles_e = (cnt + MOE_TM - 1) // MOE_TM
    tile_end = jnp.cumsum(tiles_e)
    tile_beg = tile_end - tiles_e
    n_used = tile_end[ne - 1]
    nt = (2 * m) // MOE_TM + ne
    ids = jnp.minimum(jnp.arange(nt, dtype=jnp.int32), n_used - 1)
    tile_expert = jnp.sum((ids[:, None] >= tile_end[None, :]).astype(jnp.int32), axis=1)
    tile_first = (jnp.arange(nt, dtype=jnp.int32) == tile_beg[tile_expert]).astype(jnp.int32)
    starts = (tile_beg * MOE_TM).astype(jnp.int32)
    slots = lambda a, b: jnp.stack([rec[:, a], rec[:, b]], axis=1).astype(jnp.int32).reshape(m // MOE_TD, 1, 2 * MOE_TD)
    e_arr, p_arr = slots(R_I1, R_I2), slots(R_P1, R_P2)
    ns = nt * MOE_TM

    xs = _row_copy(True, starts, e_arr, p_arr, x, jnp.zeros((ns, d), F32), (ns, d), m)
    nu = n_used.reshape(1)
    hmid = _grouped(xs, [w1, w3], tile_expert, tile_first, nu, lambda a: _silu(a[0]) * a[1], BF16, fe, 256, "moe_up")
    ys = _grouped(hmid, [w2], tile_expert, tile_first, nu, lambda a: a[0], F32, d, 512, "moe_down")
    ycat = _row_copy(False, starts, e_arr, p_arr, ys, None, (2 * m, d), m)
    return _combine_ln(x, ycat, rec, ple, gamma, beta, 512, split_rows)


def _run_group_mixers(grp, l, depth, m_total, hmain, gates, st, prm, prev):
    b = grp.n_seq
    ya, ml_c, ml_n, ml_m = _mlstm_call(
        grp, l, depth, m_total, hmain, gates, prm["gate_bias"], prm["ml_norm_w"],
        st["ml_C"].astype(F32), st["ml_n"].astype(F32).reshape(depth, b, ML_HEADS, 1, ML_DK),
        st["ml_m"].astype(F32).reshape(depth, b, ML_HEADS, 1, 1),
        [prev["ya"], prev["ml_C"], prev["ml_n"], prev["ml_m"]])
    yb, rg_h, rg_tail = _rglru_call(
        grp, l, depth, m_total, hmain, prm["rg_conv_w"], prm["rg_conv_b"], prm["rg_w_a"], prm["rg_w_x"],
        prm["rg_b_a"], prm["rg_b_x"], prm["rg_lambda"], _pad_conv_state(st["rg_conv"][l]),
        st["rg_h"].astype(F32).reshape(depth, b, LRU_BLOCKS, 1, LRU_BLOCK), [prev["yb"], prev["rg_h"]])
    yc, gd_s, tq, tk, tv = _gdn_call(
        grp, l, depth, m_total, hmain, gates, prm["gd_conv_w"], prm["gd_a_log"], prm["gd_dt_bias"],
        prm["gd_norm_w"], _pad_conv_state(st["gd_conv"][l]), st["gd_S"].astype(F32),
        [prev["yc"], prev["gd_S"]])
    out = dict(ya=ya, yb=yb, yc=yc, ml_C=ml_c, ml_n=ml_n, ml_m=ml_m, rg_h=rg_h, gd_S=gd_s)
    tails = dict(rg_conv=_unpad_conv_state(rg_tail, b),
                 gd_conv=jnp.concatenate([_unpad_conv_state(x, b) for x in (tq, tk, tv)], axis=-1))
    return out, tails


_STATE_KEYS = ("ml_C", "ml_n", "ml_m", "rg_h", "gd_S")


def kernel(x_prompt, x_sample, state_mlstm_C, state_mlstm_n, state_mlstm_m, state_rglru_h, state_rglru_conv, state_gdn_S, state_gdn_conv, p_prompt, p_sample, w_in, ml_b_i, ml_b_f, ml_norm_w, rg_conv_w, rg_conv_b, rg_w_a, rg_b_a, rg_w_x, rg_b_x, rg_lambda, gd_conv_w, gd_A_log, gd_dt_bias, gd_norm_w, w_up_mlstm, w_up_rglru, w_up_gdn, w_out, ln1_g, ln1_b, ffn_w1, ffn_w3, ffn_w2, moe_router, moe_w1, moe_w3, moe_w2, ple_w, ple_gate_w, ln2_g, ln2_b):
    bp, tp, d = x_prompt.shape
    bs, ts, _ = x_sample.shape
    depth = w_in.shape[0]
    gp = _Group(bp, tp, 0)
    gs = _Group(bs, ts, bp * tp)
    m = gp.rows + gs.rows
    pd = x_prompt.dtype

    zeros = lambda *shape: jnp.zeros(shape, F32)
    st_p = dict(ml_C=zeros(depth, bp, ML_HEADS, ML_DV, ML_DK), ml_n=zeros(depth, bp, ML_HEADS, ML_DK),
                ml_m=zeros(depth, bp, ML_HEADS), rg_h=zeros(depth, bp, LRU_WIDTH),
                rg_conv=zeros(depth, bp, CONV_W - 1, LRU_WIDTH),
                gd_S=zeros(depth, bp, GDN_HEADS, GDN_DK, GDN_DV),
                gd_conv=zeros(depth, bp, CONV_W - 1, GDN_CONV_DIM))
    st_s = dict(ml_C=state_mlstm_C, ml_n=state_mlstm_n, ml_m=state_mlstm_m, rg_h=state_rglru_h,
                rg_conv=state_rglru_conv, gd_S=state_gdn_S, gd_conv=state_gdn_conv)

    x = jnp.concatenate([x_prompt.reshape(gp.rows, d), x_sample.reshape(gs.rows, d)], axis=0).astype(F32)
    xb = x.astype(BF16)
    ident = lambda accs, extras, ids: [accs[0]]
    lane_row = lambda v, off: jnp.zeros((1, LANE), F32).at[0, off:off + v.shape[0]].set(v.astype(F32))
    acc_p = {k: None for k in _STATE_KEYS}
    acc_s = {k: None for k in _STATE_KEYS}
    tails_p, tails_s = [], []

    for l in range(depth):
        hmain = _in_proj(xb, w_in, l, 1024)
        w_gate = jnp.pad(jnp.concatenate([w_in[l, :, W_GATES_A:W_GATES_A + SUBLANE],
                                          w_in[l, :, W_GATES_B:W_GATES_B + SUBLANE]], axis=1),
                         ((0, 0), (0, LANE - 2 * SUBLANE)))
        gates = _proj([xb], [w_gate], [pl.BlockSpec((d, LANE), lambda n, i: (0, n))], [], [], ident,
                      [jax.ShapeDtypeStruct((m, LANE), F32)], LANE, 1024, LANE, "gate_proj")[0]

        prm = dict(
            gate_bias=lane_row(ml_b_i[l], G_MLI) + lane_row(ml_b_f[l], G_MLF),
            ml_norm_w=ml_norm_w[l].astype(F32).reshape(1, -1),
            rg_conv_w=rg_conv_w[l].astype(F32), rg_conv_b=rg_conv_b[l].astype(F32).reshape(1, -1),
            rg_w_a=rg_w_a[l], rg_w_x=rg_w_x[l], rg_b_a=rg_b_a[l].astype(F32).reshape(1, -1),
            rg_b_x=rg_b_x[l].astype(F32).reshape(1, -1), rg_lambda=rg_lambda[l].astype(F32).reshape(1, -1),
            gd_conv_w=gd_conv_w[l].astype(F32), gd_a_log=lane_row(gd_A_log[l], G_GDA),
            gd_dt_bias=lane_row(gd_dt_bias[l], G_GDA), gd_norm_w=gd_norm_w[l].astype(F32).reshape(1, -1))
        out_p, tl_p = _run_group_mixers(gp, l, depth, m, hmain, gates, st_p, prm,
                                        dict(acc_p, ya=None, yb=None, yc=None))
        out_s, tl_s = _run_group_mixers(gs, l, depth, m, hmain, gates, st_s, prm,
                                        dict(acc_s, ya=out_p["ya"], yb=out_p["yb"], yc=out_p["yc"]))
        acc_p = {k: out_p[k] for k in _STATE_KEYS}
        acc_s = {k: out_s[k] for k in _STATE_KEYS}
        tails_p.append(tl_p)
        tails_s.append(tl_s)
        ya, yb, yc = out_s["ya"], out_s["yb"], out_s["yc"]

        tn, tm = 512, 512
        up_spec = pl.BlockSpec((None, ya.shape[1], tn), lambda n, i: (l, 0, n))
        mg_spec = lambda br: pl.BlockSpec((tm, tn), lambda n, i: (i, (C_MG + br * d) // tn + n))
        merged = _proj([ya, yb, yc], [w_up_mlstm, w_up_rglru, w_up_gdn], [up_spec] * 3,
                       [hmain] * 3, [mg_spec(0), mg_spec(1), mg_spec(2)],
                       lambda accs, ex, ids: [_sigmoid(ex[0]) * accs[0] + _sigmoid(ex[1]) * accs[1]
                                              + _sigmoid(ex[2]) * accs[2]],
                       [jax.ShapeDtypeStruct((m, d), BF16)], d, tm, tn, "merge")[0]
        x, xb = _rowfull_ln(merged, w_out[l].astype(BF16), x, None, ln1_g[l].astype(F32), ln1_b[l].astype(F32),
                            512, d, "out_proj_ln")

        p_l = jnp.concatenate([p_prompt[l].reshape(gp.rows, -1), p_sample[l].reshape(gs.rows, -1)], axis=0)
        ple = _proj([xb, p_l], [ple_gate_w, ple_w],
                    [pl.BlockSpec((None, d, 1024), lambda n, i: (l, 0, n)),
                     pl.BlockSpec((None, P_DIM, 1024), lambda n, i: (l, 0, n))],
                    [], [], lambda accs, ex, ids: [_sigmoid(accs[0]) * accs[1]],
                    [jax.ShapeDtypeStruct((m, d), F32)], d, 1024, 1024, "ple")[0]
        j = l // 2
        split = gp.rows if l == depth - 1 else None
        g2, b2 = ln2_g[l].astype(F32), ln2_b[l].astype(F32)
        if l % 2 == 0:
            ff = ffn_w1.shape[2]
            hmid = _proj([xb], [ffn_w1, ffn_w3], [pl.BlockSpec((None, d, 512), lambda n, i: (j, 0, n))] * 2,
                         [], [], lambda accs, ex, ids: [_silu(accs[0]) * accs[1]],
                         [jax.ShapeDtypeStruct((m, ff), BF16)], ff, 1024, 512, "ffn_up", x_of_w=(0, 0))[0]
            x, xb = _rowfull_ln(hmid, ffn_w2[j].astype(BF16), x, ple, g2, b2, 512, 1408, "ffn_down_ln",
                                split_rows=split)
        else:
            x, xb = _moe_ln(x, ple, moe_router[j], moe_w1[j], moe_w3[j], moe_w2[j], g2, b2, split)

    y_prompt = x.reshape(bp, tp, d).astype(pd)
    y_sample = xb.reshape(bs, ts, d).astype(x_sample.dtype)

    def finish(acc, tails, n_seq, dtypes):
        conv = {k: jnp.stack([tl[k] for tl in tails]) for k in ("rg_conv", "gd_conv")}
        vals = dict(ml_C=acc["ml_C"], ml_n=acc["ml_n"].reshape(depth, n_seq, ML_HEADS, ML_DK),
                    ml_m=acc["ml_m"].reshape(depth, n_seq, ML_HEADS),
                    rg_h=acc["rg_h"].reshape(depth, n_seq, LRU_WIDTH), rg_conv=conv["rg_conv"],
                    gd_S=acc["gd_S"], gd_conv=conv["gd_conv"])
        names = ("ml_C", "ml_n", "ml_m", "rg_h", "rg_conv", "gd_S", "gd_conv")
        return tuple(vals[k].astype(dtypes[k]) for k in names)

    return ((y_prompt, y_sample) + finish(acc_p, tails_p, bp, {k: pd for k in st_p})
            + finish(acc_s, tails_s, bs, {k: v.dtype for k, v in st_s.items()}))
```

```python
import functools
import math

import jax
import jax.numpy as jnp
from jax import lax
from jax.experimental import pallas as pl
from jax.experimental.pallas import tpu as pltpu

F32 = jnp.float32
BF16 = jnp.bfloat16

D_MODEL = 2048
DEPTH = 2
P_DIM = 256
CONV_W = 4
N_BRANCH = 3
ML_HEADS = 4
ML_DK = 128
ML_DV = 256
ML_GATE_CAP = 15.0
LRU_WIDTH = 1024
LRU_BLOCKS = 4
LRU_BLOCK = LRU_WIDTH // LRU_BLOCKS
LRU_C = 8.0
GDN_HEADS = 4
GDN_DK = 128
GDN_DV = 256
GDN_QK = GDN_HEADS * GDN_DK
GDN_CONV_DIM = 2 * GDN_QK + GDN_HEADS * GDN_DV
N_EXPERTS = 8
DEEPNORM_ALPHA = (2 * DEPTH) ** 0.25
LN_EPS = 1e-5
RMS_EPS = 1e-6
L2_EPS = 1e-6

LANE = 128
SUBLANE = 8
CHUNK = 128
VMEM_LIMIT = 56 * 1024 * 1024

C_MLQ, C_MLK, C_MLV, C_MLO = 0, 512, 1024, 2048
C_RGX, C_RGY = 3072, 4096
C_GDQ, C_GDK, C_GDV, C_GDZ = 5120, 5632, 6144, 7168
C_MG = 8192
N_MAIN = C_MG + N_BRANCH * D_MODEL
W_GATES_A, W_GATES_B = 2048, 7176
G_MLI, G_MLF, G_GDB, G_GDA = 0, 4, 8, 12
PROJ_TN = 1024

HIGHEST = lax.Precision.HIGHEST


def _cparams(sem):
    return pltpu.CompilerParams(dimension_semantics=sem, vmem_limit_bytes=VMEM_LIMIT)


def _dot(a, b):
    return jnp.dot(a.astype(BF16), b.astype(BF16), preferred_element_type=F32)


def _dot_nt(a, b):
    return lax.dot_general(a.astype(BF16), b.astype(BF16), (((1,), (1,)), ((), ())),
                           preferred_element_type=F32)


def _dot_tn(a, b):
    return lax.dot_general(a.astype(BF16), b.astype(BF16), (((0,), (0,)), ((), ())),
                           preferred_element_type=F32)


def _dot_hi(a, b):
    return jnp.dot(a, b, precision=HIGHEST, preferred_element_type=F32)


def _bmm(a, b):
    return jnp.einsum('cij,cjk->cik', a.astype(BF16), b.astype(BF16), preferred_element_type=F32)


def _bmm_nt(a, b):
    return jnp.einsum('cik,cjk->cij', a.astype(BF16), b.astype(BF16), preferred_element_type=F32)


def _bmm_tn(a, b):
    return jnp.einsum('csi,csj->cij', a.astype(BF16), b.astype(BF16), preferred_element_type=F32)


def _sigmoid(x):
    return 1.0 / (1.0 + jnp.exp(-x))


def _silu(x):
    return x * _sigmoid(x)


def _softplus(x):
    return jnp.maximum(x, 0.0) + jnp.log1p(jnp.exp(-jnp.abs(x)))


def _proj(xs, ws, w_specs, extras, e_specs, epilogue, out_shapes, n_cols, tm, tn, name,
          lead_grid=(), x_of_w=None):
    m = xs[0].shape[0]
    nl = len(lead_grid)
    grid = tuple(lead_grid) + (n_cols // tn, m // tm)
    x_specs = [pl.BlockSpec((tm, x.shape[1]), lambda *g: (g[-1], 0)) for x in xs]
    o_specs = [pl.BlockSpec((tm, tn), lambda *g: (g[-1], g[-2]) if nl == 0 else
                            (g[-1], g[0] * (n_cols // tn) + g[-2])) for _ in out_shapes]
    scratch = [pltpu.VMEM(tuple(d for d in s.block_shape if d is not None), BF16) for s in w_specs]
    xw = tuple(range(len(ws))) if x_of_w is None else tuple(x_of_w)

    def body(*refs):
        nx, nw, ne, no = len(xs), len(ws), len(extras), len(out_shapes)
        x_refs = refs[:nx]
        w_refs = refs[nx:nx + nw]
        e_refs = refs[nx + nw:nx + nw + ne]
        o_refs = refs[nx + nw + ne:nx + nw + ne + no]
        wb_refs = refs[nx + nw + ne + no:]

        @pl.when(pl.program_id(nl + 1) == 0)
        def _():
            for w, wb in zip(w_refs, wb_refs):
                wb[...] = w[...].astype(BF16)

        xv = [x[...].astype(BF16) for x in x_refs]
        accs = [jnp.dot(xv[j], wb[...], preferred_element_type=F32) for j, wb in zip(xw, wb_refs)]
        outs = epilogue(accs, [e[...] for e in e_refs], [pl.program_id(i) for i in range(nl)])
        for o, v in zip(o_refs, outs):
            o[...] = v.astype(o.dtype)

    return pl.pallas_call(
        body,
        out_shape=out_shapes,
        grid=grid,
        in_specs=x_specs + list(w_specs) + list(e_specs),
        out_specs=o_specs,
        scratch_shapes=scratch,
        compiler_params=_cparams(("arbitrary",) * len(grid)),
        name=name,
    )(*xs, *ws, *extras)


def _in_proj(xb, w, l, tm):
    m, d = xb.shape
    tn = PROJ_TN
    n_blocks = N_MAIN // tn
    first_a = W_GATES_A // tn
    first_b = (W_GATES_B - SUBLANE) // tn

    def body(x_ref, wm_ref, wn_ref, o_ref, wb_ref):
        n = pl.program_id(0)

        @pl.when(pl.program_id(1) == 0)
        def _():
            def shifted(s):
                wcat = jnp.concatenate([wm_ref[...], wn_ref[...]], axis=1)
                wb_ref[...] = wcat[:, s:s + tn].astype(BF16)

            @pl.when(n < first_a)
            def _():
                wb_ref[...] = wm_ref[...].astype(BF16)

            @pl.when((n >= first_a) & (n < first_b))
            def _():
                shifted(SUBLANE)

            @pl.when(n >= first_b)
            def _():
                shifted(2 * SUBLANE)

        o_ref[...] = jnp.dot(x_ref[...], wb_ref[...], preferred_element_type=F32)

    return pl.pallas_call(
        body,
        out_shape=jax.ShapeDtypeStruct((m, N_MAIN), F32),
        grid=(n_blocks, m // tm),
        in_specs=[pl.BlockSpec((tm, d), lambda n, i: (i, 0)),
                  pl.BlockSpec((None, d, tn), lambda n, i: (l, 0, n)),
                  pl.BlockSpec((None, d, LANE), lambda n, i: (l, 0, (n + 1) * (tn // LANE)))],
        out_specs=pl.BlockSpec((tm, tn), lambda n, i: (i, n)),
        scratch_shapes=[pltpu.VMEM((d, tn), BF16)],
        compiler_params=_cparams(("arbitrary", "arbitrary")),
        name="in_proj",
    )(xb, w, w)


def _rowfull_ln(h, w_bf16, res, extra, gamma, beta, tm, tk, name, split_rows=None):
    m, k = h.shape
    n = w_bf16.shape[1]
    nk = k // tk
    has_extra = extra is not None
    n_in = 6 if has_extra else 5
    tiles_a = None if split_rows is None else split_rows // tm

    def body(*refs):
        h_ref, w_ref, r_ref = refs[:3]
        e_ref = refs[3] if has_extra else None
        g_ref, b_ref = refs[n_in - 2:n_in]
        oa_ref, ob_ref, acc_ref = refs[n_in:]
        i = pl.program_id(0)
        kk = pl.program_id(1)

        @pl.when(kk == 0)
        def _():
            acc_ref[...] = jnp.zeros_like(acc_ref)

        acc_ref[...] += jnp.dot(h_ref[...], w_ref[...], preferred_element_type=F32)

        @pl.when(kk == nk - 1)
        def _():
            y = DEEPNORM_ALPHA * r_ref[...] + acc_ref[...]
            if e_ref is not None:
                y = y + e_ref[...].astype(F32)
            mu = jnp.mean(y, axis=-1, keepdims=True)
            yc = y - mu
            var = jnp.mean(yc * yc, axis=-1, keepdims=True)
            out = yc * lax.rsqrt(var + LN_EPS) * g_ref[...] + b_ref[...]
            if split_rows is None:
                oa_ref[...] = out
                ob_ref[...] = out.astype(BF16)
            else:
                @pl.when(i < tiles_a)
                def _():
                    oa_ref[...] = out

                @pl.when(i >= tiles_a)
                def _():
                    ob_ref[...] = out

    row = lambda i, j: (i, 0)
    in_specs = [pl.BlockSpec((tm, tk), lambda i, j: (i, j)),
                pl.BlockSpec((tk, n), lambda i, j: (j, 0)),
                pl.BlockSpec((tm, n), row)]
    args = [h, w_bf16, res]
    if has_extra:
        in_specs.append(pl.BlockSpec((tm, n), row))
        args.append(extra)
    in_specs += [pl.BlockSpec((1, n), lambda i, j: (0, 0))] * 2
    args += [gamma.reshape(1, n), beta.reshape(1, n)]
    if split_rows is None:
        out_shape = [jax.ShapeDtypeStruct((m, n), F32), jax.ShapeDtypeStruct((m, n), BF16)]
        out_specs = [pl.BlockSpec((tm, n), row), pl.BlockSpec((tm, n), row)]
    else:
        out_shape = [jax.ShapeDtypeStruct((split_rows, n), F32), jax.ShapeDtypeStruct((m - split_rows, n), F32)]
        out_specs = [pl.BlockSpec((tm, n), lambda i, j: (jnp.minimum(i, tiles_a - 1), 0)),
                     pl.BlockSpec((tm, n), lambda i, j: (jnp.maximum(i - tiles_a, 0), 0))]
    return pl.pallas_call(
        body,
        out_shape=out_shape,
        grid=(m // tm, nk),
        in_specs=in_specs,
        out_specs=out_specs,
        scratch_shapes=[pltpu.VMEM((tm, n), F32)],
        compiler_params=_cparams(("arbitrary", "arbitrary")),
        name=name,
    )(*args)


def _seq_masks(g, lg_shift):
    row = lax.broadcasted_iota(jnp.int32, (CHUNK, CHUNK), 0)
    col = lax.broadcasted_iota(jnp.int32, (CHUNK, CHUNK), 1)
    if g == 1:
        return col <= row, col < row
    same = (row >> lg_shift) == (col >> lg_shift)
    return same & (col <= row), same & (col < row)


def _per_seq_rows(vals, g, lg):
    if g == 1:
        return vals[0]
    return jnp.concatenate([jnp.broadcast_to(v, (lg, v.shape[1])) for v in vals], axis=0)


def _last_rows(col, g, lg):
    if g == 1:
        return col[:, CHUNK - 1:, :]
    return _per_seq_rows([col[0, (i + 1) * lg - 1:(i + 1) * lg, :] for i in range(g)], g, lg)[None]


def _split3(x):
    x1 = x.astype(BF16).astype(F32)
    r1 = x - x1
    x2 = r1.astype(BF16).astype(F32)
    x3 = (r1 - x2).astype(BF16).astype(F32)
    return x1, x2, x3


def _split2(x):
    hi = x.astype(BF16)
    return hi, (x - hi.astype(F32)).astype(BF16)


def _lanes3(col, lane_ids, first, other):
    c1, c2, c3 = _split3(col)
    x = jnp.where(lane_ids == first, c1, jnp.where(lane_ids == first + 1, c2,
                                                   jnp.where(lane_ids == first + 2, c3, 0.0)))
    return jnp.where((lane_ids >= other) & (lane_ids < other + 3), 1.0, x).astype(BF16)


def _cumsum_rows(col, tri_b, lane_ids):
    return jnp.sum(_bmm(tri_b, _lanes3(col, lane_ids, 0, LANE)), axis=-1, keepdims=True)


def _outer_sum(a_col, b_col, lane_ids):
    return _bmm_nt(_lanes3(a_col, lane_ids, 0, 3), _lanes3(b_col, lane_ids, 3, 0))


def _bmm3(a, b):
    ah, al = _split2(a)
    bh, bl = _split2(b)
    return _bmm(ah, bh) + (_bmm(ah, bl) + _bmm(al, bh))


def _bmm2(a, b):
    ah, al = _split2(a)
    bb = b.astype(BF16)
    return _bmm(ah, bb) + _bmm(al, bb)


def _conv_rows(x, prev, w, g, lg):
    f = x.shape[1]
    xx = jnp.concatenate([prev.reshape(g, SUBLANE, f), x.reshape(g, lg, f)], axis=1)
    y = xx[:, SUBLANE:, :] * w[CONV_W - 1:CONV_W, :]
    for s in range(1, CONV_W):
        y = y + xx[:, SUBLANE - s:SUBLANE - s + lg, :] * w[CONV_W - 1 - s:CONV_W - s, :]
    return y.reshape(g * lg, f), xx[:, lg:, :].reshape(g * SUBLANE, f)


def _mlstm_body(r, g, q_ref, k_ref, v_ref, o_ref, gt_ref, gb_ref, nw_ref, c0_ref, n0_ref, m0_ref,
                y_ref, c_out, n_out, m_out, c_s, n_s, m_s):
    lg = CHUNK // g
    lg_shift = int(math.log2(lg))
    nch = r // CHUNK
    h = pl.program_id(1)
    t = pl.program_id(2)

    @pl.when(t == 0)
    def _():
        c_s[...] = c0_ref[:, 0]
        n_s[...] = n0_ref[:, 0]
        m_s[...] = jnp.broadcast_to(m0_ref[:, 0], m_s.shape)

    lane3 = lax.broadcasted_iota(jnp.int32, (nch, CHUNK, LANE), 2)
    causal, _ = _seq_masks(g, lg_shift)
    tri_b = jnp.broadcast_to(causal.astype(BF16), (nch, CHUNK, CHUNK))

    pre = gt_ref[...].reshape(nch, CHUNK, LANE) + gb_ref[...]
    capd = ML_GATE_CAP * jnp.tanh(pre * (1.0 / ML_GATE_CAP))
    logsig = jnp.minimum(capd, 0.0) - jnp.log1p(jnp.exp(-jnp.abs(capd)))
    ic = jnp.sum(jnp.where(lane3 == G_MLI + h, capd, 0.0), axis=2, keepdims=True)
    lf = jnp.sum(jnp.where(lane3 == G_MLF + h, logsig, 0.0), axis=2, keepdims=True)
    bc = _cumsum_rows(lf, tri_b, lane3)
    dm = jnp.where(causal, _outer_sum(bc, ic - bc, lane3), -jnp.inf)
    rmax = jnp.max(dm, axis=2, keepdims=True)

    if g == 1:
        mp = m_s[0][:, 0:1]
        m_prevs = []
        for c in range(nch):
            m_prevs.append(mp.reshape(1, 1, 1))
            mp = jnp.maximum(mp + bc[c, CHUNK - 1:, :], rmax[c, CHUNK - 1:, :])
        m_prev = jnp.concatenate(m_prevs, axis=0) if nch > 1 else m_prevs[0]
        m_s[0] = jnp.broadcast_to(mp, (1, LANE))
    else:
        m_prev = _per_seq_rows([m_s[i][:, 0:1] for i in range(g)], g, lg)[None]
    m_t = jnp.maximum(m_prev + bc, rmax)
    dmat = jnp.exp(dm - m_t)
    q3 = q_ref[...].reshape(nch, CHUNK, ML_DK) * (ML_DK ** -0.5)
    k3 = k_ref[...].reshape(nch, CHUNK, ML_DK)
    v3 = v_ref[...].reshape(nch, CHUNK, ML_DV)
    s = _bmm_nt(q3, k3) * dmat
    inter = jnp.exp(bc + m_prev - m_t)
    bc_end = _last_rows(bc, g, lg)
    m_new = _last_rows(m_t, g, lg)
    w_col = jnp.exp(ic - bc + bc_end - m_new)
    dec = jnp.exp(bc_end + m_prev - m_new)
    sv = _bmm(s, v3)
    ssum = jnp.sum(s, axis=2, keepdims=True)
    vw = v3 * w_col
    kw = k3 * w_col
    floor = jnp.exp(-m_t)

    def emit(rows, hh):
        hn = hh * lax.rsqrt(jnp.mean(hh * hh, axis=1, keepdims=True) + RMS_EPS) * nw_ref[...]
        y_ref[rows, :] = (_sigmoid(o_ref[rows, :]) * hn).astype(y_ref.dtype)

    if g == 1:
        upd = _bmm_tn(vw, k3)
        ksum = jnp.sum(kw, axis=1, keepdims=True)
        cc, nn = c_s[0], n_s[0]
        for c in range(nch):
            num = sv[c] + inter[c] * _dot_nt(q3[c], cc)
            den = ssum[c] + inter[c] * jnp.sum(q3[c] * nn, axis=1, keepdims=True)
            emit(slice(c * CHUNK, (c + 1) * CHUNK), num / jnp.maximum(jnp.abs(den), floor[c]))
            cc = dec[c] * cc + upd[c]
            nn = dec[c] * nn + ksum[c]
        c_s[0] = cc
        n_s[0] = nn
    else:
        rows_i = lax.broadcasted_iota(jnp.int32, (CHUNK, 1), 0)
        q2, k2 = q3[0], k3[0]
        qc = jnp.concatenate([_dot_nt(q2[i * lg:(i + 1) * lg], c_s[i]) for i in range(g)], axis=0)
        qn = jnp.concatenate([jnp.sum(q2[i * lg:(i + 1) * lg] * n_s[i], axis=1, keepdims=True)
                              for i in range(g)], axis=0)
        num = sv[0] + inter[0] * qc
        den = ssum[0] + inter[0] * qn
        emit(slice(0, CHUNK), num / jnp.maximum(jnp.abs(den), floor[0]))
        for i in range(g):
            dec_i = dec[0, i * lg:i * lg + 1, :]
            c_s[i] = dec_i * c_s[i] + _dot_tn(jnp.where((rows_i >> lg_shift) == i, vw[0], 0.0), k2)
            n_s[i] = dec_i * n_s[i] + jnp.sum(kw[0, i * lg:(i + 1) * lg], axis=0, keepdims=True)
            m_s[i] = jnp.broadcast_to(m_new[0, i * lg:i * lg + 1, :], (1, LANE))

    @pl.when(t == pl.num_programs(2) - 1)
    def _():
        c_out[:, 0] = c_s[...]
        n_out[:, 0] = n_s[...]
        m_out[:, 0] = m_s[:, :, 0:1]


def _gdn_body(r, g, q_ref, k_ref, v_ref, z_ref, gt_ref, cwq_ref, cwk_ref, cwv_ref, al_ref, dtb_ref,
              nw_ref, bq_ref, bk_ref, bv_ref, s0_ref, y_ref, s_out, tq_out, tk_out, tv_out,
              s_s, pq_s, pk_s, pv_s):
    lg = CHUNK // g
    lg_shift = int(math.log2(lg))
    lseq = r if g == 1 else lg
    nch = r // CHUNK
    h = pl.program_id(1)
    t = pl.program_id(2)

    @pl.when(t == 0)
    def _():
        s_s[...] = s0_ref[:, 0]
        pq_s[...] = bq_ref[...]
        pk_s[...] = bk_ref[...]
        pv_s[...] = bv_ref[...]

    qa, tq = _conv_rows(q_ref[...], pq_s[...], cwq_ref[...], g, lseq)
    ka, tk = _conv_rows(k_ref[...], pk_s[...], cwk_ref[...], g, lseq)
    va, tv = _conv_rows(v_ref[...], pv_s[...], cwv_ref[...], g, lseq)
    pq_s[...] = tq
    pk_s[...] = tk
    pv_s[...] = tv
    qa, ka, va = _silu(qa), _silu(ka), _silu(va)
    qn = qa * lax.rsqrt(jnp.sum(qa * qa, axis=1, keepdims=True) + L2_EPS) * (GDN_DK ** -0.5)
    kn = ka * lax.rsqrt(jnp.sum(ka * ka, axis=1, keepdims=True) + L2_EPS)

    lane3 = lax.broadcasted_iota(jnp.int32, (nch, CHUNK, LANE), 2)
    causal, strict = _seq_masks(g, lg_shift)
    tri_b = jnp.broadcast_to(causal.astype(BF16), (nch, CHUNK, CHUNK))
    eye = (lax.broadcasted_iota(jnp.int32, (CHUNK, CHUNK), 0)
           == lax.broadcasted_iota(jnp.int32, (CHUNK, CHUNK), 1)).astype(F32)

    q3 = qn.reshape(nch, CHUNK, GDN_DK)
    k3 = kn.reshape(nch, CHUNK, GDN_DK)
    v3 = va.reshape(nch, CHUNK, GDN_DV)
    gt = gt_ref[...].reshape(nch, CHUNK, LANE)
    beta = jnp.sum(jnp.where(lane3 == G_GDB + h, _sigmoid(gt), 0.0), axis=2, keepdims=True)
    gval = -jnp.exp(al_ref[...]) * _softplus(gt + dtb_ref[...])
    gc = _cumsum_rows(jnp.sum(jnp.where(lane3 == G_GDA + h, gval, 0.0), axis=2, keepdims=True),
                      tri_b, lane3)
    decay = jnp.exp(jnp.where(causal, _outer_sum(gc, -gc, lane3), -jnp.inf))
    p = jnp.where(strict, -(beta * _bmm_nt(k3, k3) * decay), 0.0)
    inv = eye + p
    pk = p
    for _ in range(1, lg_shift):
        pk = _bmm3(pk, pk)
        inv = inv + _bmm3(inv, pk)
    egc = jnp.exp(gc)
    u_pre = _bmm2(inv, beta * v3)
    w = _bmm2(inv, (beta * egc) * k3)
    qk = _bmm_nt(q3, k3) * decay
    q_dec = q3 * egc
    g_end = _last_rows(gc, g, lg)
    k_dec = k3 * jnp.exp(g_end - gc)

    def emit(rows, o):
        hn = o * lax.rsqrt(jnp.mean(o * o, axis=1, keepdims=True) + RMS_EPS) * nw_ref[...]
        y_ref[rows, :] = (hn * _silu(z_ref[rows, :])).astype(y_ref.dtype)

    if g == 1:
        st = s_s[0]
        for c in range(nch):
            u = u_pre[c] - _dot(w[c], st)
            emit(slice(c * CHUNK, (c + 1) * CHUNK), _dot(q_dec[c], st) + _dot(qk[c], u))
            st = jnp.exp(g_end[c]) * st + _dot_tn(k_dec[c], u)
        s_s[0] = st
    else:
        rows_i = lax.broadcasted_iota(jnp.int32, (CHUNK, 1), 0)
        us, os_ = [], []
        for i in range(g):
            sl = slice(i * lg, (i + 1) * lg)
            us.append(u_pre[0, sl] - _dot(w[0, sl], s_s[i]))
            os_.append(_dot(q_dec[0, sl], s_s[i]))
        u = jnp.concatenate(us, axis=0)
        emit(slice(0, CHUNK), jnp.concatenate(os_, axis=0) + _dot(qk[0], u))
        for i in range(g):
            kdi = jnp.where((rows_i >> lg_shift) == i, k_dec[0], 0.0)
            s_s[i] = jnp.exp(g_end[0, i * lg:i * lg + 1, :]) * s_s[i] + _dot_tn(kdi, u)

    @pl.when(t == pl.num_programs(2) - 1)
    def _():
        s_out[:, 0] = s_s[...]
        tq_out[...] = tq
        tk_out[...] = tk
        tv_out[...] = tv


def _rglru_body(r, g, x_ref, yg_ref, cw_ref, cb_ref, wa_ref, wx_ref, ba_ref, bx_ref, lam_ref,
                buf_ref, h0_ref, y_ref, h_out, tail_out, h_s, px_s):
    lseq = r if g == 1 else CHUNK // g
    t = pl.program_id(2)

    @pl.when(t == 0)
    def _():
        h_s[...] = h0_ref[:, 0]
        px_s[...] = buf_ref[...]

    xc, tail = _conv_rows(x_ref[...], px_s[...], cw_ref[...], g, lseq)
    px_s[...] = tail
    xc = xc + cb_ref[...]
    rg = _sigmoid(_dot(xc, wa_ref[0]) + ba_ref[...])
    ig = _sigmoid(_dot(xc, wx_ref[0]) + bx_ref[...])
    log_a = -LRU_C * rg * _softplus(-lam_ref[...])
    a = jnp.exp(log_a)
    u = jnp.sqrt(1.0 - jnp.exp(2.0 * log_a)) * (ig * xc)
    pos = lax.broadcasted_iota(jnp.int32, (r, 1), 0) & (lseq - 1)
    sh = 1
    while sh < lseq:
        a_sh = pltpu.roll(a, sh, 0)
        u_sh = pltpu.roll(u, sh, 0)
        msk = pos >= sh
        u = jnp.where(msk, a * u_sh + u, u)
        a = jnp.where(msk, a * a_sh, a)
        sh *= 2
    hh = u + a * _per_seq_rows([h_s[i] for i in range(g)], g, lseq)
    for i in range(g):
        h_s[i] = hh[(i + 1) * lseq - 1:(i + 1) * lseq, :]
    yg = yg_ref[...]
    gelu = 0.5 * yg * (1.0 + jnp.tanh(math.sqrt(2.0 / math.pi) * (yg + 0.044715 * (yg * yg * yg))))
    y_ref[...] = (gelu * hh).astype(y_ref.dtype)

    @pl.when(t == pl.num_programs(2) - 1)
    def _():
        h_out[:, 0] = h_s[...]
        tail_out[...] = tail


class _Group:
    def __init__(self, n_seq, seq_len, row0):
        self.n_seq, self.seq_len, self.row0 = n_seq, seq_len, row0
        self.rows = n_seq * seq_len
        if seq_len >= 4 * CHUNK:
            self.r, self.g = 4 * CHUNK, 1
        elif seq_len >= CHUNK:
            self.r, self.g = CHUNK, 1
        else:
            self.r, self.g = CHUNK, CHUNK // seq_len
        self.s = self.g
        self.nb = n_seq // self.s
        self.nt = seq_len // self.r if self.g == 1 else 1
        assert seq_len >= CONV_W - 1 and row0 % self.r == 0 and self.rows % self.r == 0
        assert n_seq % self.s == 0 and self.r & (self.r - 1) == 0
        self.rb0 = row0 // self.r

    def rb(self, b, t):
        return self.rb0 + b * self.nt + t


def _mixer_call(body, grp, l, depth, m_total, in_specs, args, y_width, y_block, state_shapes, extra_outs,
                scratch, prev, name):
    s = grp.s
    out_shape = [jax.ShapeDtypeStruct((m_total, y_width), BF16)]
    out_specs = [pl.BlockSpec((grp.r, y_block), lambda b, h, t: (grp.rb(b, t), h))]
    for tail in state_shapes:
        out_shape.append(jax.ShapeDtypeStruct((depth, grp.n_seq, y_width // y_block) + tail, F32))
        out_specs.append(pl.BlockSpec((None, s, 1) + tail, lambda b, h, t: (l, b, h) + (0,) * len(tail)))
    for shp, spec in extra_outs:
        out_shape.append(shp)
        out_specs.append(spec)
    n_in = len(args)
    aliases = {}
    in_specs = list(in_specs)
    args = list(args)
    for j, arr in enumerate(prev):
        if arr is not None:
            aliases[len(args)] = j
            in_specs.append(pl.BlockSpec(memory_space=pl.ANY))
            args.append(arr)
    n_alias = len(args) - n_in

    def wrapped(*refs):
        body(*refs[:n_in], *refs[n_in + n_alias:])

    return pl.pallas_call(
        wrapped,
        out_shape=out_shape,
        grid=(grp.nb, y_width // y_block, grp.nt),
        in_specs=in_specs,
        out_specs=out_specs,
        scratch_shapes=scratch,
        input_output_aliases=aliases,
        compiler_params=_cparams(("arbitrary",) * 3),
        name=name,
    )(*args)


def _mlstm_call(grp, l, depth, m_total, hmain, gates, gate_bias, norm_w, c0, n0, m0, prev):
    r, s = grp.r, grp.s
    row = lambda off, w: pl.BlockSpec((r, w), lambda b, h, t: (grp.rb(b, t), off // w + h))
    st = lambda *tail: pl.BlockSpec((None, s, 1) + tail, lambda b, h, t: (l, b, h, 0, 0))
    in_specs = [row(C_MLQ, ML_DK), row(C_MLK, ML_DK), row(C_MLV, ML_DV), row(C_MLO, ML_DV),
                pl.BlockSpec((r, LANE), lambda b, h, t: (grp.rb(b, t), 0)),
                pl.BlockSpec((1, LANE), lambda b, h, t: (0, 0)),
                pl.BlockSpec((1, ML_DV), lambda b, h, t: (0, h)),
                st(ML_DV, ML_DK), st(1, ML_DK), st(1, 1)]
    scratch = [pltpu.VMEM((s, ML_DV, ML_DK), F32), pltpu.VMEM((s, 1, ML_DK), F32),
               pltpu.VMEM((s, 1, LANE), F32)]
    return _mixer_call(functools.partial(_mlstm_body, r, grp.g), grp, l, depth, m_total, in_specs,
                       [hmain, hmain, hmain, hmain, gates, gate_bias, norm_w, c0, n0, m0],
                       ML_HEADS * ML_DV, ML_DV, [(ML_DV, ML_DK), (1, ML_DK), (1, 1)], [], scratch, prev, "mlstm")


def _gdn_call(grp, l, depth, m_total, hmain, gates, conv_w, a_log_row, dt_bias_row, norm_w, bufpad, s0, prev):
    r, s = grp.r, grp.s
    row = lambda off, w: pl.BlockSpec((r, w), lambda b, h, t: (grp.rb(b, t), off // w + h))
    cw = lambda off, w: pl.BlockSpec((CONV_W, w), lambda b, h, t: (0, off // w + h))
    bf = lambda off, w: pl.BlockSpec((s * SUBLANE, w), lambda b, h, t: (b, off // w + h))
    one = pl.BlockSpec((1, LANE), lambda b, h, t: (0, 0))
    in_specs = [row(C_GDQ, GDN_DK), row(C_GDK, GDN_DK), row(C_GDV, GDN_DV), row(C_GDZ, GDN_DV),
                pl.BlockSpec((r, LANE), lambda b, h, t: (grp.rb(b, t), 0)),
                cw(0, GDN_DK), cw(GDN_QK, GDN_DK), cw(2 * GDN_QK, GDN_DV),
                one, one, pl.BlockSpec((1, GDN_DV), lambda b, h, t: (0, 0)),
                bf(0, GDN_DK), bf(GDN_QK, GDN_DK), bf(2 * GDN_QK, GDN_DV),
                pl.BlockSpec((None, s, 1, GDN_DK, GDN_DV), lambda b, h, t: (l, b, h, 0, 0))]
    nrow = grp.n_seq * SUBLANE
    tail = lambda w: (jax.ShapeDtypeStruct((nrow, GDN_HEADS * w), F32),
                      pl.BlockSpec((s * SUBLANE, w), lambda b, h, t: (b, h)))
    scratch = [pltpu.VMEM((s, GDN_DK, GDN_DV), F32),
               pltpu.VMEM((s * SUBLANE, GDN_DK), F32), pltpu.VMEM((s * SUBLANE, GDN_DK), F32),
               pltpu.VMEM((s * SUBLANE, GDN_DV), F32)]
    return _mixer_call(functools.partial(_gdn_body, r, grp.g), grp, l, depth, m_total, in_specs,
                       [hmain, hmain, hmain, hmain, gates, conv_w, conv_w, conv_w, a_log_row, dt_bias_row,
                        norm_w, bufpad, bufpad, bufpad, s0],
                       GDN_HEADS * GDN_DV, GDN_DV, [(GDN_DK, GDN_DV)],
                       [tail(GDN_DK), tail(GDN_DK), tail(GDN_DV)], scratch, prev, "gdn")


def _rglru_call(grp, l, depth, m_total, hmain, conv_w, conv_b, w_a, w_x, b_a, b_x, lam, bufpad, h0, prev):
    r, s = grp.r, grp.s
    w = LRU_BLOCK
    row = lambda off: pl.BlockSpec((r, w), lambda b, h, t: (grp.rb(b, t), off // w + h))
    vec = pl.BlockSpec((1, w), lambda b, h, t: (0, h))
    blk = pl.BlockSpec((1, w, w), lambda b, h, t: (h, 0, 0))
    in_specs = [row(C_RGX), row(C_RGY),
                pl.BlockSpec((CONV_W, w), lambda b, h, t: (0, h)), vec, blk, blk, vec, vec, vec,
                pl.BlockSpec((s * SUBLANE, w), lambda b, h, t: (b, h)),
                pl.BlockSpec((None, s, 1, 1, w), lambda b, h, t: (l, b, h, 0, 0))]
    tail = (jax.ShapeDtypeStruct((grp.n_seq * SUBLANE, LRU_WIDTH), F32),
            pl.BlockSpec((s * SUBLANE, w), lambda b, h, t: (b, h)))
    scratch = [pltpu.VMEM((s, 1, w), F32), pltpu.VMEM((s * SUBLANE, w), F32)]
    return _mixer_call(functools.partial(_rglru_body, r, grp.g), grp, l, depth, m_total, in_specs,
                       [hmain, hmain, conv_w, conv_b, w_a, w_x, b_a, b_x, lam, bufpad, h0],
                       LRU_WIDTH, w, [(1, w)], [tail], scratch, prev, "rglru")


def _pad_conv_state(buf):
    b, k, c = buf.shape
    return jnp.pad(buf.astype(F32), ((0, 0), (SUBLANE - k, 0), (0, 0))).reshape(b * SUBLANE, c)


def _unpad_conv_state(tail, n_seq):
    return tail.reshape(n_seq, SUBLANE, -1)[:, SUBLANE - (CONV_W - 1):, :]


R_I1, R_I2, R_W1, R_W2, R_P1, R_P2 = 0, 1, 2, 3, 4, 5
MOE_TM = 512
MOE_TD = 512


def _router_call(x, router, tm):
    m, d = x.shape
    n_exp = router.shape[1]
    wr = jnp.pad(router.astype(F32), ((0, 0), (0, LANE - n_exp)))

    def body(x_ref, w_ref, o_ref, cnt_ref, carry):
        i = pl.program_id(0)

        @pl.when(i == 0)
        def _():
            carry[...] = jnp.zeros_like(carry)

        logits = _dot_hi(x_ref[...], w_ref[...])
        lane = lax.broadcasted_iota(jnp.int32, logits.shape, 1)
        lg = jnp.where(lane < n_exp, logits, -jnp.inf)
        v1 = jnp.max(lg, axis=1, keepdims=True)
        i1 = jnp.min(jnp.where(lg == v1, lane, LANE), axis=1, keepdims=True)
        lg2 = jnp.where(lane == i1, -jnp.inf, lg)
        v2 = jnp.max(lg2, axis=1, keepdims=True)
        i2 = jnp.min(jnp.where(lg2 == v2, lane, LANE), axis=1, keepdims=True)
        e2 = jnp.exp(v2 - v1)
        w1 = 1.0 / (1.0 + e2)
        w2 = e2 / (1.0 + e2)
        sel = ((lane == i1) | (lane == i2)).astype(F32)
        below = (lax.broadcasted_iota(jnp.int32, (tm, tm), 1)
                 < lax.broadcasted_iota(jnp.int32, (tm, tm), 0)).astype(BF16)
        rank = jnp.dot(below, sel.astype(BF16), preferred_element_type=F32) + carry[0:1, :]
        p1 = jnp.sum(jnp.where(lane == i1, rank, 0.0), axis=1, keepdims=True)
        p2 = jnp.sum(jnp.where(lane == i2, rank, 0.0), axis=1, keepdims=True)
        rec = jnp.where(lane == R_I1, i1.astype(F32), jnp.where(lane == R_I2, i2.astype(F32), 0.0))
        rec = jnp.where(lane == R_W1, w1, jnp.where(lane == R_W2, w2, rec))
        o_ref[...] = jnp.where(lane == R_P1, p1, jnp.where(lane == R_P2, p2, rec))
        carry[...] = carry[...] + jnp.sum(sel, axis=0, keepdims=True)
        cnt_ref[...] = carry[...]

    return pl.pallas_call(
        body,
        out_shape=[jax.ShapeDtypeStruct((m, LANE), F32), jax.ShapeDtypeStruct((SUBLANE, LANE), F32)],
        grid=(m // tm,),
        in_specs=[pl.BlockSpec((tm, d), lambda i: (i, 0)), pl.BlockSpec((d, LANE), lambda i: (0, 0))],
        out_specs=[pl.BlockSpec((tm, LANE), lambda i: (i, 0)), pl.BlockSpec((SUBLANE, LANE), lambda i: (0, 0))],
        scratch_shapes=[pltpu.VMEM((SUBLANE, LANE), F32)],
        compiler_params=_cparams(("arbitrary",)),
        name="router",
    )(x, wr)


def _source_table(starts, e_arr, p_arr, m, ns):
    td = MOE_TD

    def body(start_ref, e_ref, p_ref, tbl_ref):
        i = pl.program_id(0)

        @pl.when(i == 0)
        def _():
            def clear(r, c):
                tbl_ref[r] = 0
                return c

            lax.fori_loop(0, ns, clear, 0, unroll=8)

        base = i * td

        def put(j, c):
            tbl_ref[start_ref[e_ref[0, j]] + p_ref[0, j]] = base + (j >> 1)
            return c

        lax.fori_loop(0, 2 * td, put, 0, unroll=8)

    smem = pl.BlockSpec((None, 1, 2 * td), lambda i, s: (i, 0, 0), memory_space=pltpu.SMEM)
    return pl.pallas_call(
        body,
        grid_spec=pltpu.PrefetchScalarGridSpec(
            num_scalar_prefetch=1, grid=(m // td,), in_specs=[smem, smem],
            out_specs=pl.BlockSpec(memory_space=pltpu.SMEM)),
        out_shape=jax.ShapeDtypeStruct((ns,), jnp.int32),
        compiler_params=_cparams(("arbitrary",)),
        name="moe_table",
    )(starts, e_arr, p_arr)


def _gather_rows(copy_of, n_rows):
    def issue(r, c):
        copy_of(r).start()
        return c

    def drain(r, c):
        copy_of(r).wait()
        return c

    lax.fori_loop(0, n_rows, issue, 0, unroll=8)
    lax.fori_loop(0, n_rows, drain, 0, unroll=8)


def _dispatch(x, table, n_used):
    m, d = x.shape
    ns = table.shape[0]
    tm = MOE_TM

    def body(nu_ref, src_ref, x_ref, o_ref, buf, sem):
        @pl.when(pl.program_id(0) < nu_ref[0])
        def _():
            _gather_rows(lambda r: pltpu.make_async_copy(x_ref.at[pl.ds(src_ref[0, r], 1)],
                                                        buf.at[pl.ds(r, 1)], sem), tm)
            o_ref[...] = buf[...].astype(BF16)

    return pl.pallas_call(
        body,
        grid_spec=pltpu.PrefetchScalarGridSpec(
            num_scalar_prefetch=1, grid=(ns // tm,),
            in_specs=[pl.BlockSpec((None, 1, tm), lambda i, nu: (i, 0, 0), memory_space=pltpu.SMEM),
                      pl.BlockSpec(memory_space=pl.ANY)],
            out_specs=pl.BlockSpec((tm, d), lambda i, nu: (jnp.minimum(i, nu[0] - 1), 0)),
            scratch_shapes=[pltpu.VMEM((tm, d), F32), pltpu.SemaphoreType.DMA(())]),
        out_shape=jax.ShapeDtypeStruct((ns, d), BF16),
        compiler_params=_cparams(("arbitrary",)),
        name="moe_dispatch",
    )(n_used, table.reshape(ns // tm, 1, tm), x)


def _grouped(rows, ws, tile_expert, tile_first, n_used, epilogue, out_dtype, n_cols, tn, name):
    ns, k = rows.shape
    nt = ns // MOE_TM
    used = lambda i, nu: jnp.minimum(i, nu[0] - 1)
    w_spec = pl.BlockSpec((None, k, tn), lambda n, i, te, tf, nu: (te[i], 0, n))

    def body(te_ref, tf_ref, nu_ref, x_ref, *refs):
        w_refs, o_ref, wb_refs = refs[:len(ws)], refs[len(ws)], refs[len(ws) + 1:]
        i = pl.program_id(1)

        @pl.when(tf_ref[i] == 1)
        def _():
            for w, wb in zip(w_refs, wb_refs):
                wb[...] = w[...].astype(BF16)

        @pl.when(i < nu_ref[0])
        def _():
            xv = x_ref[...].astype(BF16)
            accs = [jnp.dot(xv, wb[...], preferred_element_type=F32) for wb in wb_refs]
            o_ref[...] = epilogue(accs).astype(o_ref.dtype)

    return pl.pallas_call(
        body,
        grid_spec=pltpu.PrefetchScalarGridSpec(
            num_scalar_prefetch=3, grid=(n_cols // tn, nt),
            in_specs=[pl.BlockSpec((MOE_TM, k), lambda n, i, te, tf, nu: (used(i, nu), 0))] + [w_spec] * len(ws),
            out_specs=pl.BlockSpec((MOE_TM, tn), lambda n, i, te, tf, nu: (used(i, nu), n)),
            scratch_shapes=[pltpu.VMEM((k, tn), BF16) for _ in ws]),
        out_shape=jax.ShapeDtypeStruct((ns, n_cols), out_dtype),
        compiler_params=_cparams(("arbitrary", "arbitrary")),
        name=name,
    )(tile_expert, tile_first, n_used, rows, *ws)


def _combine_ln(x, ys, starts, e_arr, p_arr, rec, ple, gamma, beta, split_rows):
    tm = MOE_TD
    m, n = x.shape
    tiles_a = None if split_rows is None else split_rows // tm

    def body(start_ref, es_ref, ps_ref, ys_ref, x_ref, r_ref, e_ref, g_ref, b_ref, oa_ref, ob_ref, ybuf, sem):
        i = pl.program_id(0)
        _gather_rows(lambda j: pltpu.make_async_copy(
            ys_ref.at[pl.ds(start_ref[es_ref[0, j]] + ps_ref[0, j], 1)],
            ybuf.at[j & 1, pl.ds(j >> 1, 1)], sem), 2 * tm)
        r = r_ref[...]
        lane = lax.broadcasted_iota(jnp.int32, r.shape, 1)
        w1 = jnp.sum(jnp.where(lane == R_W1, r, 0.0), axis=1, keepdims=True)
        w2 = jnp.sum(jnp.where(lane == R_W2, r, 0.0), axis=1, keepdims=True)
        y = DEEPNORM_ALPHA * x_ref[...] + (w1 * ybuf[0] + w2 * ybuf[1]) + e_ref[...]
        mu = jnp.mean(y, axis=-1, keepdims=True)
        yc = y - mu
        var = jnp.mean(yc * yc, axis=-1, keepdims=True)
        out = yc * lax.rsqrt(var + LN_EPS) * g_ref[...] + b_ref[...]
        if split_rows is None:
            oa_ref[...] = out
            ob_ref[...] = out.astype(BF16)
        else:
            @pl.when(i < tiles_a)
            def _():
                oa_ref[...] = out

            @pl.when(i >= tiles_a)
            def _():
                ob_ref[...] = out

    row = lambda i, s: (i, 0)
    vec = pl.BlockSpec((1, n), lambda i, s: (0, 0))
    smem = pl.BlockSpec((None, 1, 2 * tm), lambda i, s: (i, 0, 0), memory_space=pltpu.SMEM)
    if split_rows is None:
        out_shape = [jax.ShapeDtypeStruct((m, n), F32), jax.ShapeDtypeStruct((m, n), BF16)]
        out_specs = [pl.BlockSpec((tm, n), row), pl.BlockSpec((tm, n), row)]
    else:
        out_shape = [jax.ShapeDtypeStruct((split_rows, n), F32), jax.ShapeDtypeStruct((m - split_rows, n), F32)]
        out_specs = [pl.BlockSpec((tm, n), lambda i, s: (jnp.minimum(i, tiles_a - 1), 0)),
                     pl.BlockSpec((tm, n), lambda i, s: (jnp.maximum(i - tiles_a, 0), 0))]
    return pl.pallas_call(
        body,
        grid_spec=pltpu.PrefetchScalarGridSpec(
            num_scalar_prefetch=1, grid=(m // tm,),
            in_specs=[smem, smem, pl.BlockSpec(memory_space=pl.ANY), pl.BlockSpec((tm, n), row),
                      pl.BlockSpec((tm, LANE), row), pl.BlockSpec((tm, n), row), vec, vec],
            out_specs=out_specs,
            scratch_shapes=[pltpu.VMEM((2, tm, n), F32), pltpu.SemaphoreType.DMA(())]),
        out_shape=out_shape,
        compiler_params=_cparams(("arbitrary",)),
        name="moe_combine_ln",
    )(starts, e_arr, p_arr, ys, x, rec, ple, gamma.reshape(1, n), beta.reshape(1, n))


def _moe_ln(x, ple, router, w1, w3, w2, j, gamma, beta, split_rows):
    m, d = x.shape
    _, ne, _, fe = w1.shape
    w1, w3, w2 = (w.reshape((-1,) + w.shape[2:]) for w in (w1, w3, w2))
    rec, counts = _router_call(x, router, 512)
    cnt = counts[0, :ne].astype(jnp.int32)
    tiles_e = (cnt + MOE_TM - 1) // MOE_TM
    tile_end = jnp.cumsum(tiles_e)
    tile_beg = tile_end - tiles_e
    n_used = tile_end[ne - 1]
    nt = (2 * m) // MOE_TM + ne
    ids = jnp.minimum(jnp.arange(nt, dtype=jnp.int32), n_used - 1)
    tile_expert = jnp.sum((ids[:, None] >= tile_end[None, :]).astype(jnp.int32), axis=1)
    tile_first = (jnp.arange(nt, dtype=jnp.int32) == tile_beg[tile_expert]).astype(jnp.int32)
    starts = (tile_beg * MOE_TM).astype(jnp.int32)
    slots = lambda a, b: jnp.stack([rec[:, a], rec[:, b]], axis=1).astype(jnp.int32).reshape(m // MOE_TD, 1, 2 * MOE_TD)
    e_arr, p_arr = slots(R_I1, R_I2), slots(R_P1, R_P2)
    ns = nt * MOE_TM

    nu = n_used.reshape(1)
    xs = _dispatch(x, _source_table(starts, e_arr, p_arr, m, ns), nu)
    tile_w = tile_expert + j * ne
    hmid = _grouped(xs, [w1, w3], tile_w, tile_first, nu, lambda a: _silu(a[0]) * a[1], BF16, fe, 256, "moe_up")
    ys = _grouped(hmid, [w2], tile_w, tile_first, nu, lambda a: a[0], F32, d, 512, "moe_down")
    return _combine_ln(x, ys, starts, e_arr, p_arr, rec, ple, gamma, beta, split_rows)


def _run_group_mixers(grp, l, depth, m_total, hmain, gates, st, prm, prev):
    b = grp.n_seq
    ya, ml_c, ml_n, ml_m = _mlstm_call(
        grp, l, depth, m_total, hmain, gates, prm["gate_bias"], prm["ml_norm_w"],
        st["ml_C"].astype(F32), st["ml_n"].astype(F32).reshape(depth, b, ML_HEADS, 1, ML_DK),
        st["ml_m"].astype(F32).reshape(depth, b, ML_HEADS, 1, 1),
        [prev["ya"], prev["ml_C"], prev["ml_n"], prev["ml_m"]])
    yb, rg_h, rg_tail = _rglru_call(
        grp, l, depth, m_total, hmain, prm["rg_conv_w"], prm["rg_conv_b"], prm["rg_w_a"], prm["rg_w_x"],
        prm["rg_b_a"], prm["rg_b_x"], prm["rg_lambda"], _pad_conv_state(st["rg_conv"][l]),
        st["rg_h"].astype(F32).reshape(depth, b, LRU_BLOCKS, 1, LRU_BLOCK), [prev["yb"], prev["rg_h"]])
    yc, gd_s, tq, tk, tv = _gdn_call(
        grp, l, depth, m_total, hmain, gates, prm["gd_conv_w"], prm["gd_a_log"], prm["gd_dt_bias"],
        prm["gd_norm_w"], _pad_conv_state(st["gd_conv"][l]), st["gd_S"].astype(F32),
        [prev["yc"], prev["gd_S"]])
    out = dict(ya=ya, yb=yb, yc=yc, ml_C=ml_c, ml_n=ml_n, ml_m=ml_m, rg_h=rg_h, gd_S=gd_s)
    tails = dict(rg_conv=_unpad_conv_state(rg_tail, b),
                 gd_conv=jnp.concatenate([_unpad_conv_state(x, b) for x in (tq, tk, tv)], axis=-1))
    return out, tails


_STATE_KEYS = ("ml_C", "ml_n", "ml_m", "rg_h", "gd_S")


def kernel(x_prompt, x_sample, state_mlstm_C, state_mlstm_n, state_mlstm_m, state_rglru_h, state_rglru_conv, state_gdn_S, state_gdn_conv, p_prompt, p_sample, w_in, ml_b_i, ml_b_f, ml_norm_w, rg_conv_w, rg_conv_b, rg_w_a, rg_b_a, rg_w_x, rg_b_x, rg_lambda, gd_conv_w, gd_A_log, gd_dt_bias, gd_norm_w, w_up_mlstm, w_up_rglru, w_up_gdn, w_out, ln1_g, ln1_b, ffn_w1, ffn_w3, ffn_w2, moe_router, moe_w1, moe_w3, moe_w2, ple_w, ple_gate_w, ln2_g, ln2_b):
    bp, tp, d = x_prompt.shape
    bs, ts, _ = x_sample.shape
    depth = w_in.shape[0]
    gp = _Group(bp, tp, 0)
    gs = _Group(bs, ts, bp * tp)
    m = gp.rows + gs.rows
    pd = x_prompt.dtype

    zeros = lambda *shape: jnp.zeros(shape, F32)
    st_p = dict(ml_C=zeros(depth, bp, ML_HEADS, ML_DV, ML_DK), ml_n=zeros(depth, bp, ML_HEADS, ML_DK),
                ml_m=zeros(depth, bp, ML_HEADS), rg_h=zeros(depth, bp, LRU_WIDTH),
                rg_conv=zeros(depth, bp, CONV_W - 1, LRU_WIDTH),
                gd_S=zeros(depth, bp, GDN_HEADS, GDN_DK, GDN_DV),
                gd_conv=zeros(depth, bp, CONV_W - 1, GDN_CONV_DIM))
    st_s = dict(ml_C=state_mlstm_C, ml_n=state_mlstm_n, ml_m=state_mlstm_m, rg_h=state_rglru_h,
                rg_conv=state_rglru_conv, gd_S=state_gdn_S, gd_conv=state_gdn_conv)

    x = jnp.concatenate([x_prompt.reshape(gp.rows, d), x_sample.reshape(gs.rows, d)], axis=0).astype(F32)
    xb = x.astype(BF16)
    ident = lambda accs, extras, ids: [accs[0]]
    lane_row = lambda v, off: jnp.zeros((1, LANE), F32).at[0, off:off + v.shape[0]].set(v.astype(F32))
    acc_p = {k: None for k in _STATE_KEYS}
    acc_s = {k: None for k in _STATE_KEYS}
    tails_p, tails_s = [], []

    for l in range(depth):
        hmain = _in_proj(xb, w_in, l, 1024)
        w_gate = jnp.pad(jnp.concatenate([w_in[l, :, W_GATES_A:W_GATES_A + SUBLANE],
                                          w_in[l, :, W_GATES_B:W_GATES_B + SUBLANE]], axis=1),
                         ((0, 0), (0, LANE - 2 * SUBLANE)))
        gates = _proj([xb], [w_gate], [pl.BlockSpec((d, LANE), lambda n, i: (0, n))], [], [], ident,
                      [jax.ShapeDtypeStruct((m, LANE), F32)], LANE, 1024, LANE, "gate_proj")[0]

        prm = dict(
            gate_bias=lane_row(ml_b_i[l], G_MLI) + lane_row(ml_b_f[l], G_MLF),
            ml_norm_w=ml_norm_w[l].astype(F32).reshape(1, -1),
            rg_conv_w=rg_conv_w[l].astype(F32), rg_conv_b=rg_conv_b[l].astype(F32).reshape(1, -1),
            rg_w_a=rg_w_a[l], rg_w_x=rg_w_x[l], rg_b_a=rg_b_a[l].astype(F32).reshape(1, -1),
            rg_b_x=rg_b_x[l].astype(F32).reshape(1, -1), rg_lambda=rg_lambda[l].astype(F32).reshape(1, -1),
            gd_conv_w=gd_conv_w[l].astype(F32), gd_a_log=lane_row(gd_A_log[l], G_GDA),
            gd_dt_bias=lane_row(gd_dt_bias[l], G_GDA), gd_norm_w=gd_norm_w[l].astype(F32).reshape(1, -1))
        out_p, tl_p = _run_group_mixers(gp, l, depth, m, hmain, gates, st_p, prm,
                                        dict(acc_p, ya=None, yb=None, yc=None))
        out_s, tl_s = _run_group_mixers(gs, l, depth, m, hmain, gates, st_s, prm,
                                        dict(acc_s, ya=out_p["ya"], yb=out_p["yb"], yc=out_p["yc"]))
        acc_p = {k: out_p[k] for k in _STATE_KEYS}
        acc_s = {k: out_s[k] for k in _STATE_KEYS}
        tails_p.append(tl_p)
        tails_s.append(tl_s)
        ya, yb, yc = out_s["ya"], out_s["yb"], out_s["yc"]

        tn, tm = 512, 512
        up_spec = pl.BlockSpec((None, ya.shape[1], tn), lambda n, i: (l, 0, n))
        mg_spec = lambda br: pl.BlockSpec((tm, tn), lambda n, i: (i, (C_MG + br * d) // tn + n))
        merged = _proj([ya, yb, yc], [w_up_mlstm, w_up_rglru, w_up_gdn], [up_spec] * 3,
                       [hmain] * 3, [mg_spec(0), mg_spec(1), mg_spec(2)],
                       lambda accs, ex, ids: [_sigmoid(ex[0]) * accs[0] + _sigmoid(ex[1]) * accs[1]
                                              + _sigmoid(ex[2]) * accs[2]],
                       [jax.ShapeDtypeStruct((m, d), BF16)], d, tm, tn, "merge")[0]
        x, xb = _rowfull_ln(merged, w_out[l].astype(BF16), x, None, ln1_g[l].astype(F32), ln1_b[l].astype(F32),
                            512, d, "out_proj_ln")

        p_l = jnp.concatenate([p_prompt[l].reshape(gp.rows, -1), p_sample[l].reshape(gs.rows, -1)], axis=0)
        ple = _proj([xb, p_l], [ple_gate_w, ple_w],
                    [pl.BlockSpec((None, d, 1024), lambda n, i: (l, 0, n)),
                     pl.BlockSpec((None, P_DIM, 1024), lambda n, i: (l, 0, n))],
                    [], [], lambda accs, ex, ids: [_sigmoid(accs[0]) * accs[1]],
                    [jax.ShapeDtypeStruct((m, d), F32)], d, 1024, 1024, "ple")[0]
        j = l // 2
        split = gp.rows if l == depth - 1 else None
        g2, b2 = ln2_g[l].astype(F32), ln2_b[l].astype(F32)
        if l % 2 == 0:
            ff = ffn_w1.shape[2]
            hmid = _proj([xb], [ffn_w1, ffn_w3], [pl.BlockSpec((None, d, 512), lambda n, i: (j, 0, n))] * 2,
                         [], [], lambda accs, ex, ids: [_silu(accs[0]) * accs[1]],
                         [jax.ShapeDtypeStruct((m, ff), BF16)], ff, 1024, 512, "ffn_up", x_of_w=(0, 0))[0]
            x, xb = _rowfull_ln(hmid, ffn_w2[j].astype(BF16), x, ple, g2, b2, 512, 1408, "ffn_down_ln",
                                split_rows=split)
        else:
            x, xb = _moe_ln(x, ple, moe_router[j], moe_w1, moe_w3, moe_w2, j, g2, b2, split)

    y_prompt = x.reshape(bp, tp, d).astype(pd)
    y_sample = xb.reshape(bs, ts, d).astype(x_sample.dtype)

    def finish(acc, tails, n_seq, dtypes):
        conv = {k: jnp.stack([tl[k] for tl in tails]) for k in ("rg_conv", "gd_conv")}
        vals = dict(ml_C=acc["ml_C"], ml_n=acc["ml_n"].reshape(depth, n_seq, ML_HEADS, ML_DK),
                    ml_m=acc["ml_m"].reshape(depth, n_seq, ML_HEADS),
                    rg_h=acc["rg_h"].reshape(depth, n_seq, LRU_WIDTH), rg_conv=conv["rg_conv"],
                    gd_S=acc["gd_S"], gd_conv=conv["gd_conv"])
        names = ("ml_C", "ml_n", "ml_m", "rg_h", "rg_conv", "gd_S", "gd_conv")
        return tuple(vals[k].astype(dtypes[k]) for k in names)

    return ((y_prompt, y_sample) + finish(acc_p, tails_p, bp, {k: pd for k in st_p})
            + finish(acc_s, tails_s, bs, {k: v.dtype for k, v in st_s.items()}))
```

```python
import functools
import math

import jax
import jax.numpy as jnp
from jax import lax
from jax.experimental import pallas as pl
from jax.experimental.pallas import tpu as pltpu

F32 = jnp.float32
BF16 = jnp.bfloat16

D_MODEL = 2048
DEPTH = 2
P_DIM = 256
CONV_W = 4
N_BRANCH = 3
ML_HEADS = 4
ML_DK = 128
ML_DV = 256
ML_GATE_CAP = 15.0
LRU_WIDTH = 1024
LRU_BLOCKS = 4
LRU_BLOCK = LRU_WIDTH // LRU_BLOCKS
LRU_C = 8.0
GDN_HEADS = 4
GDN_DK = 128
GDN_DV = 256
GDN_QK = GDN_HEADS * GDN_DK
GDN_CONV_DIM = 2 * GDN_QK + GDN_HEADS * GDN_DV
N_EXPERTS = 8
DEEPNORM_ALPHA = (2 * DEPTH) ** 0.25
LN_EPS = 1e-5
RMS_EPS = 1e-6
L2_EPS = 1e-6

LANE = 128
SUBLANE = 8
CHUNK = 128
VMEM_LIMIT = 56 * 1024 * 1024

C_MLQ, C_MLK, C_MLV, C_MLO = 0, 512, 1024, 2048
C_RGX, C_RGY = 3072, 4096
C_GDQ, C_GDK, C_GDV, C_GDZ = 5120, 5632, 6144, 7168
C_MG = 8192
N_MAIN = C_MG + N_BRANCH * D_MODEL
W_GATES_A, W_GATES_B = 2048, 7176
G_MLI, G_MLF, G_GDB, G_GDA = 0, 4, 8, 12
PROJ_TN = 1024

HIGHEST = lax.Precision.HIGHEST


def _cparams(sem):
    return pltpu.CompilerParams(dimension_semantics=sem, vmem_limit_bytes=VMEM_LIMIT)


def _dot(a, b):
    return jnp.dot(a.astype(BF16), b.astype(BF16), preferred_element_type=F32)


def _dot_nt(a, b):
    return lax.dot_general(a.astype(BF16), b.astype(BF16), (((1,), (1,)), ((), ())),
                           preferred_element_type=F32)


def _dot_tn(a, b):
    return lax.dot_general(a.astype(BF16), b.astype(BF16), (((0,), (0,)), ((), ())),
                           preferred_element_type=F32)


def _dot_hi(a, b):
    return jnp.dot(a, b, precision=HIGHEST, preferred_element_type=F32)


def _bmm(a, b):
    return jnp.einsum('cij,cjk->cik', a.astype(BF16), b.astype(BF16), preferred_element_type=F32)


def _bmm_nt(a, b):
    return jnp.einsum('cik,cjk->cij', a.astype(BF16), b.astype(BF16), preferred_element_type=F32)


def _bmm_tn(a, b):
    return jnp.einsum('csi,csj->cij', a.astype(BF16), b.astype(BF16), preferred_element_type=F32)


def _sigmoid(x):
    return 1.0 / (1.0 + jnp.exp(-x))


def _silu(x):
    return x * _sigmoid(x)


def _softplus(x):
    return jnp.maximum(x, 0.0) + jnp.log1p(jnp.exp(-jnp.abs(x)))


def _proj(xs, ws, w_specs, extras, e_specs, epilogue, out_shapes, n_cols, tm, tn, name,
          lead_grid=(), x_of_w=None):
    m = xs[0].shape[0]
    nl = len(lead_grid)
    grid = tuple(lead_grid) + (n_cols // tn, m // tm)
    x_specs = [pl.BlockSpec((tm, x.shape[1]), lambda *g: (g[-1], 0)) for x in xs]
    o_specs = [pl.BlockSpec((tm, tn), lambda *g: (g[-1], g[-2]) if nl == 0 else
                            (g[-1], g[0] * (n_cols // tn) + g[-2])) for _ in out_shapes]
    scratch = [pltpu.VMEM(tuple(d for d in s.block_shape if d is not None), BF16) for s in w_specs]
    xw = tuple(range(len(ws))) if x_of_w is None else tuple(x_of_w)

    def body(*refs):
        nx, nw, ne, no = len(xs), len(ws), len(extras), len(out_shapes)
        x_refs = refs[:nx]
        w_refs = refs[nx:nx + nw]
        e_refs = refs[nx + nw:nx + nw + ne]
        o_refs = refs[nx + nw + ne:nx + nw + ne + no]
        wb_refs = refs[nx + nw + ne + no:]

        @pl.when(pl.program_id(nl + 1) == 0)
        def _():
            for w, wb in zip(w_refs, wb_refs):
                wb[...] = w[...].astype(BF16)

        xv = [x[...].astype(BF16) for x in x_refs]
        accs = [jnp.dot(xv[j], wb[...], preferred_element_type=F32) for j, wb in zip(xw, wb_refs)]
        outs = epilogue(accs, [e[...] for e in e_refs], [pl.program_id(i) for i in range(nl)])
        for o, v in zip(o_refs, outs):
            o[...] = v.astype(o.dtype)

    return pl.pallas_call(
        body,
        out_shape=out_shapes,
        grid=grid,
        in_specs=x_specs + list(w_specs) + list(e_specs),
        out_specs=o_specs,
        scratch_shapes=scratch,
        compiler_params=_cparams(("arbitrary",) * len(grid)),
        name=name,
    )(*xs, *ws, *extras)


def _in_proj(xb, wt, l, tm):
    m, d = xb.shape
    tn = PROJ_TN
    n_blocks = N_MAIN // tn
    first_a = W_GATES_A // tn
    first_b = (W_GATES_B - SUBLANE) // tn
    nxt = 2 * SUBLANE

    def body(x_ref, wm_ref, wn_ref, o_ref, wb_ref):
        n = pl.program_id(0)

        @pl.when(pl.program_id(1) == 0)
        def _():
            def shifted(s):
                wcat = jnp.concatenate([wm_ref[...], wn_ref[...]], axis=0)
                wb_ref[...] = wcat[s:s + tn, :].astype(BF16)

            @pl.when(n < first_a)
            def _():
                wb_ref[...] = wm_ref[...].astype(BF16)

            @pl.when((n >= first_a) & (n < first_b))
            def _():
                shifted(SUBLANE)

            @pl.when(n >= first_b)
            def _():
                shifted(2 * SUBLANE)

        o_ref[...] = lax.dot_general(x_ref[...], wb_ref[...], (((1,), (1,)), ((), ())),
                                     preferred_element_type=F32)

    return pl.pallas_call(
        body,
        out_shape=jax.ShapeDtypeStruct((m, N_MAIN), F32),
        grid=(n_blocks, m // tm),
        in_specs=[pl.BlockSpec((tm, d), lambda n, i: (i, 0)),
                  pl.BlockSpec((None, tn, d), lambda n, i: (l, n, 0)),
                  pl.BlockSpec((None, nxt, d), lambda n, i: (l, (n + 1) * (tn // nxt), 0))],
        out_specs=pl.BlockSpec((tm, tn), lambda n, i: (i, n)),
        scratch_shapes=[pltpu.VMEM((tn, d), BF16)],
        compiler_params=_cparams(("arbitrary", "arbitrary")),
        name="in_proj",
    )(xb, wt, wt)


def _gate_proj(xb, wt, l, tm):
    m, d = xb.shape
    blk_a, blk_b = W_GATES_A // LANE, W_GATES_B // LANE

    def body(x_ref, w_ref, o_ref):
        o_ref[...] = lax.dot_general(x_ref[...], w_ref[...].astype(BF16), (((1,), (1,)), ((), ())),
                                     preferred_element_type=F32)

    return pl.pallas_call(
        body,
        out_shape=jax.ShapeDtypeStruct((m, 2 * LANE), F32),
        grid=(2, m // tm),
        in_specs=[pl.BlockSpec((tm, d), lambda n, i: (i, 0)),
                  pl.BlockSpec((None, LANE, d), lambda n, i: (l, blk_a + n * (blk_b - blk_a), 0))],
        out_specs=pl.BlockSpec((tm, LANE), lambda n, i: (i, n)),
        compiler_params=_cparams(("arbitrary", "arbitrary")),
        name="gate_proj",
    )(xb, wt)


def _rowfull_ln(h, w_bf16, res, extra, gamma, beta, tm, tk, name, split_rows=None):
    m, k = h.shape
    n = w_bf16.shape[1]
    nk = k // tk
    has_extra = extra is not None
    n_in = 6 if has_extra else 5
    tiles_a = None if split_rows is None else split_rows // tm

    def body(*refs):
        h_ref, w_ref, r_ref = refs[:3]
        e_ref = refs[3] if has_extra else None
        g_ref, b_ref = refs[n_in - 2:n_in]
        oa_ref, ob_ref, acc_ref = refs[n_in:]
        i = pl.program_id(0)
        kk = pl.program_id(1)

        @pl.when(kk == 0)
        def _():
            acc_ref[...] = jnp.zeros_like(acc_ref)

        acc_ref[...] += jnp.dot(h_ref[...], w_ref[...], preferred_element_type=F32)

        @pl.when(kk == nk - 1)
        def _():
            y = DEEPNORM_ALPHA * r_ref[...] + acc_ref[...]
            if e_ref is not None:
                y = y + e_ref[...].astype(F32)
            mu = jnp.mean(y, axis=-1, keepdims=True)
            yc = y - mu
            var = jnp.mean(yc * yc, axis=-1, keepdims=True)
            out = yc * lax.rsqrt(var + LN_EPS) * g_ref[...] + b_ref[...]
            if split_rows is None:
                oa_ref[...] = out
                ob_ref[...] = out.astype(BF16)
            else:
                @pl.when(i < tiles_a)
                def _():
                    oa_ref[...] = out

                @pl.when(i >= tiles_a)
                def _():
                    ob_ref[...] = out

    row = lambda i, j: (i, 0)
    in_specs = [pl.BlockSpec((tm, tk), lambda i, j: (i, j)),
                pl.BlockSpec((tk, n), lambda i, j: (j, 0)),
                pl.BlockSpec((tm, n), row)]
    args = [h, w_bf16, res]
    if has_extra:
        in_specs.append(pl.BlockSpec((tm, n), row))
        args.append(extra)
    in_specs += [pl.BlockSpec((1, n), lambda i, j: (0, 0))] * 2
    args += [gamma.reshape(1, n), beta.reshape(1, n)]
    if split_rows is None:
        out_shape = [jax.ShapeDtypeStruct((m, n), F32), jax.ShapeDtypeStruct((m, n), BF16)]
        out_specs = [pl.BlockSpec((tm, n), row), pl.BlockSpec((tm, n), row)]
    else:
        out_shape = [jax.ShapeDtypeStruct((split_rows, n), F32), jax.ShapeDtypeStruct((m - split_rows, n), F32)]
        out_specs = [pl.BlockSpec((tm, n), lambda i, j: (jnp.minimum(i, tiles_a - 1), 0)),
                     pl.BlockSpec((tm, n), lambda i, j: (jnp.maximum(i - tiles_a, 0), 0))]
    return pl.pallas_call(
        body,
        out_shape=out_shape,
        grid=(m // tm, nk),
        in_specs=in_specs,
        out_specs=out_specs,
        scratch_shapes=[pltpu.VMEM((tm, n), F32)],
        compiler_params=_cparams(("arbitrary", "arbitrary")),
        name=name,
    )(*args)


def _seq_masks(g, lg_shift):
    row = lax.broadcasted_iota(jnp.int32, (CHUNK, CHUNK), 0)
    col = lax.broadcasted_iota(jnp.int32, (CHUNK, CHUNK), 1)
    if g == 1:
        return col <= row, col < row
    same = (row >> lg_shift) == (col >> lg_shift)
    return same & (col <= row), same & (col < row)


def _per_seq_rows(vals, g, lg):
    if g == 1:
        return vals[0]
    return jnp.concatenate([jnp.broadcast_to(v, (lg, v.shape[1])) for v in vals], axis=0)


def _last_rows(col, g, lg):
    if g == 1:
        return col[:, CHUNK - 1:, :]
    return _per_seq_rows([col[0, (i + 1) * lg - 1:(i + 1) * lg, :] for i in range(g)], g, lg)[None]


def _split3(x):
    x1 = x.astype(BF16).astype(F32)
    r1 = x - x1
    x2 = r1.astype(BF16).astype(F32)
    x3 = (r1 - x2).astype(BF16).astype(F32)
    return x1, x2, x3


def _split2(x):
    hi = x.astype(BF16)
    return hi, (x - hi.astype(F32)).astype(BF16)


def _lanes3(col, lane_ids, first, other):
    c1, c2, c3 = _split3(col)
    x = jnp.where(lane_ids == first, c1, jnp.where(lane_ids == first + 1, c2,
                                                   jnp.where(lane_ids == first + 2, c3, 0.0)))
    return jnp.where((lane_ids >= other) & (lane_ids < other + 3), 1.0, x).astype(BF16)


def _cumsum_rows(col, tri_b, lane_ids):
    return jnp.sum(_bmm(tri_b, _lanes3(col, lane_ids, 0, LANE)), axis=-1, keepdims=True)


def _outer_sum(a_col, b_col, lane_ids):
    return _bmm_nt(_lanes3(a_col, lane_ids, 0, 3), _lanes3(b_col, lane_ids, 3, 0))


def _bmm3(a, b):
    ah, al = _split2(a)
    bh, bl = _split2(b)
    return _bmm(ah, bh) + (_bmm(ah, bl) + _bmm(al, bh))


def _bmm2(a, b):
    ah, al = _split2(a)
    bb = b.astype(BF16)
    return _bmm(ah, bb) + _bmm(al, bb)


def _conv_rows(x, prev, w, g, lg):
    f = x.shape[1]
    xx = jnp.concatenate([prev.reshape(g, SUBLANE, f), x.reshape(g, lg, f)], axis=1)
    y = xx[:, SUBLANE:, :] * w[CONV_W - 1:CONV_W, :]
    for s in range(1, CONV_W):
        y = y + xx[:, SUBLANE - s:SUBLANE - s + lg, :] * w[CONV_W - 1 - s:CONV_W - s, :]
    return y.reshape(g * lg, f), xx[:, lg:, :].reshape(g * SUBLANE, f)


def _mlstm_body(r, g, q_ref, k_ref, v_ref, o_ref, gt_ref, gb_ref, nw_ref, c0_ref, n0_ref, m0_ref,
                y_ref, c_out, n_out, m_out, c_s, n_s, m_s):
    lg = CHUNK // g
    lg_shift = int(math.log2(lg))
    nch = r // CHUNK
    h = pl.program_id(1)
    t = pl.program_id(2)

    @pl.when(t == 0)
    def _():
        c_s[...] = c0_ref[:, 0]
        n_s[...] = n0_ref[:, 0]
        m_s[...] = jnp.broadcast_to(m0_ref[:, 0], m_s.shape)

    lane3 = lax.broadcasted_iota(jnp.int32, (nch, CHUNK, LANE), 2)
    causal, _ = _seq_masks(g, lg_shift)
    tri_b = jnp.broadcast_to(causal.astype(BF16), (nch, CHUNK, CHUNK))

    pre = gt_ref[...].reshape(nch, CHUNK, LANE) + gb_ref[...]
    capd = ML_GATE_CAP * jnp.tanh(pre * (1.0 / ML_GATE_CAP))
    logsig = jnp.minimum(capd, 0.0) - jnp.log1p(jnp.exp(-jnp.abs(capd)))
    ic = jnp.sum(jnp.where(lane3 == G_MLI + h, capd, 0.0), axis=2, keepdims=True)
    lf = jnp.sum(jnp.where(lane3 == G_MLF + h, logsig, 0.0), axis=2, keepdims=True)
    bc = _cumsum_rows(lf, tri_b, lane3)
    dm = jnp.where(causal, _outer_sum(bc, ic - bc, lane3), -jnp.inf)
    rmax = jnp.max(dm, axis=2, keepdims=True)

    if g == 1:
        mp = m_s[0][:, 0:1]
        m_prevs = []
        for c in range(nch):
            m_prevs.append(mp.reshape(1, 1, 1))
            mp = jnp.maximum(mp + bc[c, CHUNK - 1:, :], rmax[c, CHUNK - 1:, :])
        m_prev = jnp.concatenate(m_prevs, axis=0) if nch > 1 else m_prevs[0]
        m_s[0] = jnp.broadcast_to(mp, (1, LANE))
    else:
        m_prev = _per_seq_rows([m_s[i][:, 0:1] for i in range(g)], g, lg)[None]
    m_t = jnp.maximum(m_prev + bc, rmax)
    dmat = jnp.exp(dm - m_t)
    q3 = q_ref[...].reshape(nch, CHUNK, ML_DK) * (ML_DK ** -0.5)
    k3 = k_ref[...].reshape(nch, CHUNK, ML_DK)
    v3 = v_ref[...].reshape(nch, CHUNK, ML_DV)
    s = _bmm_nt(q3, k3) * dmat
    inter = jnp.exp(bc + m_prev - m_t)
    bc_end = _last_rows(bc, g, lg)
    m_new = _last_rows(m_t, g, lg)
    w_col = jnp.exp(ic - bc + bc_end - m_new)
    dec = jnp.exp(bc_end + m_prev - m_new)
    sv = _bmm(s, v3)
    ssum = jnp.sum(s, axis=2, keepdims=True)
    vw = v3 * w_col
    kw = k3 * w_col
    floor = jnp.exp(-m_t)

    def emit(rows, hh):
        hn = hh * lax.rsqrt(jnp.mean(hh * hh, axis=1, keepdims=True) + RMS_EPS) * nw_ref[...]
        y_ref[rows, :] = (_sigmoid(o_ref[rows, :]) * hn).astype(y_ref.dtype)

    if g == 1:
        upd = _bmm_tn(vw, k3)
        ksum = jnp.sum(kw, axis=1, keepdims=True)
        cc, nn = c_s[0], n_s[0]
        for c in range(nch):
            num = sv[c] + inter[c] * _dot_nt(q3[c], cc)
            den = ssum[c] + inter[c] * jnp.sum(q3[c] * nn, axis=1, keepdims=True)
            emit(slice(c * CHUNK, (c + 1) * CHUNK), num / jnp.maximum(jnp.abs(den), floor[c]))
            cc = dec[c] * cc + upd[c]
            nn = dec[c] * nn + ksum[c]
        c_s[0] = cc
        n_s[0] = nn
    else:
        rows_i = lax.broadcasted_iota(jnp.int32, (CHUNK, 1), 0)
        q2, k2 = q3[0], k3[0]
        qc = jnp.concatenate([_dot_nt(q2[i * lg:(i + 1) * lg], c_s[i]) for i in range(g)], axis=0)
        qn = jnp.concatenate([jnp.sum(q2[i * lg:(i + 1) * lg] * n_s[i], axis=1, keepdims=True)
                              for i in range(g)], axis=0)
        num = sv[0] + inter[0] * qc
        den = ssum[0] + inter[0] * qn
        emit(slice(0, CHUNK), num / jnp.maximum(jnp.abs(den), floor[0]))
        for i in range(g):
            dec_i = dec[0, i * lg:i * lg + 1, :]
            c_s[i] = dec_i * c_s[i] + _dot_tn(jnp.where((rows_i >> lg_shift) == i, vw[0], 0.0), k2)
            n_s[i] = dec_i * n_s[i] + jnp.sum(kw[0, i * lg:(i + 1) * lg], axis=0, keepdims=True)
            m_s[i] = jnp.broadcast_to(m_new[0, i * lg:i * lg + 1, :], (1, LANE))

    @pl.when(t == pl.num_programs(2) - 1)
    def _():
        c_out[:, 0] = c_s[...]
        n_out[:, 0] = n_s[...]
        m_out[:, 0] = m_s[:, :, 0:1]


def _gdn_body(r, g, q_ref, k_ref, v_ref, z_ref, gt_ref, cwq_ref, cwk_ref, cwv_ref, al_ref, dtb_ref,
              nw_ref, bq_ref, bk_ref, bv_ref, s0_ref, y_ref, s_out, tq_out, tk_out, tv_out,
              s_s, pq_s, pk_s, pv_s):
    lg = CHUNK // g
    lg_shift = int(math.log2(lg))
    lseq = r if g == 1 else lg
    nch = r // CHUNK
    h = pl.program_id(1)
    t = pl.program_id(2)

    @pl.when(t == 0)
    def _():
        s_s[...] = s0_ref[:, 0]
        pq_s[...] = bq_ref[...]
        pk_s[...] = bk_ref[...]
        pv_s[...] = bv_ref[...]

    qa, tq = _conv_rows(q_ref[...], pq_s[...], cwq_ref[...], g, lseq)
    ka, tk = _conv_rows(k_ref[...], pk_s[...], cwk_ref[...], g, lseq)
    va, tv = _conv_rows(v_ref[...], pv_s[...], cwv_ref[...], g, lseq)
    pq_s[...] = tq
    pk_s[...] = tk
    pv_s[...] = tv
    qa, ka, va = _silu(qa), _silu(ka), _silu(va)
    qn = qa * lax.rsqrt(jnp.sum(qa * qa, axis=1, keepdims=True) + L2_EPS) * (GDN_DK ** -0.5)
    kn = ka * lax.rsqrt(jnp.sum(ka * ka, axis=1, keepdims=True) + L2_EPS)

    lane3 = lax.broadcasted_iota(jnp.int32, (nch, CHUNK, LANE), 2)
    causal, strict = _seq_masks(g, lg_shift)
    tri_b = jnp.broadcast_to(causal.astype(BF16), (nch, CHUNK, CHUNK))
    eye = (lax.broadcasted_iota(jnp.int32, (CHUNK, CHUNK), 0)
           == lax.broadcasted_iota(jnp.int32, (CHUNK, CHUNK), 1)).astype(F32)

    q3 = qn.reshape(nch, CHUNK, GDN_DK)
    k3 = kn.reshape(nch, CHUNK, GDN_DK)
    v3 = va.reshape(nch, CHUNK, GDN_DV)
    gt = gt_ref[...].reshape(nch, CHUNK, LANE)
    beta = jnp.sum(jnp.where(lane3 == G_GDB + h, _sigmoid(gt), 0.0), axis=2, keepdims=True)
    gval = -jnp.exp(al_ref[...]) * _softplus(gt + dtb_ref[...])
    gc = _cumsum_rows(jnp.sum(jnp.where(lane3 == G_GDA + h, gval, 0.0), axis=2, keepdims=True),
                      tri_b, lane3)
    decay = jnp.exp(jnp.where(causal, _outer_sum(gc, -gc, lane3), -jnp.inf))
    p = jnp.where(strict, -(beta * _bmm_nt(k3, k3) * decay), 0.0)
    inv = eye + p
    ph, pl_ = _split2(p)
    for _ in range(1, lg_shift):
        ph, pl_ = _split2(_bmm(ph, ph) + (_bmm(ph, pl_) + _bmm(pl_, ph)))
        ih, il = _split2(inv)
        inv = inv + (_bmm(ih, ph) + (_bmm(ih, pl_) + _bmm(il, ph)))
    egc = jnp.exp(gc)
    u_pre = _bmm2(inv, beta * v3)
    w = _bmm2(inv, (beta * egc) * k3)
    qk = _bmm_nt(q3, k3) * decay
    q_dec = q3 * egc
    g_end = _last_rows(gc, g, lg)
    k_dec = k3 * jnp.exp(g_end - gc)

    def emit(rows, o):
        hn = o * lax.rsqrt(jnp.mean(o * o, axis=1, keepdims=True) + RMS_EPS) * nw_ref[...]
        y_ref[rows, :] = (hn * _silu(z_ref[rows, :])).astype(y_ref.dtype)

    if g == 1:
        st = s_s[0]
        for c in range(nch):
            u = u_pre[c] - _dot(w[c], st)
            emit(slice(c * CHUNK, (c + 1) * CHUNK), _dot(q_dec[c], st) + _dot(qk[c], u))
            st = jnp.exp(g_end[c]) * st + _dot_tn(k_dec[c], u)
        s_s[0] = st
    else:
        rows_i = lax.broadcasted_iota(jnp.int32, (CHUNK, 1), 0)
        us, os_ = [], []
        for i in range(g):
            sl = slice(i * lg, (i + 1) * lg)
            us.append(u_pre[0, sl] - _dot(w[0, sl], s_s[i]))
            os_.append(_dot(q_dec[0, sl], s_s[i]))
        u = jnp.concatenate(us, axis=0)
        emit(slice(0, CHUNK), jnp.concatenate(os_, axis=0) + _dot(qk[0], u))
        for i in range(g):
            kdi = jnp.where((rows_i >> lg_shift) == i, k_dec[0], 0.0)
            s_s[i] = jnp.exp(g_end[0, i * lg:i * lg + 1, :]) * s_s[i] + _dot_tn(kdi, u)

    @pl.when(t == pl.num_programs(2) - 1)
    def _():
        s_out[:, 0] = s_s[...]
        tq_out[...] = tq
        tk_out[...] = tk
        tv_out[...] = tv


def _rglru_body(r, g, x_ref, yg_ref, cw_ref, cb_ref, wa_ref, wx_ref, ba_ref, bx_ref, lam_ref,
                buf_ref, h0_ref, y_ref, h_out, tail_out, h_s, px_s):
    lseq = r if g == 1 else CHUNK // g
    t = pl.program_id(2)

    @pl.when(t == 0)
    def _():
        h_s[...] = h0_ref[:, 0]
        px_s[...] = buf_ref[...]

    xc, tail = _conv_rows(x_ref[...], px_s[...], cw_ref[...], g, lseq)
    px_s[...] = tail
    xc = xc + cb_ref[...]
    rg = _sigmoid(_dot(xc, wa_ref[0]) + ba_ref[...])
    ig = _sigmoid(_dot(xc, wx_ref[0]) + bx_ref[...])
    log_a = -LRU_C * rg * _softplus(-lam_ref[...])
    a = jnp.exp(log_a)
    u = jnp.sqrt(1.0 - jnp.exp(2.0 * log_a)) * (ig * xc)
    pos = lax.broadcasted_iota(jnp.int32, (r, 1), 0) & (SUBLANE - 1)
    sh = 1
    while sh < SUBLANE:
        a_sh = pltpu.roll(a, sh, 0)
        u_sh = pltpu.roll(u, sh, 0)
        msk = pos >= sh
        u = jnp.where(msk, a * u_sh + u, u)
        a = jnp.where(msk, a * a_sh, a)
        sh *= 2
    blocks = []
    for i in range(g):
        carry = h_s[i]
        for k in range(lseq // SUBLANE):
            rows = slice(i * lseq + k * SUBLANE, i * lseq + (k + 1) * SUBLANE)
            blk = u[rows] + a[rows] * carry
            blocks.append(blk)
            carry = blk[SUBLANE - 1:, :]
        h_s[i] = carry
    hh = jnp.concatenate(blocks, axis=0)
    yg = yg_ref[...]
    gelu = 0.5 * yg * (1.0 + jnp.tanh(math.sqrt(2.0 / math.pi) * (yg + 0.044715 * (yg * yg * yg))))
    y_ref[...] = (gelu * hh).astype(y_ref.dtype)

    @pl.when(t == pl.num_programs(2) - 1)
    def _():
        h_out[:, 0] = h_s[...]
        tail_out[...] = tail


class _Group:
    def __init__(self, n_seq, seq_len, row0):
        self.n_seq, self.seq_len, self.row0 = n_seq, seq_len, row0
        self.rows = n_seq * seq_len
        if seq_len >= 4 * CHUNK:
            self.r, self.g = 4 * CHUNK, 1
        elif seq_len >= CHUNK:
            self.r, self.g = CHUNK, 1
        else:
            self.r, self.g = CHUNK, CHUNK // seq_len
        self.s = self.g
        self.nb = n_seq // self.s
        self.nt = seq_len // self.r if self.g == 1 else 1
        assert seq_len >= CONV_W - 1 and row0 % self.r == 0 and self.rows % self.r == 0
        assert n_seq % self.s == 0 and self.r & (self.r - 1) == 0
        self.rb0 = row0 // self.r

    def rb(self, b, t):
        return self.rb0 + b * self.nt + t


def _mixer_call(body, grp, l, depth, m_total, in_specs, args, y_width, y_block, state_shapes, extra_outs,
                scratch, prev, name):
    s = grp.s
    out_shape = [jax.ShapeDtypeStruct((m_total, y_width), BF16)]
    out_specs = [pl.BlockSpec((grp.r, y_block), lambda b, h, t: (grp.rb(b, t), h))]
    for tail in state_shapes:
        out_shape.append(jax.ShapeDtypeStruct((depth, grp.n_seq, y_width // y_block) + tail, F32))
        out_specs.append(pl.BlockSpec((None, s, 1) + tail, lambda b, h, t: (l, b, h) + (0,) * len(tail)))
    for shp, spec in extra_outs:
        out_shape.append(shp)
        out_specs.append(spec)
    n_in = len(args)
    aliases = {}
    in_specs = list(in_specs)
    args = list(args)
    for j, arr in enumerate(prev):
        if arr is not None:
            aliases[len(args)] = j
            in_specs.append(pl.BlockSpec(memory_space=pl.ANY))
            args.append(arr)
    n_alias = len(args) - n_in

    def wrapped(*refs):
        body(*refs[:n_in], *refs[n_in + n_alias:])

    return pl.pallas_call(
        wrapped,
        out_shape=out_shape,
        grid=(grp.nb, y_width // y_block, grp.nt),
        in_specs=in_specs,
        out_specs=out_specs,
        scratch_shapes=scratch,
        input_output_aliases=aliases,
        compiler_params=_cparams(("arbitrary",) * 3),
        name=name,
    )(*args)


def _mlstm_call(grp, l, depth, m_total, hmain, gates, gate_bias, norm_w, c0, n0, m0, prev):
    r, s = grp.r, grp.s
    row = lambda off, w: pl.BlockSpec((r, w), lambda b, h, t: (grp.rb(b, t), off // w + h))
    st = lambda *tail: pl.BlockSpec((None, s, 1) + tail, lambda b, h, t: (l, b, h, 0, 0))
    in_specs = [row(C_MLQ, ML_DK), row(C_MLK, ML_DK), row(C_MLV, ML_DV), row(C_MLO, ML_DV),
                pl.BlockSpec((r, LANE), lambda b, h, t: (grp.rb(b, t), 0)),
                pl.BlockSpec((1, LANE), lambda b, h, t: (0, 0)),
                pl.BlockSpec((1, ML_DV), lambda b, h, t: (0, h)),
                st(ML_DV, ML_DK), st(1, ML_DK), st(1, 1)]
    scratch = [pltpu.VMEM((s, ML_DV, ML_DK), F32), pltpu.VMEM((s, 1, ML_DK), F32),
               pltpu.VMEM((s, 1, LANE), F32)]
    return _mixer_call(functools.partial(_mlstm_body, r, grp.g), grp, l, depth, m_total, in_specs,
                       [hmain, hmain, hmain, hmain, gates, gate_bias, norm_w, c0, n0, m0],
                       ML_HEADS * ML_DV, ML_DV, [(ML_DV, ML_DK), (1, ML_DK), (1, 1)], [], scratch, prev, "mlstm")


def _gdn_call(grp, l, depth, m_total, hmain, gates, conv_w, a_log_row, dt_bias_row, norm_w, bufpad, s0, prev):
    r, s = grp.r, grp.s
    row = lambda off, w: pl.BlockSpec((r, w), lambda b, h, t: (grp.rb(b, t), off // w + h))
    cw = lambda off, w: pl.BlockSpec((CONV_W, w), lambda b, h, t: (0, off // w + h))
    bf = lambda off, w: pl.BlockSpec((s * SUBLANE, w), lambda b, h, t: (b, off // w + h))
    one = pl.BlockSpec((1, LANE), lambda b, h, t: (0, 0))
    in_specs = [row(C_GDQ, GDN_DK), row(C_GDK, GDN_DK), row(C_GDV, GDN_DV), row(C_GDZ, GDN_DV),
                pl.BlockSpec((r, LANE), lambda b, h, t: (grp.rb(b, t), 1)),
                cw(0, GDN_DK), cw(GDN_QK, GDN_DK), cw(2 * GDN_QK, GDN_DV),
                one, one, pl.BlockSpec((1, GDN_DV), lambda b, h, t: (0, 0)),
                bf(0, GDN_DK), bf(GDN_QK, GDN_DK), bf(2 * GDN_QK, GDN_DV),
                pl.BlockSpec((None, s, 1, GDN_DK, GDN_DV), lambda b, h, t: (l, b, h, 0, 0))]
    nrow = grp.n_seq * SUBLANE
    tail = lambda w: (jax.ShapeDtypeStruct((nrow, GDN_HEADS * w), F32),
                      pl.BlockSpec((s * SUBLANE, w), lambda b, h, t: (b, h)))
    scratch = [pltpu.VMEM((s, GDN_DK, GDN_DV), F32),
               pltpu.VMEM((s * SUBLANE, GDN_DK), F32), pltpu.VMEM((s * SUBLANE, GDN_DK), F32),
               pltpu.VMEM((s * SUBLANE, GDN_DV), F32)]
    return _mixer_call(functools.partial(_gdn_body, r, grp.g), grp, l, depth, m_total, in_specs,
                       [hmain, hmain, hmain, hmain, gates, conv_w, conv_w, conv_w, a_log_row, dt_bias_row,
                        norm_w, bufpad, bufpad, bufpad, s0],
                       GDN_HEADS * GDN_DV, GDN_DV, [(GDN_DK, GDN_DV)],
                       [tail(GDN_DK), tail(GDN_DK), tail(GDN_DV)], scratch, prev, "gdn")


def _rglru_call(grp, l, depth, m_total, hmain, conv_w, conv_b, w_a, w_x, b_a, b_x, lam, bufpad, h0, prev):
    r, s = grp.r, grp.s
    w = LRU_BLOCK
    row = lambda off: pl.BlockSpec((r, w), lambda b, h, t: (grp.rb(b, t), off // w + h))
    vec = pl.BlockSpec((1, w), lambda b, h, t: (0, h))
    blk = pl.BlockSpec((1, w, w), lambda b, h, t: (h, 0, 0))
    in_specs = [row(C_RGX), row(C_RGY),
                pl.BlockSpec((CONV_W, w), lambda b, h, t: (0, h)), vec, blk, blk, vec, vec, vec,
                pl.BlockSpec((s * SUBLANE, w), lambda b, h, t: (b, h)),
                pl.BlockSpec((None, s, 1, 1, w), lambda b, h, t: (l, b, h, 0, 0))]
    tail = (jax.ShapeDtypeStruct((grp.n_seq * SUBLANE, LRU_WIDTH), F32),
            pl.BlockSpec((s * SUBLANE, w), lambda b, h, t: (b, h)))
    scratch = [pltpu.VMEM((s, 1, w), F32), pltpu.VMEM((s * SUBLANE, w), F32)]
    return _mixer_call(functools.partial(_rglru_body, r, grp.g), grp, l, depth, m_total, in_specs,
                       [hmain, hmain, conv_w, conv_b, w_a, w_x, b_a, b_x, lam, bufpad, h0],
                       LRU_WIDTH, w, [(1, w)], [tail], scratch, prev, "rglru")


def _pad_conv_state(buf):
    b, k, c = buf.shape
    return jnp.pad(buf.astype(F32), ((0, 0), (SUBLANE - k, 0), (0, 0))).reshape(b * SUBLANE, c)


def _unpad_conv_state(tail, n_seq):
    return tail.reshape(n_seq, SUBLANE, -1)[:, SUBLANE - (CONV_W - 1):, :]


R_I1, R_I2, R_W1, R_W2, R_P1, R_P2 = 0, 1, 2, 3, 4, 5
MOE_TM = 512
MOE_TD = 512


def _router_call(x, router, tm):
    m, d = x.shape
    n_exp = router.shape[1]
    wr = jnp.pad(router.astype(F32), ((0, 0), (0, LANE - n_exp)))

    def body(x_ref, w_ref, o_ref, cnt_ref, carry):
        i = pl.program_id(0)

        @pl.when(i == 0)
        def _():
            carry[...] = jnp.zeros_like(carry)

        xh, xl = _split2(x_ref[...])
        wh, wl = _split2(w_ref[...])
        logits = _dot(xh, wh) + (_dot(xh, wl) + _dot(xl, wh))
        lane = lax.broadcasted_iota(jnp.int32, logits.shape, 1)
        lg = jnp.where(lane < n_exp, logits, -jnp.inf)
        v1 = jnp.max(lg, axis=1, keepdims=True)
        i1 = jnp.min(jnp.where(lg == v1, lane, LANE), axis=1, keepdims=True)
        lg2 = jnp.where(lane == i1, -jnp.inf, lg)
        v2 = jnp.max(lg2, axis=1, keepdims=True)
        i2 = jnp.min(jnp.where(lg2 == v2, lane, LANE), axis=1, keepdims=True)
        e2 = jnp.exp(v2 - v1)
        w1 = 1.0 / (1.0 + e2)
        w2 = e2 / (1.0 + e2)
        sel = ((lane == i1) | (lane == i2)).astype(F32)
        below = (lax.broadcasted_iota(jnp.int32, (tm, tm), 1)
                 < lax.broadcasted_iota(jnp.int32, (tm, tm), 0)).astype(BF16)
        rank = jnp.dot(below, sel.astype(BF16), preferred_element_type=F32) + carry[0:1, :]
        p1 = jnp.sum(jnp.where(lane == i1, rank, 0.0), axis=1, keepdims=True)
        p2 = jnp.sum(jnp.where(lane == i2, rank, 0.0), axis=1, keepdims=True)
        rec = jnp.where(lane == R_I1, i1.astype(F32), jnp.where(lane == R_I2, i2.astype(F32), 0.0))
        rec = jnp.where(lane == R_W1, w1, jnp.where(lane == R_W2, w2, rec))
        o_ref[...] = jnp.where(lane == R_P1, p1, jnp.where(lane == R_P2, p2, rec))
        carry[...] = carry[...] + jnp.sum(sel, axis=0, keepdims=True)
        cnt_ref[...] = carry[...]

    return pl.pallas_call(
        body,
        out_shape=[jax.ShapeDtypeStruct((m, LANE), F32), jax.ShapeDtypeStruct((SUBLANE, LANE), F32)],
        grid=(m // tm,),
        in_specs=[pl.BlockSpec((tm, d), lambda i: (i, 0)), pl.BlockSpec((d, LANE), lambda i: (0, 0))],
        out_specs=[pl.BlockSpec((tm, LANE), lambda i: (i, 0)), pl.BlockSpec((SUBLANE, LANE), lambda i: (0, 0))],
        scratch_shapes=[pltpu.VMEM((SUBLANE, LANE), F32)],
        compiler_params=_cparams(("arbitrary",)),
        name="router",
    )(x, wr)


def _source_table(starts, e_arr, p_arr, m, ns):
    td = MOE_TD

    def body(start_ref, e_ref, p_ref, tbl_ref):
        i = pl.program_id(0)

        @pl.when(i == 0)
        def _():
            def clear(r, c):
                tbl_ref[r] = 0
                return c

            lax.fori_loop(0, ns, clear, 0, unroll=8)

        base = i * td

        def put(j, c):
            tbl_ref[start_ref[e_ref[0, j]] + p_ref[0, j]] = base + (j >> 1)
            return c

        lax.fori_loop(0, 2 * td, put, 0, unroll=8)

    smem = pl.BlockSpec((None, 1, 2 * td), lambda i, s: (i, 0, 0), memory_space=pltpu.SMEM)
    return pl.pallas_call(
        body,
        grid_spec=pltpu.PrefetchScalarGridSpec(
            num_scalar_prefetch=1, grid=(m // td,), in_specs=[smem, smem],
            out_specs=pl.BlockSpec(memory_space=pltpu.SMEM)),
        out_shape=jax.ShapeDtypeStruct((ns,), jnp.int32),
        compiler_params=_cparams(("arbitrary",)),
        name="moe_table",
    )(starts, e_arr, p_arr)


def _gather_rows(copy_of, n_rows):
    def issue(r, c):
        copy_of(r).start()
        return c

    def drain(r, c):
        copy_of(r).wait()
        return c

    lax.fori_loop(0, n_rows, issue, 0, unroll=8)
    lax.fori_loop(0, n_rows, drain, 0, unroll=8)


def _dispatch(x, table, n_used):
    m, d = x.shape
    ns = table.shape[0]
    tm = MOE_TM

    def body(nu_ref, src_ref, x_ref, o_ref, buf, sem):
        @pl.when(pl.program_id(0) < nu_ref[0])
        def _():
            _gather_rows(lambda r: pltpu.make_async_copy(x_ref.at[pl.ds(src_ref[0, r], 1)],
                                                        buf.at[pl.ds(r, 1)], sem), tm)
            o_ref[...] = buf[...].astype(BF16)

    return pl.pallas_call(
        body,
        grid_spec=pltpu.PrefetchScalarGridSpec(
            num_scalar_prefetch=1, grid=(ns // tm,),
            in_specs=[pl.BlockSpec((None, 1, tm), lambda i, nu: (i, 0, 0), memory_space=pltpu.SMEM),
                      pl.BlockSpec(memory_space=pl.ANY)],
            out_specs=pl.BlockSpec((tm, d), lambda i, nu: (jnp.minimum(i, nu[0] - 1), 0)),
            scratch_shapes=[pltpu.VMEM((tm, d), F32), pltpu.SemaphoreType.DMA(())]),
        out_shape=jax.ShapeDtypeStruct((ns, d), BF16),
        compiler_params=_cparams(("arbitrary",)),
        name="moe_dispatch",
    )(n_used, table.reshape(ns // tm, 1, tm), x)


def _grouped(rows, ws, tile_expert, tile_first, n_used, epilogue, out_dtype, n_cols, tn, name, w_buffers=2):
    ns, k = rows.shape
    nt = ns // MOE_TM
    used = lambda i, nu: jnp.minimum(i, nu[0] - 1)
    w_spec = pl.BlockSpec((None, k, tn), lambda n, i, te, tf, nu: (te[i], 0, n),
                          pipeline_mode=pl.Buffered(w_buffers))

    def body(te_ref, tf_ref, nu_ref, x_ref, *refs):
        w_refs, o_ref, wb_refs = refs[:len(ws)], refs[len(ws)], refs[len(ws) + 1:]
        i = pl.program_id(1)

        @pl.when(tf_ref[i] == 1)
        def _():
            for w, wb in zip(w_refs, wb_refs):
                wb[...] = w[...].astype(BF16)

        @pl.when(i < nu_ref[0])
        def _():
            xv = x_ref[...].astype(BF16)
            accs = [jnp.dot(xv, wb[...], preferred_element_type=F32) for wb in wb_refs]
            o_ref[...] = epilogue(accs).astype(o_ref.dtype)

    return pl.pallas_call(
        body,
        grid_spec=pltpu.PrefetchScalarGridSpec(
            num_scalar_prefetch=3, grid=(n_cols // tn, nt),
            in_specs=[pl.BlockSpec((MOE_TM, k), lambda n, i, te, tf, nu: (used(i, nu), 0))] + [w_spec] * len(ws),
            out_specs=pl.BlockSpec((MOE_TM, tn), lambda n, i, te, tf, nu: (used(i, nu), n)),
            scratch_shapes=[pltpu.VMEM((k, tn), BF16) for _ in ws]),
        out_shape=jax.ShapeDtypeStruct((ns, n_cols), out_dtype),
        compiler_params=_cparams(("arbitrary", "arbitrary")),
        name=name,
    )(tile_expert, tile_first, n_used, rows, *ws)


def _combine_ln(x, ys, starts, e_arr, p_arr, rec, ple, gamma, beta, split_rows):
    tm = MOE_TD
    m, n = x.shape
    tiles_a = None if split_rows is None else split_rows // tm

    def body(start_ref, es_ref, ps_ref, ys_ref, x_ref, r_ref, e_ref, g_ref, b_ref, oa_ref, ob_ref, ybuf, sem):
        i = pl.program_id(0)
        _gather_rows(lambda j: pltpu.make_async_copy(
            ys_ref.at[pl.ds(start_ref[es_ref[0, j]] + ps_ref[0, j], 1)],
            ybuf.at[j & 1, pl.ds(j >> 1, 1)], sem), 2 * tm)
        r = r_ref[...]
        lane = lax.broadcasted_iota(jnp.int32, r.shape, 1)
        w1 = jnp.sum(jnp.where(lane == R_W1, r, 0.0), axis=1, keepdims=True)
        w2 = jnp.sum(jnp.where(lane == R_W2, r, 0.0), axis=1, keepdims=True)
        y = DEEPNORM_ALPHA * x_ref[...] + (w1 * ybuf[0] + w2 * ybuf[1]) + e_ref[...]
        mu = jnp.mean(y, axis=-1, keepdims=True)
        yc = y - mu
        var = jnp.mean(yc * yc, axis=-1, keepdims=True)
        out = yc * lax.rsqrt(var + LN_EPS) * g_ref[...] + b_ref[...]
        if split_rows is None:
            oa_ref[...] = out
            ob_ref[...] = out.astype(BF16)
        else:
            @pl.when(i < tiles_a)
            def _():
                oa_ref[...] = out

            @pl.when(i >= tiles_a)
            def _():
                ob_ref[...] = out

    row = lambda i, s: (i, 0)
    vec = pl.BlockSpec((1, n), lambda i, s: (0, 0))
    smem = pl.BlockSpec((None, 1, 2 * tm), lambda i, s: (i, 0, 0), memory_space=pltpu.SMEM)
    if split_rows is None:
        out_shape = [jax.ShapeDtypeStruct((m, n), F32), jax.ShapeDtypeStruct((m, n), BF16)]
        out_specs = [pl.BlockSpec((tm, n), row), pl.BlockSpec((tm, n), row)]
    else:
        out_shape = [jax.ShapeDtypeStruct((split_rows, n), F32), jax.ShapeDtypeStruct((m - split_rows, n), F32)]
        out_specs = [pl.BlockSpec((tm, n), lambda i, s: (jnp.minimum(i, tiles_a - 1), 0)),
                     pl.BlockSpec((tm, n), lambda i, s: (jnp.maximum(i - tiles_a, 0), 0))]
    return pl.pallas_call(
        body,
        grid_spec=pltpu.PrefetchScalarGridSpec(
            num_scalar_prefetch=1, grid=(m // tm,),
            in_specs=[smem, smem, pl.BlockSpec(memory_space=pl.ANY), pl.BlockSpec((tm, n), row),
                      pl.BlockSpec((tm, LANE), row), pl.BlockSpec((tm, n), row), vec, vec],
            out_specs=out_specs,
            scratch_shapes=[pltpu.VMEM((2, tm, n), F32), pltpu.SemaphoreType.DMA(())]),
        out_shape=out_shape,
        compiler_params=_cparams(("arbitrary",)),
        name="moe_combine_ln",
    )(starts, e_arr, p_arr, ys, x, rec, ple, gamma.reshape(1, n), beta.reshape(1, n))


def _moe_ln(x, ple, router, w1, w3, w2, j, gamma, beta, split_rows):
    m, d = x.shape
    _, ne, _, fe = w1.shape
    w1, w3, w2 = (w.reshape((-1,) + w.shape[2:]) for w in (w1, w3, w2))
    rec, counts = _router_call(x, router, 512)
    cnt = counts[0, :ne].astype(jnp.int32)
    tiles_e = (cnt + MOE_TM - 1) // MOE_TM
    tile_end = jnp.cumsum(tiles_e)
    tile_beg = tile_end - tiles_e
    n_used = tile_end[ne - 1]
    nt = (2 * m) // MOE_TM + ne
    ids = jnp.minimum(jnp.arange(nt, dtype=jnp.int32), n_used - 1)
    tile_expert = jnp.sum((ids[:, None] >= tile_end[None, :]).astype(jnp.int32), axis=1)
    tile_first = (jnp.arange(nt, dtype=jnp.int32) == tile_beg[tile_expert]).astype(jnp.int32)
    starts = (tile_beg * MOE_TM).astype(jnp.int32)
    slots = lambda a, b: jnp.stack([rec[:, a], rec[:, b]], axis=1).astype(jnp.int32).reshape(m // MOE_TD, 1, 2 * MOE_TD)
    e_arr, p_arr = slots(R_I1, R_I2), slots(R_P1, R_P2)
    ns = nt * MOE_TM

    nu = n_used.reshape(1)
    xs = _dispatch(x, _source_table(starts, e_arr, p_arr, m, ns), nu)
    tile_w = tile_expert + j * ne
    hmid = _grouped(xs, [w1, w3], tile_w, tile_first, nu, lambda a: _silu(a[0]) * a[1], BF16, fe, fe // 2,
                    "moe_up", w_buffers=1)
    ys = _grouped(hmid, [w2], tile_w, tile_first, nu, lambda a: a[0], F32, d, 1024, "moe_down")
    return _combine_ln(x, ys, starts, e_arr, p_arr, rec, ple, gamma, beta, split_rows)


def _run_group_mixers(grp, l, depth, m_total, hmain, gates, st, prm, prev):
    b = grp.n_seq
    ya, ml_c, ml_n, ml_m = _mlstm_call(
        grp, l, depth, m_total, hmain, gates, prm["gate_bias"], prm["ml_norm_w"],
        st["ml_C"].astype(F32), st["ml_n"].astype(F32).reshape(depth, b, ML_HEADS, 1, ML_DK),
        st["ml_m"].astype(F32).reshape(depth, b, ML_HEADS, 1, 1),
        [prev["ya"], prev["ml_C"], prev["ml_n"], prev["ml_m"]])
    yb, rg_h, rg_tail = _rglru_call(
        grp, l, depth, m_total, hmain, prm["rg_conv_w"], prm["rg_conv_b"], prm["rg_w_a"], prm["rg_w_x"],
        prm["rg_b_a"], prm["rg_b_x"], prm["rg_lambda"], _pad_conv_state(st["rg_conv"][l]),
        st["rg_h"].astype(F32).reshape(depth, b, LRU_BLOCKS, 1, LRU_BLOCK), [prev["yb"], prev["rg_h"]])
    yc, gd_s, tq, tk, tv = _gdn_call(
        grp, l, depth, m_total, hmain, gates, prm["gd_conv_w"], prm["gd_a_log"], prm["gd_dt_bias"],
        prm["gd_norm_w"], _pad_conv_state(st["gd_conv"][l]), st["gd_S"].astype(F32),
        [prev["yc"], prev["gd_S"]])
    out = dict(ya=ya, yb=yb, yc=yc, ml_C=ml_c, ml_n=ml_n, ml_m=ml_m, rg_h=rg_h, gd_S=gd_s)
    tails = dict(rg_conv=_unpad_conv_state(rg_tail, b),
                 gd_conv=jnp.concatenate([_unpad_conv_state(x, b) for x in (tq, tk, tv)], axis=-1))
    return out, tails


_STATE_KEYS = ("ml_C", "ml_n", "ml_m", "rg_h", "gd_S")


def kernel(x_prompt, x_sample, state_mlstm_C, state_mlstm_n, state_mlstm_m, state_rglru_h, state_rglru_conv, state_gdn_S, state_gdn_conv, p_prompt, p_sample, w_in, ml_b_i, ml_b_f, ml_norm_w, rg_conv_w, rg_conv_b, rg_w_a, rg_b_a, rg_w_x, rg_b_x, rg_lambda, gd_conv_w, gd_A_log, gd_dt_bias, gd_norm_w, w_up_mlstm, w_up_rglru, w_up_gdn, w_out, ln1_g, ln1_b, ffn_w1, ffn_w3, ffn_w2, moe_router, moe_w1, moe_w3, moe_w2, ple_w, ple_gate_w, ln2_g, ln2_b):
    bp, tp, d = x_prompt.shape
    bs, ts, _ = x_sample.shape
    depth = w_in.shape[0]
    gp = _Group(bp, tp, 0)
    gs = _Group(bs, ts, bp * tp)
    m = gp.rows + gs.rows
    pd = x_prompt.dtype

    zeros = lambda *shape: jnp.zeros(shape, F32)
    st_p = dict(ml_C=zeros(depth, bp, ML_HEADS, ML_DV, ML_DK), ml_n=zeros(depth, bp, ML_HEADS, ML_DK),
                ml_m=zeros(depth, bp, ML_HEADS), rg_h=zeros(depth, bp, LRU_WIDTH),
                rg_conv=zeros(depth, bp, CONV_W - 1, LRU_WIDTH),
                gd_S=zeros(depth, bp, GDN_HEADS, GDN_DK, GDN_DV),
                gd_conv=zeros(depth, bp, CONV_W - 1, GDN_CONV_DIM))
    st_s = dict(ml_C=state_mlstm_C, ml_n=state_mlstm_n, ml_m=state_mlstm_m, rg_h=state_rglru_h,
                rg_conv=state_rglru_conv, gd_S=state_gdn_S, gd_conv=state_gdn_conv)

    x = jnp.concatenate([x_prompt.reshape(gp.rows, d), x_sample.reshape(gs.rows, d)], axis=0).astype(F32)
    xb = x.astype(BF16)
    w_in_t = jnp.swapaxes(w_in, 1, 2)
    lane_row = lambda v, off: jnp.zeros((1, LANE), F32).at[0, off:off + v.shape[0]].set(v.astype(F32))
    acc_p = {k: None for k in _STATE_KEYS}
    acc_s = {k: None for k in _STATE_KEYS}
    tails_p, tails_s = [], []

    for l in range(depth):
        hmain = _in_proj(xb, w_in_t, l, 1024)
        gates = _gate_proj(xb, w_in_t, l, 1024)

        prm = dict(
            gate_bias=lane_row(ml_b_i[l], G_MLI) + lane_row(ml_b_f[l], G_MLF),
            ml_norm_w=ml_norm_w[l].astype(F32).reshape(1, -1),
            rg_conv_w=rg_conv_w[l].astype(F32), rg_conv_b=rg_conv_b[l].astype(F32).reshape(1, -1),
            rg_w_a=rg_w_a[l], rg_w_x=rg_w_x[l], rg_b_a=rg_b_a[l].astype(F32).reshape(1, -1),
            rg_b_x=rg_b_x[l].astype(F32).reshape(1, -1), rg_lambda=rg_lambda[l].astype(F32).reshape(1, -1),
            gd_conv_w=gd_conv_w[l].astype(F32), gd_a_log=lane_row(gd_A_log[l], G_GDA),
            gd_dt_bias=lane_row(gd_dt_bias[l], G_GDA), gd_norm_w=gd_norm_w[l].astype(F32).reshape(1, -1))
        out_p, tl_p = _run_group_mixers(gp, l, depth, m, hmain, gates, st_p, prm,
                                        dict(acc_p, ya=None, yb=None, yc=None))
        out_s, tl_s = _run_group_mixers(gs, l, depth, m, hmain, gates, st_s, prm,
                                        dict(acc_s, ya=out_p["ya"], yb=out_p["yb"], yc=out_p["yc"]))
        acc_p = {k: out_p[k] for k in _STATE_KEYS}
        acc_s = {k: out_s[k] for k in _STATE_KEYS}
        tails_p.append(tl_p)
        tails_s.append(tl_s)
        ya, yb, yc = out_s["ya"], out_s["yb"], out_s["yc"]

        tn, tm = 1024, 512
        up_spec = pl.BlockSpec((None, ya.shape[1], tn), lambda n, i: (l, 0, n))
        mg_spec = lambda br: pl.BlockSpec((tm, tn), lambda n, i: (i, (C_MG + br * d) // tn + n))
        merged = _proj([ya, yb, yc], [w_up_mlstm, w_up_rglru, w_up_gdn], [up_spec] * 3,
                       [hmain] * 3, [mg_spec(0), mg_spec(1), mg_spec(2)],
                       lambda accs, ex, ids: [_sigmoid(ex[0]) * accs[0] + _sigmoid(ex[1]) * accs[1]
                                              + _sigmoid(ex[2]) * accs[2]],
                       [jax.ShapeDtypeStruct((m, d), BF16)], d, tm, tn, "merge")[0]
        x, xb = _rowfull_ln(merged, w_out[l].astype(BF16), x, None, ln1_g[l].astype(F32), ln1_b[l].astype(F32),
                            512, d, "out_proj_ln")

        p_l = jnp.concatenate([p_prompt[l].reshape(gp.rows, -1), p_sample[l].reshape(gs.rows, -1)], axis=0)
        ple = _proj([xb, p_l], [ple_gate_w, ple_w],
                    [pl.BlockSpec((None, d, 1024), lambda n, i: (l, 0, n)),
                     pl.BlockSpec((None, P_DIM, 1024), lambda n, i: (l, 0, n))],
                    [], [], lambda accs, ex, ids: [_sigmoid(accs[0]) * accs[1]],
                    [jax.ShapeDtypeStruct((m, d), F32)], d, 1024, 1024, "ple")[0]
        j = l // 2
        split = gp.rows if l == depth - 1 else None
        g2, b2 = ln2_g[l].astype(F32), ln2_b[l].astype(F32)
        if l % 2 == 0:
            ff = ffn_w1.shape[2]
            hmid = _proj([xb], [ffn_w1, ffn_w3], [pl.BlockSpec((None, d, 512), lambda n, i: (j, 0, n))] * 2,
                         [], [], lambda accs, ex, ids: [_silu(accs[0]) * accs[1]],
                         [jax.ShapeDtypeStruct((m, ff), BF16)], ff, 1024, 512, "ffn_up", x_of_w=(0, 0))[0]
            x, xb = _rowfull_ln(hmid, ffn_w2[j].astype(BF16), x, ple, g2, b2, 512, 1408, "ffn_down_ln",
                                split_rows=split)
        else:
            x, xb = _moe_ln(x, ple, moe_router[j], moe_w1, moe_w3, moe_w2, j, g2, b2, split)

    y_prompt = x.reshape(bp, tp, d).astype(pd)
    y_sample = xb.reshape(bs, ts, d).astype(x_sample.dtype)

    def finish(acc, tails, n_seq, dtypes):
        conv = {k: jnp.stack([tl[k] for tl in tails]) for k in ("rg_conv", "gd_conv")}
        vals = dict(ml_C=acc["ml_C"], ml_n=acc["ml_n"].reshape(depth, n_seq, ML_HEADS, ML_DK),
                    ml_m=acc["ml_m"].reshape(depth, n_seq, ML_HEADS),
                    rg_h=acc["rg_h"].reshape(depth, n_seq, LRU_WIDTH), rg_conv=conv["rg_conv"],
                    gd_S=acc["gd_S"], gd_conv=conv["gd_conv"])
        names = ("ml_C", "ml_n", "ml_m", "rg_h", "rg_conv", "gd_S", "gd_conv")
        return tuple(vals[k].astype(dtypes[k]) for k in names)

    return ((y_prompt, y_sample) + finish(acc_p, tails_p, bp, {k: pd for k in st_p})
            + finish(acc_s, tails_s, bs, {k: v.dtype for k, v in st_s.items()}))
```

```python
import functools
import math

import jax
import jax.numpy as jnp
from jax import lax
from jax.experimental import pallas as pl
from jax.experimental.pallas import tpu as pltpu

F32 = jnp.float32
BF16 = jnp.bfloat16

D_MODEL = 2048
DEPTH = 2
P_DIM = 256
CONV_W = 4
N_BRANCH = 3
ML_HEADS = 4
ML_DK = 128
ML_DV = 256
ML_GATE_CAP = 15.0
LRU_WIDTH = 1024
LRU_BLOCKS = 4
LRU_BLOCK = LRU_WIDTH // LRU_BLOCKS
LRU_C = 8.0
GDN_HEADS = 4
GDN_DK = 128
GDN_DV = 256
GDN_QK = GDN_HEADS * GDN_DK
GDN_CONV_DIM = 2 * GDN_QK + GDN_HEADS * GDN_DV
N_EXPERTS = 8
DEEPNORM_ALPHA = (2 * DEPTH) ** 0.25
LN_EPS = 1e-5
RMS_EPS = 1e-6
L2_EPS = 1e-6

LANE = 128
SUBLANE = 8
CHUNK = 128
VMEM_LIMIT = 56 * 1024 * 1024

C_MLQ, C_MLK, C_MLV, C_MLO = 0, 512, 1024, 2048
C_RGX, C_RGY = 3072, 4096
C_GDQ, C_GDK, C_GDV, C_GDZ = 5120, 5632, 6144, 7168
C_MG = 8192
N_MAIN = C_MG + N_BRANCH * D_MODEL
W_GATES_A, W_GATES_B = 2048, 7176
G_MLI, G_MLF, G_GDB, G_GDA = 0, 4, 8, 12
PROJ_TN = 1024


def _cparams(sem):
    return pltpu.CompilerParams(dimension_semantics=sem, vmem_limit_bytes=VMEM_LIMIT)


def _dot(a, b):
    return jnp.dot(a.astype(BF16), b.astype(BF16), preferred_element_type=F32)


def _dot_nt(a, b):
    return lax.dot_general(a.astype(BF16), b.astype(BF16), (((1,), (1,)), ((), ())),
                           preferred_element_type=F32)


def _dot_tn(a, b):
    return lax.dot_general(a.astype(BF16), b.astype(BF16), (((0,), (0,)), ((), ())),
                           preferred_element_type=F32)


def _bmm(a, b):
    return jnp.einsum('cij,cjk->cik', a.astype(BF16), b.astype(BF16), preferred_element_type=F32)


def _bmm_nt(a, b):
    return jnp.einsum('cik,cjk->cij', a.astype(BF16), b.astype(BF16), preferred_element_type=F32)


def _bmm_tn(a, b):
    return jnp.einsum('csi,csj->cij', a.astype(BF16), b.astype(BF16), preferred_element_type=F32)


def _sigmoid(x):
    return 1.0 / (1.0 + jnp.exp(-x))


def _silu(x):
    return x * _sigmoid(x)


def _softplus(x):
    return jnp.maximum(x, 0.0) + jnp.log1p(jnp.exp(-jnp.abs(x)))


def _proj(xs, ws, w_specs, extras, e_specs, epilogue, out_shapes, n_cols, tm, tn, name,
          lead_grid=(), x_of_w=None):
    m = xs[0].shape[0]
    nl = len(lead_grid)
    grid = tuple(lead_grid) + (n_cols // tn, m // tm)
    x_specs = [pl.BlockSpec((tm, x.shape[1]), lambda *g: (g[-1], 0)) for x in xs]
    o_specs = [pl.BlockSpec((tm, tn), lambda *g: (g[-1], g[-2]) if nl == 0 else
                            (g[-1], g[0] * (n_cols // tn) + g[-2])) for _ in out_shapes]
    scratch = [pltpu.VMEM(tuple(d for d in s.block_shape if d is not None), BF16) for s in w_specs]
    xw = tuple(range(len(ws))) if x_of_w is None else tuple(x_of_w)

    def body(*refs):
        nx, nw, ne, no = len(xs), len(ws), len(extras), len(out_shapes)
        x_refs = refs[:nx]
        w_refs = refs[nx:nx + nw]
        e_refs = refs[nx + nw:nx + nw + ne]
        o_refs = refs[nx + nw + ne:nx + nw + ne + no]
        wb_refs = refs[nx + nw + ne + no:]

        @pl.when(pl.program_id(nl + 1) == 0)
        def _():
            for w, wb in zip(w_refs, wb_refs):
                wb[...] = w[...].astype(BF16)

        xv = [x[...].astype(BF16) for x in x_refs]
        accs = [jnp.dot(xv[j], wb[...], preferred_element_type=F32) for j, wb in zip(xw, wb_refs)]
        outs = epilogue(accs, [e[...] for e in e_refs], [pl.program_id(i) for i in range(nl)])
        for o, v in zip(o_refs, outs):
            o[...] = v.astype(o.dtype)

    return pl.pallas_call(
        body,
        out_shape=out_shapes,
        grid=grid,
        in_specs=x_specs + list(w_specs) + list(e_specs),
        out_specs=o_specs,
        scratch_shapes=scratch,
        compiler_params=_cparams(("arbitrary",) * len(grid)),
        name=name,
    )(*xs, *ws, *extras)


def _in_proj(xb, wt, l, tm):
    m, d = xb.shape
    tn = PROJ_TN
    n_blocks = N_MAIN // tn
    first_a = W_GATES_A // tn
    first_b = (W_GATES_B - SUBLANE) // tn
    nxt = 2 * SUBLANE

    def body(x_ref, wm_ref, wn_ref, o_ref, wb_ref):
        n = pl.program_id(0)

        @pl.when(pl.program_id(1) == 0)
        def _():
            def shifted(s):
                wcat = jnp.concatenate([wm_ref[...], wn_ref[...]], axis=0)
                wb_ref[...] = wcat[s:s + tn, :].astype(BF16)

            @pl.when(n < first_a)
            def _():
                wb_ref[...] = wm_ref[...].astype(BF16)

            @pl.when((n >= first_a) & (n < first_b))
            def _():
                shifted(SUBLANE)

            @pl.when(n >= first_b)
            def _():
                shifted(2 * SUBLANE)

        o_ref[...] = lax.dot_general(x_ref[...], wb_ref[...], (((1,), (1,)), ((), ())),
                                     preferred_element_type=F32)

    return pl.pallas_call(
        body,
        out_shape=jax.ShapeDtypeStruct((m, N_MAIN), F32),
        grid=(n_blocks, m // tm),
        in_specs=[pl.BlockSpec((tm, d), lambda n, i: (i, 0)),
                  pl.BlockSpec((None, tn, d), lambda n, i: (l, n, 0)),
                  pl.BlockSpec((None, nxt, d), lambda n, i: (l, (n + 1) * (tn // nxt), 0))],
        out_specs=pl.BlockSpec((tm, tn), lambda n, i: (i, n)),
        scratch_shapes=[pltpu.VMEM((tn, d), BF16)],
        compiler_params=_cparams(("arbitrary", "arbitrary")),
        name="in_proj",
    )(xb, wt, wt)


def _gate_proj(xb, wt, l, tm):
    m, d = xb.shape
    blk_a, blk_b = W_GATES_A // LANE, W_GATES_B // LANE

    def body(x_ref, w_ref, o_ref):
        o_ref[...] = lax.dot_general(x_ref[...], w_ref[...].astype(BF16), (((1,), (1,)), ((), ())),
                                     preferred_element_type=F32)

    return pl.pallas_call(
        body,
        out_shape=jax.ShapeDtypeStruct((m, 2 * LANE), F32),
        grid=(2, m // tm),
        in_specs=[pl.BlockSpec((tm, d), lambda n, i: (i, 0)),
                  pl.BlockSpec((None, LANE, d), lambda n, i: (l, blk_a + n * (blk_b - blk_a), 0))],
        out_specs=pl.BlockSpec((tm, LANE), lambda n, i: (i, n)),
        compiler_params=_cparams(("arbitrary", "arbitrary")),
        name="gate_proj",
    )(xb, wt)


def _rowfull_ln(h, w_bf16, res, extra, gamma, beta, tm, tk, name, split_rows=None):
    m, k = h.shape
    n = w_bf16.shape[1]
    nk = k // tk
    has_extra = extra is not None
    n_in = 6 if has_extra else 5
    tiles_a = None if split_rows is None else split_rows // tm

    def body(*refs):
        h_ref, w_ref, r_ref = refs[:3]
        e_ref = refs[3] if has_extra else None
        g_ref, b_ref = refs[n_in - 2:n_in]
        oa_ref, ob_ref, acc_ref = refs[n_in:]
        i = pl.program_id(0)
        kk = pl.program_id(1)

        @pl.when(kk == 0)
        def _():
            acc_ref[...] = jnp.zeros_like(acc_ref)

        acc_ref[...] += jnp.dot(h_ref[...], w_ref[...], preferred_element_type=F32)

        @pl.when(kk == nk - 1)
        def _():
            y = DEEPNORM_ALPHA * r_ref[...] + acc_ref[...]
            if e_ref is not None:
                y = y + e_ref[...].astype(F32)
            mu = jnp.mean(y, axis=-1, keepdims=True)
            yc = y - mu
            var = jnp.mean(yc * yc, axis=-1, keepdims=True)
            out = yc * lax.rsqrt(var + LN_EPS) * g_ref[...] + b_ref[...]
            if split_rows is None:
                oa_ref[...] = out
                ob_ref[...] = out.astype(BF16)
            else:
                @pl.when(i < tiles_a)
                def _():
                    oa_ref[...] = out

                @pl.when(i >= tiles_a)
                def _():
                    ob_ref[...] = out

    row = lambda i, j: (i, 0)
    in_specs = [pl.BlockSpec((tm, tk), lambda i, j: (i, j)),
                pl.BlockSpec((tk, n), lambda i, j: (j, 0), pipeline_mode=pl.Buffered(1 if nk == 1 else 2)),
                pl.BlockSpec((tm, n), row)]
    args = [h, w_bf16, res]
    if has_extra:
        in_specs.append(pl.BlockSpec((tm, n), row))
        args.append(extra)
    in_specs += [pl.BlockSpec((1, n), lambda i, j: (0, 0))] * 2
    args += [gamma.reshape(1, n), beta.reshape(1, n)]
    if split_rows is None:
        out_shape = [jax.ShapeDtypeStruct((m, n), F32), jax.ShapeDtypeStruct((m, n), BF16)]
        out_specs = [pl.BlockSpec((tm, n), row), pl.BlockSpec((tm, n), row)]
    else:
        out_shape = [jax.ShapeDtypeStruct((split_rows, n), F32), jax.ShapeDtypeStruct((m - split_rows, n), F32)]
        out_specs = [pl.BlockSpec((tm, n), lambda i, j: (jnp.minimum(i, tiles_a - 1), 0)),
                     pl.BlockSpec((tm, n), lambda i, j: (jnp.maximum(i - tiles_a, 0), 0))]
    return pl.pallas_call(
        body,
        out_shape=out_shape,
        grid=(m // tm, nk),
        in_specs=in_specs,
        out_specs=out_specs,
        scratch_shapes=[pltpu.VMEM((tm, n), F32)],
        compiler_params=_cparams(("arbitrary", "arbitrary")),
        name=name,
    )(*args)


def _seq_masks(g, lg_shift):
    row = lax.broadcasted_iota(jnp.int32, (CHUNK, CHUNK), 0)
    col = lax.broadcasted_iota(jnp.int32, (CHUNK, CHUNK), 1)
    if g == 1:
        return col <= row, col < row
    same = (row >> lg_shift) == (col >> lg_shift)
    return same & (col <= row), same & (col < row)


def _per_seq_rows(vals, g, lg):
    if g == 1:
        return vals[0]
    return jnp.concatenate([jnp.broadcast_to(v, (lg, v.shape[1])) for v in vals], axis=0)


def _last_rows(col, g, lg):
    if g == 1:
        return col[:, CHUNK - 1:, :]
    return _per_seq_rows([col[0, (i + 1) * lg - 1:(i + 1) * lg, :] for i in range(g)], g, lg)[None]


def _split3(x):
    x1 = x.astype(BF16).astype(F32)
    r1 = x - x1
    x2 = r1.astype(BF16).astype(F32)
    x3 = (r1 - x2).astype(BF16).astype(F32)
    return x1, x2, x3


def _split2(x):
    hi = x.astype(BF16)
    return hi, (x - hi.astype(F32)).astype(BF16)


def _lanes3(col, lane_ids, first, other):
    c1, c2, c3 = _split3(col)
    x = jnp.where(lane_ids == first, c1, jnp.where(lane_ids == first + 1, c2,
                                                   jnp.where(lane_ids == first + 2, c3, 0.0)))
    return jnp.where((lane_ids >= other) & (lane_ids < other + 3), 1.0, x).astype(BF16)


def _cumsum_rows(col, tri_b, lane_ids):
    return jnp.sum(_bmm(tri_b, _lanes3(col, lane_ids, 0, LANE)), axis=-1, keepdims=True)


def _outer_sum(a_col, b_col, lane_ids):
    return _bmm_nt(_lanes3(a_col, lane_ids, 0, 3), _lanes3(b_col, lane_ids, 3, 0))


def _bmm2(a, b):
    ah, al = _split2(a)
    bb = b.astype(BF16)
    return _bmm(ah, bb) + _bmm(al, bb)


def _conv_rows(x, prev, w, g, lg):
    f = x.shape[1]
    xx = jnp.concatenate([prev.reshape(g, SUBLANE, f), x.reshape(g, lg, f)], axis=1)
    y = xx[:, SUBLANE:, :] * w[CONV_W - 1:CONV_W, :]
    for s in range(1, CONV_W):
        y = y + xx[:, SUBLANE - s:SUBLANE - s + lg, :] * w[CONV_W - 1 - s:CONV_W - s, :]
    return y.reshape(g * lg, f), xx[:, lg:, :].reshape(g * SUBLANE, f)


def _mlstm_body(r, g, q_ref, k_ref, v_ref, o_ref, gt_ref, gb_ref, nw_ref, c0_ref, n0_ref, m0_ref,
                y_ref, c_out, n_out, m_out, c_s, n_s, m_s):
    lg = CHUNK // g
    lg_shift = int(math.log2(lg))
    nch = r // CHUNK
    h = pl.program_id(1)
    t = pl.program_id(2)

    @pl.when(t == 0)
    def _():
        c_s[...] = c0_ref[:, 0]
        n_s[...] = n0_ref[:, 0]
        m_s[...] = jnp.broadcast_to(m0_ref[:, 0], m_s.shape)

    lane3 = lax.broadcasted_iota(jnp.int32, (nch, CHUNK, LANE), 2)
    causal, _ = _seq_masks(g, lg_shift)
    tri_b = jnp.broadcast_to(causal.astype(BF16), (nch, CHUNK, CHUNK))

    pre = gt_ref[...].reshape(nch, CHUNK, LANE) + gb_ref[...]
    capd = ML_GATE_CAP * jnp.tanh(pre * (1.0 / ML_GATE_CAP))
    logsig = jnp.minimum(capd, 0.0) - jnp.log1p(jnp.exp(-jnp.abs(capd)))
    ic = jnp.sum(jnp.where(lane3 == G_MLI + h, capd, 0.0), axis=2, keepdims=True)
    lf = jnp.sum(jnp.where(lane3 == G_MLF + h, logsig, 0.0), axis=2, keepdims=True)
    bc = _cumsum_rows(lf, tri_b, lane3)
    dm = jnp.where(causal, _outer_sum(bc, ic - bc, lane3), -jnp.inf)
    rmax = jnp.max(dm, axis=2, keepdims=True)

    if g == 1:
        mp = m_s[0][:, 0:1]
        m_prevs = []
        for c in range(nch):
            m_prevs.append(mp.reshape(1, 1, 1))
            mp = jnp.maximum(mp + bc[c, CHUNK - 1:, :], rmax[c, CHUNK - 1:, :])
        m_prev = jnp.concatenate(m_prevs, axis=0) if nch > 1 else m_prevs[0]
        m_s[0] = jnp.broadcast_to(mp, (1, LANE))
    else:
        m_prev = _per_seq_rows([m_s[i][:, 0:1] for i in range(g)], g, lg)[None]
    m_t = jnp.maximum(m_prev + bc, rmax)
    dmat = jnp.exp(dm - m_t)
    q3 = q_ref[...].reshape(nch, CHUNK, ML_DK) * (ML_DK ** -0.5)
    k3 = k_ref[...].reshape(nch, CHUNK, ML_DK)
    v3 = v_ref[...].reshape(nch, CHUNK, ML_DV)
    s = _bmm_nt(q3, k3) * dmat
    inter = jnp.exp(bc + m_prev - m_t)
    bc_end = _last_rows(bc, g, lg)
    m_new = _last_rows(m_t, g, lg)
    w_col = jnp.exp(ic - bc + bc_end - m_new)
    dec = jnp.exp(bc_end + m_prev - m_new)
    sv = _bmm(s, v3)
    ssum = jnp.sum(s, axis=2, keepdims=True)
    vw = v3 * w_col
    kw = k3 * w_col
    floor = jnp.exp(-m_t)

    def emit(rows, hh):
        hn = hh * lax.rsqrt(jnp.mean(hh * hh, axis=1, keepdims=True) + RMS_EPS) * nw_ref[...]
        y_ref[rows, :] = (_sigmoid(o_ref[rows, :]) * hn).astype(y_ref.dtype)

    if g == 1:
        upd = _bmm_tn(vw, k3)
        ksum = jnp.sum(kw, axis=1, keepdims=True)
        cc, nn = c_s[0], n_s[0]
        for c in range(nch):
            num = sv[c] + inter[c] * _dot_nt(q3[c], cc)
            den = ssum[c] + inter[c] * jnp.sum(q3[c] * nn, axis=1, keepdims=True)
            emit(slice(c * CHUNK, (c + 1) * CHUNK), num / jnp.maximum(jnp.abs(den), floor[c]))
            cc = dec[c] * cc + upd[c]
            nn = dec[c] * nn + ksum[c]
        c_s[0] = cc
        n_s[0] = nn
    else:
        rows_i = lax.broadcasted_iota(jnp.int32, (CHUNK, 1), 0)
        q2, k2 = q3[0], k3[0]
        qc = jnp.concatenate([_dot_nt(q2[i * lg:(i + 1) * lg], c_s[i]) for i in range(g)], axis=0)
        qn = jnp.concatenate([jnp.sum(q2[i * lg:(i + 1) * lg] * n_s[i], axis=1, keepdims=True)
                              for i in range(g)], axis=0)
        num = sv[0] + inter[0] * qc
        den = ssum[0] + inter[0] * qn
        emit(slice(0, CHUNK), num / jnp.maximum(jnp.abs(den), floor[0]))
        for i in range(g):
            dec_i = dec[0, i * lg:i * lg + 1, :]
            c_s[i] = dec_i * c_s[i] + _dot_tn(jnp.where((rows_i >> lg_shift) == i, vw[0], 0.0), k2)
            n_s[i] = dec_i * n_s[i] + jnp.sum(kw[0, i * lg:(i + 1) * lg], axis=0, keepdims=True)
            m_s[i] = jnp.broadcast_to(m_new[0, i * lg:i * lg + 1, :], (1, LANE))

    @pl.when(t == pl.num_programs(2) - 1)
    def _():
        c_out[:, 0] = c_s[...]
        n_out[:, 0] = n_s[...]
        m_out[:, 0] = m_s[:, :, 0:1]


def _gdn_body(r, g, q_ref, k_ref, v_ref, z_ref, gt_ref, cwq_ref, cwk_ref, cwv_ref, al_ref, dtb_ref,
              nw_ref, bq_ref, bk_ref, bv_ref, s0_ref, y_ref, s_out, tq_out, tk_out, tv_out,
              s_s, pq_s, pk_s, pv_s):
    lg = CHUNK // g
    lg_shift = int(math.log2(lg))
    lseq = r if g == 1 else lg
    nch = r // CHUNK
    h = pl.program_id(1)
    t = pl.program_id(2)

    @pl.when(t == 0)
    def _():
        s_s[...] = s0_ref[:, 0]
        pq_s[...] = bq_ref[...]
        pk_s[...] = bk_ref[...]
        pv_s[...] = bv_ref[...]

    qa, tq = _conv_rows(q_ref[...], pq_s[...], cwq_ref[...], g, lseq)
    ka, tk = _conv_rows(k_ref[...], pk_s[...], cwk_ref[...], g, lseq)
    va, tv = _conv_rows(v_ref[...], pv_s[...], cwv_ref[...], g, lseq)
    pq_s[...] = tq
    pk_s[...] = tk
    pv_s[...] = tv
    qa, ka, va = _silu(qa), _silu(ka), _silu(va)
    qn = qa * lax.rsqrt(jnp.sum(qa * qa, axis=1, keepdims=True) + L2_EPS) * (GDN_DK ** -0.5)
    kn = ka * lax.rsqrt(jnp.sum(ka * ka, axis=1, keepdims=True) + L2_EPS)

    lane3 = lax.broadcasted_iota(jnp.int32, (nch, CHUNK, LANE), 2)
    causal, strict = _seq_masks(g, lg_shift)
    tri_b = jnp.broadcast_to(causal.astype(BF16), (nch, CHUNK, CHUNK))
    eye = (lax.broadcasted_iota(jnp.int32, (CHUNK, CHUNK), 0)
           == lax.broadcasted_iota(jnp.int32, (CHUNK, CHUNK), 1)).astype(F32)

    q3 = qn.reshape(nch, CHUNK, GDN_DK)
    k3 = kn.reshape(nch, CHUNK, GDN_DK)
    v3 = va.reshape(nch, CHUNK, GDN_DV)
    gt = gt_ref[...].reshape(nch, CHUNK, LANE)
    beta = jnp.sum(jnp.where(lane3 == G_GDB + h, _sigmoid(gt), 0.0), axis=2, keepdims=True)
    gval = -jnp.exp(al_ref[...]) * _softplus(gt + dtb_ref[...])
    gc = _cumsum_rows(jnp.sum(jnp.where(lane3 == G_GDA + h, gval, 0.0), axis=2, keepdims=True),
                      tri_b, lane3)
    decay = jnp.exp(jnp.where(causal, _outer_sum(gc, -gc, lane3), -jnp.inf))
    p = jnp.where(strict, -(beta * _bmm_nt(k3, k3) * decay), 0.0)
    inv = eye + p
    ph, pl_ = _split2(p)
    for _ in range(1, lg_shift):
        ph, pl_ = _split2(_bmm(ph, ph) + (_bmm(ph, pl_) + _bmm(pl_, ph)))
        ih, il = _split2(inv)
        inv = inv + (_bmm(ih, ph) + (_bmm(ih, pl_) + _bmm(il, ph)))
    egc = jnp.exp(gc)
    u_pre = _bmm2(inv, beta * v3)
    w = _bmm2(inv, (beta * egc) * k3)
    qk = _bmm_nt(q3, k3) * decay
    q_dec = q3 * egc
    g_end = _last_rows(gc, g, lg)
    k_dec = k3 * jnp.exp(g_end - gc)

    def emit(rows, o):
        hn = o * lax.rsqrt(jnp.mean(o * o, axis=1, keepdims=True) + RMS_EPS) * nw_ref[...]
        y_ref[rows, :] = (hn * _silu(z_ref[rows, :])).astype(y_ref.dtype)

    if g == 1:
        st = s_s[0]
        for c in range(nch):
            u = u_pre[c] - _dot(w[c], st)
            emit(slice(c * CHUNK, (c + 1) * CHUNK), _dot(q_dec[c], st) + _dot(qk[c], u))
            st = jnp.exp(g_end[c]) * st + _dot_tn(k_dec[c], u)
        s_s[0] = st
    else:
        rows_i = lax.broadcasted_iota(jnp.int32, (CHUNK, 1), 0)
        us, os_ = [], []
        for i in range(g):
            sl = slice(i * lg, (i + 1) * lg)
            us.append(u_pre[0, sl] - _dot(w[0, sl], s_s[i]))
            os_.append(_dot(q_dec[0, sl], s_s[i]))
        u = jnp.concatenate(us, axis=0)
        emit(slice(0, CHUNK), jnp.concatenate(os_, axis=0) + _dot(qk[0], u))
        for i in range(g):
            kdi = jnp.where((rows_i >> lg_shift) == i, k_dec[0], 0.0)
            s_s[i] = jnp.exp(g_end[0, i * lg:i * lg + 1, :]) * s_s[i] + _dot_tn(kdi, u)

    @pl.when(t == pl.num_programs(2) - 1)
    def _():
        s_out[:, 0] = s_s[...]
        tq_out[...] = tq
        tk_out[...] = tk
        tv_out[...] = tv


def _rglru_body(r, g, x_ref, yg_ref, cw_ref, cb_ref, wa_ref, wx_ref, ba_ref, bx_ref, lam_ref,
                buf_ref, h0_ref, y_ref, h_out, tail_out, h_s, px_s):
    lseq = r if g == 1 else CHUNK // g
    t = pl.program_id(2)

    @pl.when(t == 0)
    def _():
        h_s[...] = h0_ref[:, 0]
        px_s[...] = buf_ref[...]

    xc, tail = _conv_rows(x_ref[...], px_s[...], cw_ref[...], g, lseq)
    px_s[...] = tail
    xc = xc + cb_ref[...]
    rg = _sigmoid(_dot(xc, wa_ref[0]) + ba_ref[...])
    ig = _sigmoid(_dot(xc, wx_ref[0]) + bx_ref[...])
    log_a = -LRU_C * rg * _softplus(-lam_ref[...])
    a = jnp.exp(log_a)
    u = jnp.sqrt(1.0 - jnp.exp(2.0 * log_a)) * (ig * xc)
    pos = lax.broadcasted_iota(jnp.int32, (r, 1), 0) & (SUBLANE - 1)
    sh = 1
    while sh < SUBLANE:
        a_sh = pltpu.roll(a, sh, 0)
        u_sh = pltpu.roll(u, sh, 0)
        msk = pos >= sh
        u = jnp.where(msk, a * u_sh + u, u)
        a = jnp.where(msk, a * a_sh, a)
        sh *= 2
    blocks = []
    for i in range(g):
        carry = h_s[i]
        for k in range(lseq // SUBLANE):
            rows = slice(i * lseq + k * SUBLANE, i * lseq + (k + 1) * SUBLANE)
            blk = u[rows] + a[rows] * carry
            blocks.append(blk)
            carry = blk[SUBLANE - 1:, :]
        h_s[i] = carry
    hh = jnp.concatenate(blocks, axis=0)
    yg = yg_ref[...]
    gelu = 0.5 * yg * (1.0 + jnp.tanh(math.sqrt(2.0 / math.pi) * (yg + 0.044715 * (yg * yg * yg))))
    y_ref[...] = (gelu * hh).astype(y_ref.dtype)

    @pl.when(t == pl.num_programs(2) - 1)
    def _():
        h_out[:, 0] = h_s[...]
        tail_out[...] = tail


class _Group:
    def __init__(self, n_seq, seq_len, row0):
        self.n_seq, self.seq_len, self.row0 = n_seq, seq_len, row0
        self.rows = n_seq * seq_len
        if seq_len >= 4 * CHUNK:
            self.r, self.g = 4 * CHUNK, 1
        elif seq_len >= CHUNK:
            self.r, self.g = CHUNK, 1
        else:
            self.r, self.g = CHUNK, CHUNK // seq_len
        self.s = self.g
        self.nb = n_seq // self.s
        self.nt = seq_len // self.r if self.g == 1 else 1
        assert seq_len >= CONV_W - 1 and row0 % self.r == 0 and self.rows % self.r == 0
        assert n_seq % self.s == 0 and self.r & (self.r - 1) == 0
        self.rb0 = row0 // self.r

    def rb(self, b, t):
        return self.rb0 + b * self.nt + t


def _mixer_call(body, grp, l, depth, m_total, in_specs, args, y_width, y_block, state_shapes, extra_outs,
                scratch, prev, name):
    s = grp.s
    out_shape = [jax.ShapeDtypeStruct((m_total, y_width), BF16)]
    out_specs = [pl.BlockSpec((grp.r, y_block), lambda b, h, t: (grp.rb(b, t), h))]
    for tail in state_shapes:
        out_shape.append(jax.ShapeDtypeStruct((depth, grp.n_seq, y_width // y_block) + tail, F32))
        out_specs.append(pl.BlockSpec((None, s, 1) + tail, lambda b, h, t: (l, b, h) + (0,) * len(tail)))
    for shp, spec in extra_outs:
        out_shape.append(shp)
        out_specs.append(spec)
    n_in = len(args)
    aliases = {}
    in_specs = list(in_specs)
    args = list(args)
    for j, arr in enumerate(prev):
        if arr is not None:
            aliases[len(args)] = j
            in_specs.append(pl.BlockSpec(memory_space=pl.ANY))
            args.append(arr)
    n_alias = len(args) - n_in

    def wrapped(*refs):
        body(*refs[:n_in], *refs[n_in + n_alias:])

    return pl.pallas_call(
        wrapped,
        out_shape=out_shape,
        grid=(grp.nb, y_width // y_block, grp.nt),
        in_specs=in_specs,
        out_specs=out_specs,
        scratch_shapes=scratch,
        input_output_aliases=aliases,
        compiler_params=_cparams(("arbitrary",) * 3),
        name=name,
    )(*args)


def _mlstm_call(grp, l, depth, m_total, hmain, gates, gate_bias, norm_w, c0, n0, m0, prev):
    r, s = grp.r, grp.s
    row = lambda off, w: pl.BlockSpec((r, w), lambda b, h, t: (grp.rb(b, t), off // w + h))
    st = lambda *tail: pl.BlockSpec((None, s, 1) + tail, lambda b, h, t: (l, b, h, 0, 0))
    in_specs = [row(C_MLQ, ML_DK), row(C_MLK, ML_DK), row(C_MLV, ML_DV), row(C_MLO, ML_DV),
                pl.BlockSpec((r, LANE), lambda b, h, t: (grp.rb(b, t), 0)),
                pl.BlockSpec((1, LANE), lambda b, h, t: (0, 0)),
                pl.BlockSpec((1, ML_DV), lambda b, h, t: (0, h)),
                st(ML_DV, ML_DK), st(1, ML_DK), st(1, 1)]
    scratch = [pltpu.VMEM((s, ML_DV, ML_DK), F32), pltpu.VMEM((s, 1, ML_DK), F32),
               pltpu.VMEM((s, 1, LANE), F32)]
    return _mixer_call(functools.partial(_mlstm_body, r, grp.g), grp, l, depth, m_total, in_specs,
                       [hmain, hmain, hmain, hmain, gates, gate_bias, norm_w, c0, n0, m0],
                       ML_HEADS * ML_DV, ML_DV, [(ML_DV, ML_DK), (1, ML_DK), (1, 1)], [], scratch, prev, "mlstm")


def _gdn_call(grp, l, depth, m_total, hmain, gates, conv_w, a_log_row, dt_bias_row, norm_w, bufpad, s0, prev):
    r, s = grp.r, grp.s
    row = lambda off, w: pl.BlockSpec((r, w), lambda b, h, t: (grp.rb(b, t), off // w + h))
    cw = lambda off, w: pl.BlockSpec((CONV_W, w), lambda b, h, t: (0, off // w + h))
    bf = lambda off, w: pl.BlockSpec((s * SUBLANE, w), lambda b, h, t: (b, off // w + h))
    one = pl.BlockSpec((1, LANE), lambda b, h, t: (0, 0))
    in_specs = [row(C_GDQ, GDN_DK), row(C_GDK, GDN_DK), row(C_GDV, GDN_DV), row(C_GDZ, GDN_DV),
                pl.BlockSpec((r, LANE), lambda b, h, t: (grp.rb(b, t), 1)),
                cw(0, GDN_DK), cw(GDN_QK, GDN_DK), cw(2 * GDN_QK, GDN_DV),
                one, one, pl.BlockSpec((1, GDN_DV), lambda b, h, t: (0, 0)),
                bf(0, GDN_DK), bf(GDN_QK, GDN_DK), bf(2 * GDN_QK, GDN_DV),
                pl.BlockSpec((None, s, 1, GDN_DK, GDN_DV), lambda b, h, t: (l, b, h, 0, 0))]
    nrow = grp.n_seq * SUBLANE
    tail = lambda w: (jax.ShapeDtypeStruct((nrow, GDN_HEADS * w), F32),
                      pl.BlockSpec((s * SUBLANE, w), lambda b, h, t: (b, h)))
    scratch = [pltpu.VMEM((s, GDN_DK, GDN_DV), F32),
               pltpu.VMEM((s * SUBLANE, GDN_DK), F32), pltpu.VMEM((s * SUBLANE, GDN_DK), F32),
               pltpu.VMEM((s * SUBLANE, GDN_DV), F32)]
    return _mixer_call(functools.partial(_gdn_body, r, grp.g), grp, l, depth, m_total, in_specs,
                       [hmain, hmain, hmain, hmain, gates, conv_w, conv_w, conv_w, a_log_row, dt_bias_row,
                        norm_w, bufpad, bufpad, bufpad, s0],
                       GDN_HEADS * GDN_DV, GDN_DV, [(GDN_DK, GDN_DV)],
                       [tail(GDN_DK), tail(GDN_DK), tail(GDN_DV)], scratch, prev, "gdn")


def _rglru_call(grp, l, depth, m_total, hmain, conv_w, conv_b, w_a, w_x, b_a, b_x, lam, bufpad, h0, prev):
    r, s = grp.r, grp.s
    w = LRU_BLOCK
    row = lambda off: pl.BlockSpec((r, w), lambda b, h, t: (grp.rb(b, t), off // w + h))
    vec = pl.BlockSpec((1, w), lambda b, h, t: (0, h))
    blk = pl.BlockSpec((1, w, w), lambda b, h, t: (h, 0, 0))
    in_specs = [row(C_RGX), row(C_RGY),
                pl.BlockSpec((CONV_W, w), lambda b, h, t: (0, h)), vec, blk, blk, vec, vec, vec,
                pl.BlockSpec((s * SUBLANE, w), lambda b, h, t: (b, h)),
                pl.BlockSpec((None, s, 1, 1, w), lambda b, h, t: (l, b, h, 0, 0))]
    tail = (jax.ShapeDtypeStruct((grp.n_seq * SUBLANE, LRU_WIDTH), F32),
            pl.BlockSpec((s * SUBLANE, w), lambda b, h, t: (b, h)))
    scratch = [pltpu.VMEM((s, 1, w), F32), pltpu.VMEM((s * SUBLANE, w), F32)]
    return _mixer_call(functools.partial(_rglru_body, r, grp.g), grp, l, depth, m_total, in_specs,
                       [hmain, hmain, conv_w, conv_b, w_a, w_x, b_a, b_x, lam, bufpad, h0],
                       LRU_WIDTH, w, [(1, w)], [tail], scratch, prev, "rglru")


def _pad_conv_state(buf):
    b, k, c = buf.shape
    return jnp.pad(buf.astype(F32), ((0, 0), (SUBLANE - k, 0), (0, 0))).reshape(b * SUBLANE, c)


def _unpad_conv_state(tail, n_seq):
    return tail.reshape(n_seq, SUBLANE, -1)[:, SUBLANE - (CONV_W - 1):, :]


R_I1, R_I2, R_W1, R_W2, R_P1, R_P2 = 0, 1, 2, 3, 4, 5
MOE_TM = 512
MOE_TD = 512


def _router_call(x, router, tm):
    m, d = x.shape
    n_exp = router.shape[1]
    wr = jnp.pad(router.astype(F32), ((0, 0), (0, LANE - n_exp)))

    def body(x_ref, w_ref, o_ref, cnt_ref, carry):
        i = pl.program_id(0)

        @pl.when(i == 0)
        def _():
            carry[...] = jnp.zeros_like(carry)

        xh, xl = _split2(x_ref[...])
        wh, wl = _split2(w_ref[...])
        logits = _dot(xh, wh) + (_dot(xh, wl) + _dot(xl, wh))
        lane = lax.broadcasted_iota(jnp.int32, logits.shape, 1)
        lg = jnp.where(lane < n_exp, logits, -jnp.inf)
        v1 = jnp.max(lg, axis=1, keepdims=True)
        i1 = jnp.min(jnp.where(lg == v1, lane, LANE), axis=1, keepdims=True)
        lg2 = jnp.where(lane == i1, -jnp.inf, lg)
        v2 = jnp.max(lg2, axis=1, keepdims=True)
        i2 = jnp.min(jnp.where(lg2 == v2, lane, LANE), axis=1, keepdims=True)
        e2 = jnp.exp(v2 - v1)
        w1 = 1.0 / (1.0 + e2)
        w2 = e2 / (1.0 + e2)
        sel = ((lane == i1) | (lane == i2)).astype(F32)
        below = (lax.broadcasted_iota(jnp.int32, (tm, tm), 1)
                 < lax.broadcasted_iota(jnp.int32, (tm, tm), 0)).astype(BF16)
        rank = jnp.dot(below, sel.astype(BF16), preferred_element_type=F32) + carry[0:1, :]
        p1 = jnp.sum(jnp.where(lane == i1, rank, 0.0), axis=1, keepdims=True)
        p2 = jnp.sum(jnp.where(lane == i2, rank, 0.0), axis=1, keepdims=True)
        rec = jnp.where(lane == R_I1, i1.astype(F32), jnp.where(lane == R_I2, i2.astype(F32), 0.0))
        rec = jnp.where(lane == R_W1, w1, jnp.where(lane == R_W2, w2, rec))
        o_ref[...] = jnp.where(lane == R_P1, p1, jnp.where(lane == R_P2, p2, rec))
        carry[...] = carry[...] + jnp.sum(sel, axis=0, keepdims=True)
        cnt_ref[...] = carry[...]

    return pl.pallas_call(
        body,
        out_shape=[jax.ShapeDtypeStruct((m, LANE), F32), jax.ShapeDtypeStruct((SUBLANE, LANE), F32)],
        grid=(m // tm,),
        in_specs=[pl.BlockSpec((tm, d), lambda i: (i, 0)), pl.BlockSpec((d, LANE), lambda i: (0, 0))],
        out_specs=[pl.BlockSpec((tm, LANE), lambda i: (i, 0)), pl.BlockSpec((SUBLANE, LANE), lambda i: (0, 0))],
        scratch_shapes=[pltpu.VMEM((SUBLANE, LANE), F32)],
        compiler_params=_cparams(("arbitrary",)),
        name="router",
    )(x, wr)


def _source_table(starts, e_arr, p_arr, m, ns):
    td = MOE_TD

    def body(start_ref, e_ref, p_ref, tbl_ref):
        i = pl.program_id(0)

        @pl.when(i == 0)
        def _():
            def clear(r, c):
                tbl_ref[r] = 0
                return c

            lax.fori_loop(0, ns, clear, 0, unroll=8)

        base = i * td

        def put(j, c):
            tbl_ref[start_ref[e_ref[0, j]] + p_ref[0, j]] = base + (j >> 1)
            return c

        lax.fori_loop(0, 2 * td, put, 0, unroll=8)

    smem = pl.BlockSpec((None, 1, 2 * td), lambda i, s: (i, 0, 0), memory_space=pltpu.SMEM)
    return pl.pallas_call(
        body,
        grid_spec=pltpu.PrefetchScalarGridSpec(
            num_scalar_prefetch=1, grid=(m // td,), in_specs=[smem, smem],
            out_specs=pl.BlockSpec(memory_space=pltpu.SMEM)),
        out_shape=jax.ShapeDtypeStruct((ns,), jnp.int32),
        compiler_params=_cparams(("arbitrary",)),
        name="moe_table",
    )(starts, e_arr, p_arr)


def _start_rows(copy_of, n_rows):
    def issue(r, c):
        copy_of(r).start()
        return c

    lax.fori_loop(0, n_rows, issue, 0, unroll=8)


def _wait_rows(copy_of, n_rows):
    def drain(r, c):
        copy_of(r).wait()
        return c

    lax.fori_loop(0, n_rows, drain, 0, unroll=8)


def _dispatch(x, table, n_used):
    m, d = x.shape
    ns = table.shape[0]
    tm = MOE_TM
    nt = ns // tm

    def body(nu_ref, src_ref, nxt_ref, x_ref, o_ref, buf, sem):
        i = pl.program_id(0)
        slot = i & 1

        def rows(idx_ref, s):
            return lambda r: pltpu.make_async_copy(x_ref.at[pl.ds(idx_ref[0, r], 1)],
                                                   buf.at[s, pl.ds(r, 1)], sem.at[s])

        @pl.when(i == 0)
        def _():
            _start_rows(rows(src_ref, 0), tm)

        @pl.when(i + 1 < nu_ref[0])
        def _():
            _start_rows(rows(nxt_ref, 1 - slot), tm)

        @pl.when(i < nu_ref[0])
        def _():
            _wait_rows(rows(src_ref, slot), tm)
            o_ref[...] = buf[slot].astype(BF16)

    idx = lambda off: pl.BlockSpec((None, 1, tm), lambda i, nu: (jnp.minimum(i + off, nt - 1), 0, 0),
                                   memory_space=pltpu.SMEM)
    tbl = table.reshape(nt, 1, tm)
    return pl.pallas_call(
        body,
        grid_spec=pltpu.PrefetchScalarGridSpec(
            num_scalar_prefetch=1, grid=(nt,),
            in_specs=[idx(0), idx(1), pl.BlockSpec(memory_space=pl.ANY)],
            out_specs=pl.BlockSpec((tm, d), lambda i, nu: (jnp.minimum(i, nu[0] - 1), 0)),
            scratch_shapes=[pltpu.VMEM((2, tm, d), F32), pltpu.SemaphoreType.DMA((2,))]),
        out_shape=jax.ShapeDtypeStruct((ns, d), BF16),
        compiler_params=_cparams(("arbitrary",)),
        name="moe_dispatch",
    )(n_used, tbl, tbl, x)


def _grouped(rows, ws, tile_expert, tile_first, n_used, epilogue, out_dtype, n_cols, tn, name, w_buffers=2):
    ns, k = rows.shape
    nt = ns // MOE_TM
    used = lambda i, nu: jnp.minimum(i, nu[0] - 1)
    w_spec = pl.BlockSpec((None, k, tn), lambda n, i, te, tf, nu: (te[i], 0, n),
                          pipeline_mode=pl.Buffered(w_buffers))

    def body(te_ref, tf_ref, nu_ref, x_ref, *refs):
        w_refs, o_ref, wb_refs = refs[:len(ws)], refs[len(ws)], refs[len(ws) + 1:]
        i = pl.program_id(1)

        @pl.when(tf_ref[i] == 1)
        def _():
            for w, wb in zip(w_refs, wb_refs):
                wb[...] = w[...].astype(BF16)

        @pl.when(i < nu_ref[0])
        def _():
            xv = x_ref[...].astype(BF16)
            accs = [jnp.dot(xv, wb[...], preferred_element_type=F32) for wb in wb_refs]
            o_ref[...] = epilogue(accs).astype(o_ref.dtype)

    return pl.pallas_call(
        body,
        grid_spec=pltpu.PrefetchScalarGridSpec(
            num_scalar_prefetch=3, grid=(n_cols // tn, nt),
            in_specs=[pl.BlockSpec((MOE_TM, k), lambda n, i, te, tf, nu: (used(i, nu), 0))] + [w_spec] * len(ws),
            out_specs=pl.BlockSpec((MOE_TM, tn), lambda n, i, te, tf, nu: (used(i, nu), n)),
            scratch_shapes=[pltpu.VMEM((k, tn), BF16) for _ in ws]),
        out_shape=jax.ShapeDtypeStruct((ns, n_cols), out_dtype),
        compiler_params=_cparams(("arbitrary", "arbitrary")),
        name=name,
    )(tile_expert, tile_first, n_used, rows, *ws)


def _combine_ln(x, ys, starts, e_arr, p_arr, rec, ple, gamma, beta, split_rows):
    tm = MOE_TD
    m, n = x.shape
    n_tiles = m // tm
    tiles_a = None if split_rows is None else split_rows // tm

    def body(start_ref, es_ref, ps_ref, en_ref, pn_ref, ys_ref, x_ref, r_ref, e_ref, g_ref, b_ref,
             oa_ref, ob_ref, ybuf, sem):
        i = pl.program_id(0)
        slot = i & 1

        def rows(e_ref_, p_ref_, s):
            return lambda j: pltpu.make_async_copy(
                ys_ref.at[pl.ds(start_ref[e_ref_[0, j]] + p_ref_[0, j], 1)],
                ybuf.at[s, j & 1, pl.ds(j >> 1, 1)], sem.at[s])

        @pl.when(i == 0)
        def _():
            _start_rows(rows(es_ref, ps_ref, 0), 2 * tm)

        @pl.when(i + 1 < n_tiles)
        def _():
            _start_rows(rows(en_ref, pn_ref, 1 - slot), 2 * tm)

        _wait_rows(rows(es_ref, ps_ref, slot), 2 * tm)
        r = r_ref[...]
        lane = lax.broadcasted_iota(jnp.int32, r.shape, 1)
        w1 = jnp.sum(jnp.where(lane == R_W1, r, 0.0), axis=1, keepdims=True)
        w2 = jnp.sum(jnp.where(lane == R_W2, r, 0.0), axis=1, keepdims=True)
        y = DEEPNORM_ALPHA * x_ref[...] + (w1 * ybuf[slot, 0] + w2 * ybuf[slot, 1]) + e_ref[...]
        mu = jnp.mean(y, axis=-1, keepdims=True)
        yc = y - mu
        var = jnp.mean(yc * yc, axis=-1, keepdims=True)
        out = yc * lax.rsqrt(var + LN_EPS) * g_ref[...] + b_ref[...]
        if split_rows is None:
            oa_ref[...] = out
            ob_ref[...] = out.astype(BF16)
        else:
            @pl.when(i < tiles_a)
            def _():
                oa_ref[...] = out

            @pl.when(i >= tiles_a)
            def _():
                ob_ref[...] = out

    row = lambda i, s: (i, 0)
    vec = pl.BlockSpec((1, n), lambda i, s: (0, 0))
    smem = lambda off: pl.BlockSpec((None, 1, 2 * tm), lambda i, s: (jnp.minimum(i + off, n_tiles - 1), 0, 0),
                                    memory_space=pltpu.SMEM)
    if split_rows is None:
        out_shape = [jax.ShapeDtypeStruct((m, n), F32), jax.ShapeDtypeStruct((m, n), BF16)]
        out_specs = [pl.BlockSpec((tm, n), row), pl.BlockSpec((tm, n), row)]
    else:
        out_shape = [jax.ShapeDtypeStruct((split_rows, n), F32), jax.ShapeDtypeStruct((m - split_rows, n), F32)]
        out_specs = [pl.BlockSpec((tm, n), lambda i, s: (jnp.minimum(i, tiles_a - 1), 0)),
                     pl.BlockSpec((tm, n), lambda i, s: (jnp.maximum(i - tiles_a, 0), 0))]
    return pl.pallas_call(
        body,
        grid_spec=pltpu.PrefetchScalarGridSpec(
            num_scalar_prefetch=1, grid=(m // tm,),
            in_specs=[smem(0), smem(0), smem(1), smem(1), pl.BlockSpec(memory_space=pl.ANY),
                      pl.BlockSpec((tm, n), row), pl.BlockSpec((tm, LANE), row), pl.BlockSpec((tm, n), row), vec, vec],
            out_specs=out_specs,
            scratch_shapes=[pltpu.VMEM((2, 2, tm, n), F32), pltpu.SemaphoreType.DMA((2,))]),
        out_shape=out_shape,
        compiler_params=_cparams(("arbitrary",)),
        name="moe_combine_ln",
    )(starts, e_arr, p_arr, e_arr, p_arr, ys, x, rec, ple, gamma.reshape(1, n), beta.reshape(1, n))


def _moe_ln(x, ple, router, w1, w3, w2, j, gamma, beta, split_rows):
    m, d = x.shape
    _, ne, _, fe = w1.shape
    w1, w3, w2 = (w.reshape((-1,) + w.shape[2:]) for w in (w1, w3, w2))
    rec, counts = _router_call(x, router, 512)
    cnt = counts[0, :ne].astype(jnp.int32)
    tiles_e = (cnt + MOE_TM - 1) // MOE_TM
    tile_end = jnp.cumsum(tiles_e)
    tile_beg = tile_end - tiles_e
    n_used = tile_end[ne - 1]
    nt = (2 * m) // MOE_TM + ne
    ids = jnp.minimum(jnp.arange(nt, dtype=jnp.int32), n_used - 1)
    tile_expert = jnp.sum((ids[:, None] >= tile_end[None, :]).astype(jnp.int32), axis=1)
    tile_first = (jnp.arange(nt, dtype=jnp.int32) == tile_beg[tile_expert]).astype(jnp.int32)
    starts = (tile_beg * MOE_TM).astype(jnp.int32)
    slots = lambda a, b: jnp.stack([rec[:, a], rec[:, b]], axis=1).astype(jnp.int32).reshape(m // MOE_TD, 1, 2 * MOE_TD)
    e_arr, p_arr = slots(R_I1, R_I2), slots(R_P1, R_P2)
    ns = nt * MOE_TM

    nu = n_used.reshape(1)
    xs = _dispatch(x, _source_table(starts, e_arr, p_arr, m, ns), nu)
    tile_w = tile_expert + j * ne
    hmid = _grouped(xs, [w1, w3], tile_w, tile_first, nu, lambda a: _silu(a[0]) * a[1], BF16, fe, fe // 2,
                    "moe_up", w_buffers=1)
    ys = _grouped(hmid, [w2], tile_w, tile_first, nu, lambda a: a[0], F32, d, 1024, "moe_down")
    return _combine_ln(x, ys, starts, e_arr, p_arr, rec, ple, gamma, beta, split_rows)


def _run_group_mixers(grp, l, depth, m_total, hmain, gates, st, prm, prev):
    b = grp.n_seq
    ya, ml_c, ml_n, ml_m = _mlstm_call(
        grp, l, depth, m_total, hmain, gates, prm["gate_bias"], prm["ml_norm_w"],
        st["ml_C"].astype(F32), st["ml_n"].astype(F32).reshape(depth, b, ML_HEADS, 1, ML_DK),
        st["ml_m"].astype(F32).reshape(depth, b, ML_HEADS, 1, 1),
        [prev["ya"], prev["ml_C"], prev["ml_n"], prev["ml_m"]])
    yb, rg_h, rg_tail = _rglru_call(
        grp, l, depth, m_total, hmain, prm["rg_conv_w"], prm["rg_conv_b"], prm["rg_w_a"], prm["rg_w_x"],
        prm["rg_b_a"], prm["rg_b_x"], prm["rg_lambda"], _pad_conv_state(st["rg_conv"][l]),
        st["rg_h"].astype(F32).reshape(depth, b, LRU_BLOCKS, 1, LRU_BLOCK), [prev["yb"], prev["rg_h"]])
    yc, gd_s, tq, tk, tv = _gdn_call(
        grp, l, depth, m_total, hmain, gates, prm["gd_conv_w"], prm["gd_a_log"], prm["gd_dt_bias"],
        prm["gd_norm_w"], _pad_conv_state(st["gd_conv"][l]), st["gd_S"].astype(F32),
        [prev["yc"], prev["gd_S"]])
    out = dict(ya=ya, yb=yb, yc=yc, ml_C=ml_c, ml_n=ml_n, ml_m=ml_m, rg_h=rg_h, gd_S=gd_s)
    tails = dict(rg_conv=_unpad_conv_state(rg_tail, b),
                 gd_conv=jnp.concatenate([_unpad_conv_state(x, b) for x in (tq, tk, tv)], axis=-1))
    return out, tails


_STATE_KEYS = ("ml_C", "ml_n", "ml_m", "rg_h", "gd_S")


def kernel(x_prompt, x_sample, state_mlstm_C, state_mlstm_n, state_mlstm_m, state_rglru_h, state_rglru_conv, state_gdn_S, state_gdn_conv, p_prompt, p_sample, w_in, ml_b_i, ml_b_f, ml_norm_w, rg_conv_w, rg_conv_b, rg_w_a, rg_b_a, rg_w_x, rg_b_x, rg_lambda, gd_conv_w, gd_A_log, gd_dt_bias, gd_norm_w, w_up_mlstm, w_up_rglru, w_up_gdn, w_out, ln1_g, ln1_b, ffn_w1, ffn_w3, ffn_w2, moe_router, moe_w1, moe_w3, moe_w2, ple_w, ple_gate_w, ln2_g, ln2_b):
    bp, tp, d = x_prompt.shape
    bs, ts, _ = x_sample.shape
    depth = w_in.shape[0]
    gp = _Group(bp, tp, 0)
    gs = _Group(bs, ts, bp * tp)
    m = gp.rows + gs.rows
    pd = x_prompt.dtype

    zeros = lambda *shape: jnp.zeros(shape, F32)
    st_p = dict(ml_C=zeros(depth, bp, ML_HEADS, ML_DV, ML_DK), ml_n=zeros(depth, bp, ML_HEADS, ML_DK),
                ml_m=zeros(depth, bp, ML_HEADS), rg_h=zeros(depth, bp, LRU_WIDTH),
                rg_conv=zeros(depth, bp, CONV_W - 1, LRU_WIDTH),
                gd_S=zeros(depth, bp, GDN_HEADS, GDN_DK, GDN_DV),
                gd_conv=zeros(depth, bp, CONV_W - 1, GDN_CONV_DIM))
    st_s = dict(ml_C=state_mlstm_C, ml_n=state_mlstm_n, ml_m=state_mlstm_m, rg_h=state_rglru_h,
                rg_conv=state_rglru_conv, gd_S=state_gdn_S, gd_conv=state_gdn_conv)

    x = jnp.concatenate([x_prompt.reshape(gp.rows, d), x_sample.reshape(gs.rows, d)], axis=0).astype(F32)
    xb = x.astype(BF16)
    w_in_t = jnp.swapaxes(w_in, 1, 2)
    lane_row = lambda v, off: jnp.zeros((1, LANE), F32).at[0, off:off + v.shape[0]].set(v.astype(F32))
    acc_p = {k: None for k in _STATE_KEYS}
    acc_s = {k: None for k in _STATE_KEYS}
    tails_p, tails_s = [], []

    for l in range(depth):
        hmain = _in_proj(xb, w_in_t, l, 1024)
        gates = _gate_proj(xb, w_in_t, l, 1024)

        prm = dict(
            gate_bias=lane_row(ml_b_i[l], G_MLI) + lane_row(ml_b_f[l], G_MLF),
            ml_norm_w=ml_norm_w[l].astype(F32).reshape(1, -1),
            rg_conv_w=rg_conv_w[l].astype(F32), rg_conv_b=rg_conv_b[l].astype(F32).reshape(1, -1),
            rg_w_a=rg_w_a[l], rg_w_x=rg_w_x[l], rg_b_a=rg_b_a[l].astype(F32).reshape(1, -1),
            rg_b_x=rg_b_x[l].astype(F32).reshape(1, -1), rg_lambda=rg_lambda[l].astype(F32).reshape(1, -1),
            gd_conv_w=gd_conv_w[l].astype(F32), gd_a_log=lane_row(gd_A_log[l], G_GDA),
            gd_dt_bias=lane_row(gd_dt_bias[l], G_GDA), gd_norm_w=gd_norm_w[l].astype(F32).reshape(1, -1))
        out_p, tl_p = _run_group_mixers(gp, l, depth, m, hmain, gates, st_p, prm,
                                        dict(acc_p, ya=None, yb=None, yc=None))
        out_s, tl_s = _run_group_mixers(gs, l, depth, m, hmain, gates, st_s, prm,
                                        dict(acc_s, ya=out_p["ya"], yb=out_p["yb"], yc=out_p["yc"]))
        acc_p = {k: out_p[k] for k in _STATE_KEYS}
        acc_s = {k: out_s[k] for k in _STATE_KEYS}
        tails_p.append(tl_p)
        tails_s.append(tl_s)
        ya, yb, yc = out_s["ya"], out_s["yb"], out_s["yc"]

        tn, tm = 1024, 512
        up_spec = pl.BlockSpec((None, ya.shape[1], tn), lambda n, i: (l, 0, n))
        mg_spec = lambda br: pl.BlockSpec((tm, tn), lambda n, i: (i, (C_MG + br * d) // tn + n))
        merged = _proj([ya, yb, yc], [w_up_mlstm, w_up_rglru, w_up_gdn], [up_spec] * 3,
                       [hmain] * 3, [mg_spec(0), mg_spec(1), mg_spec(2)],
                       lambda accs, ex, ids: [_sigmoid(ex[0]) * accs[0] + _sigmoid(ex[1]) * accs[1]
                                              + _sigmoid(ex[2]) * accs[2]],
                       [jax.ShapeDtypeStruct((m, d), BF16)], d, tm, tn, "merge")[0]
        x, xb = _rowfull_ln(merged, w_out[l].astype(BF16), x, None, ln1_g[l].astype(F32), ln1_b[l].astype(F32),
                            512, d, "out_proj_ln")

        p_l = jnp.concatenate([p_prompt[l].reshape(gp.rows, -1), p_sample[l].reshape(gs.rows, -1)], axis=0)
        ple = _proj([xb, p_l], [ple_gate_w, ple_w],
                    [pl.BlockSpec((None, d, 1024), lambda n, i: (l, 0, n)),
                     pl.BlockSpec((None, P_DIM, 1024), lambda n, i: (l, 0, n))],
                    [], [], lambda accs, ex, ids: [_sigmoid(accs[0]) * accs[1]],
                    [jax.ShapeDtypeStruct((m, d), F32)], d, 1024, 1024, "ple")[0]
        j = l // 2
        split = gp.rows if l == depth - 1 else None
        g2, b2 = ln2_g[l].astype(F32), ln2_b[l].astype(F32)
        if l % 2 == 0:
            ff = ffn_w1.shape[2]
            hmid = _proj([xb], [ffn_w1, ffn_w3], [pl.BlockSpec((None, d, 512), lambda n, i: (j, 0, n))] * 2,
                         [], [], lambda accs, ex, ids: [_silu(accs[0]) * accs[1]],
                         [jax.ShapeDtypeStruct((m, ff), BF16)], ff, 1024, 512, "ffn_up", x_of_w=(0, 0))[0]
            x, xb = _rowfull_ln(hmid, ffn_w2[j].astype(BF16), x, ple, g2, b2, 256, ff, "ffn_down_ln",
                                split_rows=split)
        else:
            x, xb = _moe_ln(x, ple, moe_router[j], moe_w1, moe_w3, moe_w2, j, g2, b2, split)

    y_prompt = x.reshape(bp, tp, d).astype(pd)
    y_sample = xb.reshape(bs, ts, d).astype(x_sample.dtype)

    def finish(acc, tails, n_seq, dtypes):
        conv = {k: jnp.stack([tl[k] for tl in tails]) for k in ("rg_conv", "gd_conv")}
        vals = dict(ml_C=acc["ml_C"], ml_n=acc["ml_n"].reshape(depth, n_seq, ML_HEADS, ML_DK),
                    ml_m=acc["ml_m"].reshape(depth, n_seq, ML_HEADS),
                    rg_h=acc["rg_h"].reshape(depth, n_seq, LRU_WIDTH), rg_conv=conv["rg_conv"],
                    gd_S=acc["gd_S"], gd_conv=conv["gd_conv"])
        names = ("ml_C", "ml_n", "ml_m", "rg_h", "rg_conv", "gd_S", "gd_conv")
        return tuple(vals[k].astype(dtypes[k]) for k in names)

    return ((y_prompt, y_sample) + finish(acc_p, tails_p, bp, {k: pd for k in st_p})
            + finish(acc_s, tails_s, bs, {k: v.dtype for k, v in st_s.items()}))
```

```python
import functools
import math

import jax
import jax.numpy as jnp
from jax import lax
from jax.experimental import pallas as pl
from jax.experimental.pallas import tpu as pltpu

F32 = jnp.float32
BF16 = jnp.bfloat16

D_MODEL = 2048
DEPTH = 2
P_DIM = 256
CONV_W = 4
N_BRANCH = 3
ML_HEADS = 4
ML_DK = 128
ML_DV = 256
ML_GATE_CAP = 15.0
LRU_WIDTH = 1024
LRU_BLOCKS = 4
LRU_BLOCK = LRU_WIDTH // LRU_BLOCKS
LRU_C = 8.0
GDN_HEADS = 4
GDN_DK = 128
GDN_DV = 256
GDN_QK = GDN_HEADS * GDN_DK
GDN_CONV_DIM = 2 * GDN_QK + GDN_HEADS * GDN_DV
N_EXPERTS = 8
DEEPNORM_ALPHA = (2 * DEPTH) ** 0.25
LN_EPS = 1e-5
RMS_EPS = 1e-6
L2_EPS = 1e-6

LANE = 128
SUBLANE = 8
CHUNK = 128
VMEM_LIMIT = 56 * 1024 * 1024

C_MLQ, C_MLK, C_MLV, C_MLO = 0, 512, 1024, 2048
C_RGX, C_RGY = 3072, 4096
C_GDQ, C_GDK, C_GDV, C_GDZ = 5120, 5632, 6144, 7168
C_MG = 8192
N_MAIN = C_MG + N_BRANCH * D_MODEL
W_GATES_A, W_GATES_B = 2048, 7176
G_MLI, G_MLF, G_GDB, G_GDA = 0, 4, 8, 12
PROJ_TN = 1024


def _cparams(sem):
    return pltpu.CompilerParams(dimension_semantics=sem, vmem_limit_bytes=VMEM_LIMIT)


def _dot(a, b):
    return jnp.dot(a.astype(BF16), b.astype(BF16), preferred_element_type=F32)


def _dot_nt(a, b):
    return lax.dot_general(a.astype(BF16), b.astype(BF16), (((1,), (1,)), ((), ())),
                           preferred_element_type=F32)


def _dot_tn(a, b):
    return lax.dot_general(a.astype(BF16), b.astype(BF16), (((0,), (0,)), ((), ())),
                           preferred_element_type=F32)


def _bmm(a, b):
    return jnp.einsum('cij,cjk->cik', a.astype(BF16), b.astype(BF16), preferred_element_type=F32)


def _bmm_nt(a, b):
    return jnp.einsum('cik,cjk->cij', a.astype(BF16), b.astype(BF16), preferred_element_type=F32)


def _bmm_tn(a, b):
    return jnp.einsum('csi,csj->cij', a.astype(BF16), b.astype(BF16), preferred_element_type=F32)


def _sigmoid(x):
    return 1.0 / (1.0 + jnp.exp(-x))


def _silu(x):
    return x * _sigmoid(x)


def _softplus(x):
    return jnp.maximum(x, 0.0) + jnp.log1p(jnp.exp(-jnp.abs(x)))


def _proj(xs, ws, w_specs, extras, e_specs, epilogue, out_shapes, n_cols, tm, tn, name,
          lead_grid=(), x_of_w=None):
    m = xs[0].shape[0]
    nl = len(lead_grid)
    grid = tuple(lead_grid) + (n_cols // tn, m // tm)
    x_specs = [pl.BlockSpec((tm, x.shape[1]), lambda *g: (g[-1], 0)) for x in xs]
    o_specs = [pl.BlockSpec((tm, tn), lambda *g: (g[-1], g[-2]) if nl == 0 else
                            (g[-1], g[0] * (n_cols // tn) + g[-2])) for _ in out_shapes]
    scratch = [pltpu.VMEM(tuple(d for d in s.block_shape if d is not None), BF16) for s in w_specs]
    xw = tuple(range(len(ws))) if x_of_w is None else tuple(x_of_w)

    def body(*refs):
        nx, nw, ne, no = len(xs), len(ws), len(extras), len(out_shapes)
        x_refs = refs[:nx]
        w_refs = refs[nx:nx + nw]
        e_refs = refs[nx + nw:nx + nw + ne]
        o_refs = refs[nx + nw + ne:nx + nw + ne + no]
        wb_refs = refs[nx + nw + ne + no:]

        @pl.when(pl.program_id(nl + 1) == 0)
        def _():
            for w, wb in zip(w_refs, wb_refs):
                wb[...] = w[...].astype(BF16)

        xv = [x[...].astype(BF16) for x in x_refs]
        accs = [jnp.dot(xv[j], wb[...], preferred_element_type=F32) for j, wb in zip(xw, wb_refs)]
        outs = epilogue(accs, [e[...] for e in e_refs], [pl.program_id(i) for i in range(nl)])
        for o, v in zip(o_refs, outs):
            o[...] = v.astype(o.dtype)

    return pl.pallas_call(
        body,
        out_shape=out_shapes,
        grid=grid,
        in_specs=x_specs + list(w_specs) + list(e_specs),
        out_specs=o_specs,
        scratch_shapes=scratch,
        compiler_params=_cparams(("arbitrary",) * len(grid)),
        name=name,
    )(*xs, *ws, *extras)


def _in_proj(xb, wt, l, tm):
    m, d = xb.shape
    tn = PROJ_TN
    n_blocks = N_MAIN // tn
    first_a = W_GATES_A // tn
    first_b = (W_GATES_B - SUBLANE) // tn
    nxt = 2 * SUBLANE

    def body(x_ref, wm_ref, wn_ref, o_ref, wb_ref):
        n = pl.program_id(0)

        @pl.when(pl.program_id(1) == 0)
        def _():
            def shifted(s):
                wcat = jnp.concatenate([wm_ref[...], wn_ref[...]], axis=0)
                wb_ref[...] = wcat[s:s + tn, :].astype(BF16)

            @pl.when(n < first_a)
            def _():
                wb_ref[...] = wm_ref[...].astype(BF16)

            @pl.when((n >= first_a) & (n < first_b))
            def _():
                shifted(SUBLANE)

            @pl.when(n >= first_b)
            def _():
                shifted(2 * SUBLANE)

        o_ref[...] = lax.dot_general(x_ref[...], wb_ref[...], (((1,), (1,)), ((), ())),
                                     preferred_element_type=F32)

    return pl.pallas_call(
        body,
        out_shape=jax.ShapeDtypeStruct((m, N_MAIN), F32),
        grid=(n_blocks, m // tm),
        in_specs=[pl.BlockSpec((tm, d), lambda n, i: (i, 0)),
                  pl.BlockSpec((None, tn, d), lambda n, i: (l, n, 0)),
                  pl.BlockSpec((None, nxt, d), lambda n, i: (l, (n + 1) * (tn // nxt), 0))],
        out_specs=pl.BlockSpec((tm, tn), lambda n, i: (i, n)),
        scratch_shapes=[pltpu.VMEM((tn, d), BF16)],
        compiler_params=_cparams(("arbitrary", "arbitrary")),
        name="in_proj",
    )(xb, wt, wt)


def _gate_proj(xb, wt, l, tm):
    m, d = xb.shape
    blk_a, blk_b = W_GATES_A // LANE, W_GATES_B // LANE

    def body(x_ref, wa_ref, wb_ref, o_ref):
        w = jnp.concatenate([wa_ref[...], wb_ref[...]], axis=0).astype(BF16)
        o_ref[...] = lax.dot_general(x_ref[...], w, (((1,), (1,)), ((), ())), preferred_element_type=F32)

    return pl.pallas_call(
        body,
        out_shape=jax.ShapeDtypeStruct((m, 2 * LANE), F32),
        grid=(m // tm,),
        in_specs=[pl.BlockSpec((tm, d), lambda i: (i, 0)),
                  pl.BlockSpec((None, LANE, d), lambda i: (l, blk_a, 0)),
                  pl.BlockSpec((None, LANE, d), lambda i: (l, blk_b, 0))],
        out_specs=pl.BlockSpec((tm, 2 * LANE), lambda i: (i, 0)),
        compiler_params=_cparams(("arbitrary",)),
        name="gate_proj",
    )(xb, wt, wt)


def _rowfull_ln(h, w_bf16, res, extra, gamma, beta, tm, tk, name, split_rows=None):
    m, k = h.shape
    n = w_bf16.shape[1]
    nk = k // tk
    has_extra = extra is not None
    n_in = 6 if has_extra else 5
    tiles_a = None if split_rows is None else split_rows // tm

    def body(*refs):
        h_ref, w_ref, r_ref = refs[:3]
        e_ref = refs[3] if has_extra else None
        g_ref, b_ref = refs[n_in - 2:n_in]
        oa_ref, ob_ref, acc_ref = refs[n_in:]
        i = pl.program_id(0)
        kk = pl.program_id(1)

        @pl.when(kk == 0)
        def _():
            acc_ref[...] = jnp.zeros_like(acc_ref)

        acc_ref[...] += jnp.dot(h_ref[...], w_ref[...], preferred_element_type=F32)

        @pl.when(kk == nk - 1)
        def _():
            y = DEEPNORM_ALPHA * r_ref[...] + acc_ref[...]
            if e_ref is not None:
                y = y + e_ref[...].astype(F32)
            mu = jnp.mean(y, axis=-1, keepdims=True)
            yc = y - mu
            var = jnp.mean(yc * yc, axis=-1, keepdims=True)
            out = yc * lax.rsqrt(var + LN_EPS) * g_ref[...] + b_ref[...]
            if split_rows is None:
                oa_ref[...] = out
                ob_ref[...] = out.astype(BF16)
            else:
                @pl.when(i < tiles_a)
                def _():
                    oa_ref[...] = out

                @pl.when(i >= tiles_a)
                def _():
                    ob_ref[...] = out

    row = lambda i, j: (i, 0)
    in_specs = [pl.BlockSpec((tm, tk), lambda i, j: (i, j)),
                pl.BlockSpec((tk, n), lambda i, j: (j, 0), pipeline_mode=pl.Buffered(1 if nk == 1 else 2)),
                pl.BlockSpec((tm, n), row)]
    args = [h, w_bf16, res]
    if has_extra:
        in_specs.append(pl.BlockSpec((tm, n), row))
        args.append(extra)
    in_specs += [pl.BlockSpec((1, n), lambda i, j: (0, 0))] * 2
    args += [gamma.reshape(1, n), beta.reshape(1, n)]
    if split_rows is None:
        out_shape = [jax.ShapeDtypeStruct((m, n), F32), jax.ShapeDtypeStruct((m, n), BF16)]
        out_specs = [pl.BlockSpec((tm, n), row), pl.BlockSpec((tm, n), row)]
    else:
        out_shape = [jax.ShapeDtypeStruct((split_rows, n), F32), jax.ShapeDtypeStruct((m - split_rows, n), F32)]
        out_specs = [pl.BlockSpec((tm, n), lambda i, j: (jnp.minimum(i, tiles_a - 1), 0)),
                     pl.BlockSpec((tm, n), lambda i, j: (jnp.maximum(i - tiles_a, 0), 0))]
    return pl.pallas_call(
        body,
        out_shape=out_shape,
        grid=(m // tm, nk),
        in_specs=in_specs,
        out_specs=out_specs,
        scratch_shapes=[pltpu.VMEM((tm, n), F32)],
        compiler_params=_cparams(("arbitrary", "arbitrary")),
        name=name,
    )(*args)


def _seq_masks(g, lg_shift):
    row = lax.broadcasted_iota(jnp.int32, (CHUNK, CHUNK), 0)
    col = lax.broadcasted_iota(jnp.int32, (CHUNK, CHUNK), 1)
    if g == 1:
        return col <= row, col < row
    same = (row >> lg_shift) == (col >> lg_shift)
    return same & (col <= row), same & (col < row)


def _per_seq_rows(vals, g, lg):
    if g == 1:
        return vals[0]
    return jnp.concatenate([jnp.broadcast_to(v, (lg, v.shape[1])) for v in vals], axis=0)


def _last_rows(col, g, lg):
    if g == 1:
        return col[:, CHUNK - 1:, :]
    return _per_seq_rows([col[0, (i + 1) * lg - 1:(i + 1) * lg, :] for i in range(g)], g, lg)[None]


def _split3(x):
    x1 = x.astype(BF16).astype(F32)
    r1 = x - x1
    x2 = r1.astype(BF16).astype(F32)
    x3 = (r1 - x2).astype(BF16).astype(F32)
    return x1, x2, x3


def _split2(x):
    hi = x.astype(BF16)
    return hi, (x - hi.astype(F32)).astype(BF16)


def _lanes3(col, lane_ids, first, other):
    c1, c2, c3 = _split3(col)
    x = jnp.where(lane_ids == first, c1, jnp.where(lane_ids == first + 1, c2,
                                                   jnp.where(lane_ids == first + 2, c3, 0.0)))
    return jnp.where((lane_ids >= other) & (lane_ids < other + 3), 1.0, x).astype(BF16)


def _cumsum_rows(col, tri_b, lane_ids):
    return jnp.sum(_bmm(tri_b, _lanes3(col, lane_ids, 0, LANE)), axis=-1, keepdims=True)


def _outer_sum(a_col, b_col, lane_ids):
    return _bmm_nt(_lanes3(a_col, lane_ids, 0, 3), _lanes3(b_col, lane_ids, 3, 0))


def _bmm2(a, b):
    ah, al = _split2(a)
    bb = b.astype(BF16)
    return _bmm(ah, bb) + _bmm(al, bb)


def _conv_rows(x, prev, w, g, lg):
    f = x.shape[1]
    xx = jnp.concatenate([prev.reshape(g, SUBLANE, f), x.reshape(g, lg, f)], axis=1)
    y = xx[:, SUBLANE:, :] * w[CONV_W - 1:CONV_W, :]
    for s in range(1, CONV_W):
        y = y + xx[:, SUBLANE - s:SUBLANE - s + lg, :] * w[CONV_W - 1 - s:CONV_W - s, :]
    return y.reshape(g * lg, f), xx[:, lg:, :].reshape(g * SUBLANE, f)


def _mlstm_body(r, g, q_ref, k_ref, v_ref, o_ref, gt_ref, gb_ref, nw_ref, c0_ref, n0_ref, m0_ref,
                y_ref, c_out, n_out, m_out, c_s, n_s, m_s):
    lg = CHUNK // g
    lg_shift = int(math.log2(lg))
    nch = r // CHUNK
    h = pl.program_id(1)
    t = pl.program_id(2)

    @pl.when(t == 0)
    def _():
        c_s[...] = c0_ref[:, 0]
        n_s[...] = n0_ref[:, 0]
        m_s[...] = jnp.broadcast_to(m0_ref[:, 0], m_s.shape)

    lane3 = lax.broadcasted_iota(jnp.int32, (nch, CHUNK, LANE), 2)
    causal, _ = _seq_masks(g, lg_shift)
    tri_b = jnp.broadcast_to(causal.astype(BF16), (nch, CHUNK, CHUNK))

    pre = gt_ref[...].reshape(nch, CHUNK, LANE) + gb_ref[...]
    capd = ML_GATE_CAP * jnp.tanh(pre * (1.0 / ML_GATE_CAP))
    logsig = jnp.minimum(capd, 0.0) - jnp.log1p(jnp.exp(-jnp.abs(capd)))
    ic = jnp.sum(jnp.where(lane3 == G_MLI + h, capd, 0.0), axis=2, keepdims=True)
    lf = jnp.sum(jnp.where(lane3 == G_MLF + h, logsig, 0.0), axis=2, keepdims=True)
    bc = _cumsum_rows(lf, tri_b, lane3)
    dm = jnp.where(causal, _outer_sum(bc, ic - bc, lane3), -jnp.inf)
    rmax = jnp.max(dm, axis=2, keepdims=True)

    if g == 1:
        mp = m_s[0][:, 0:1]
        m_prevs = []
        for c in range(nch):
            m_prevs.append(mp.reshape(1, 1, 1))
            mp = jnp.maximum(mp + bc[c, CHUNK - 1:, :], rmax[c, CHUNK - 1:, :])
        m_prev = jnp.concatenate(m_prevs, axis=0) if nch > 1 else m_prevs[0]
        m_s[0] = jnp.broadcast_to(mp, (1, LANE))
    else:
        m_prev = _per_seq_rows([m_s[i][:, 0:1] for i in range(g)], g, lg)[None]
    m_t = jnp.maximum(m_prev + bc, rmax)
    dmat = jnp.exp(dm - m_t)
    q3 = q_ref[...].reshape(nch, CHUNK, ML_DK) * (ML_DK ** -0.5)
    k3 = k_ref[...].reshape(nch, CHUNK, ML_DK)
    v3 = v_ref[...].reshape(nch, CHUNK, ML_DV)
    s = _bmm_nt(q3, k3) * dmat
    inter = jnp.exp(bc + m_prev - m_t)
    bc_end = _last_rows(bc, g, lg)
    m_new = _last_rows(m_t, g, lg)
    w_col = jnp.exp(ic - bc + bc_end - m_new)
    dec = jnp.exp(bc_end + m_prev - m_new)
    sv = _bmm(s, v3)
    ssum = jnp.sum(s, axis=2, keepdims=True)
    vw = v3 * w_col
    kw = k3 * w_col
    floor = jnp.exp(-m_t)

    def emit(rows, hh):
        hn = hh * lax.rsqrt(jnp.mean(hh * hh, axis=1, keepdims=True) + RMS_EPS) * nw_ref[...]
        y_ref[rows, :] = (_sigmoid(o_ref[rows, :]) * hn).astype(y_ref.dtype)

    if g == 1:
        upd = _bmm_tn(vw, k3)
        ksum = jnp.sum(kw, axis=1, keepdims=True)
        cc, nn = c_s[0], n_s[0]
        for c in range(nch):
            num = sv[c] + inter[c] * _dot_nt(q3[c], cc)
            den = ssum[c] + inter[c] * jnp.sum(q3[c] * nn, axis=1, keepdims=True)
            emit(slice(c * CHUNK, (c + 1) * CHUNK), num / jnp.maximum(jnp.abs(den), floor[c]))
            cc = dec[c] * cc + upd[c]
            nn = dec[c] * nn + ksum[c]
        c_s[0] = cc
        n_s[0] = nn
    else:
        rows_i = lax.broadcasted_iota(jnp.int32, (CHUNK, 1), 0)
        q2, k2 = q3[0], k3[0]
        qc = jnp.concatenate([_dot_nt(q2[i * lg:(i + 1) * lg], c_s[i]) for i in range(g)], axis=0)
        qn = jnp.concatenate([jnp.sum(q2[i * lg:(i + 1) * lg] * n_s[i], axis=1, keepdims=True)
                              for i in range(g)], axis=0)
        num = sv[0] + inter[0] * qc
        den = ssum[0] + inter[0] * qn
        emit(slice(0, CHUNK), num / jnp.maximum(jnp.abs(den), floor[0]))
        for i in range(g):
            dec_i = dec[0, i * lg:i * lg + 1, :]
            c_s[i] = dec_i * c_s[i] + _dot_tn(jnp.where((rows_i >> lg_shift) == i, vw[0], 0.0), k2)
            n_s[i] = dec_i * n_s[i] + jnp.sum(kw[0, i * lg:(i + 1) * lg], axis=0, keepdims=True)
            m_s[i] = jnp.broadcast_to(m_new[0, i * lg:i * lg + 1, :], (1, LANE))

    @pl.when(t == pl.num_programs(2) - 1)
    def _():
        c_out[:, 0] = c_s[...]
        n_out[:, 0] = n_s[...]
        m_out[:, 0] = m_s[:, :, 0:1]


def _gdn_body(r, g, q_ref, k_ref, v_ref, z_ref, gt_ref, cwq_ref, cwk_ref, cwv_ref, al_ref, dtb_ref,
              nw_ref, bq_ref, bk_ref, bv_ref, s0_ref, y_ref, s_out, tq_out, tk_out, tv_out,
              s_s, pq_s, pk_s, pv_s):
    lg = CHUNK // g
    lg_shift = int(math.log2(lg))
    lseq = r if g == 1 else lg
    nch = r // CHUNK
    h = pl.program_id(1)
    t = pl.program_id(2)

    @pl.when(t == 0)
    def _():
        s_s[...] = s0_ref[:, 0]
        pq_s[...] = bq_ref[...]
        pk_s[...] = bk_ref[...]
        pv_s[...] = bv_ref[...]

    qa, tq = _conv_rows(q_ref[...], pq_s[...], cwq_ref[...], g, lseq)
    ka, tk = _conv_rows(k_ref[...], pk_s[...], cwk_ref[...], g, lseq)
    va, tv = _conv_rows(v_ref[...], pv_s[...], cwv_ref[...], g, lseq)
    pq_s[...] = tq
    pk_s[...] = tk
    pv_s[...] = tv
    qa, ka, va = _silu(qa), _silu(ka), _silu(va)
    qn = qa * lax.rsqrt(jnp.sum(qa * qa, axis=1, keepdims=True) + L2_EPS) * (GDN_DK ** -0.5)
    kn = ka * lax.rsqrt(jnp.sum(ka * ka, axis=1, keepdims=True) + L2_EPS)

    lane3 = lax.broadcasted_iota(jnp.int32, (nch, CHUNK, LANE), 2)
    causal, strict = _seq_masks(g, lg_shift)
    tri_b = jnp.broadcast_to(causal.astype(BF16), (nch, CHUNK, CHUNK))
    eye = (lax.broadcasted_iota(jnp.int32, (CHUNK, CHUNK), 0)
           == lax.broadcasted_iota(jnp.int32, (CHUNK, CHUNK), 1)).astype(F32)

    q3 = qn.reshape(nch, CHUNK, GDN_DK)
    k3 = kn.reshape(nch, CHUNK, GDN_DK)
    v3 = va.reshape(nch, CHUNK, GDN_DV)
    gt = gt_ref[...].reshape(nch, CHUNK, LANE)
    beta = jnp.sum(jnp.where(lane3 == G_GDB + h, _sigmoid(gt), 0.0), axis=2, keepdims=True)
    gval = -jnp.exp(al_ref[...]) * _softplus(gt + dtb_ref[...])
    gc = _cumsum_rows(jnp.sum(jnp.where(lane3 == G_GDA + h, gval, 0.0), axis=2, keepdims=True),
                      tri_b, lane3)
    decay = jnp.exp(jnp.where(causal, _outer_sum(gc, -gc, lane3), -jnp.inf))
    p = jnp.where(strict, -(beta * _bmm_nt(k3, k3) * decay), 0.0)
    inv = eye + p
    ph, pl_ = _split2(p)
    for _ in range(1, lg_shift):
        ph, pl_ = _split2(_bmm(ph, ph) + (_bmm(ph, pl_) + _bmm(pl_, ph)))
        ih, il = _split2(inv)
        inv = inv + (_bmm(ih, ph) + (_bmm(ih, pl_) + _bmm(il, ph)))
    egc = jnp.exp(gc)
    u_pre = _bmm2(inv, beta * v3)
    w = _bmm2(inv, (beta * egc) * k3)
    qk = _bmm_nt(q3, k3) * decay
    q_dec = q3 * egc
    g_end = _last_rows(gc, g, lg)
    k_dec = k3 * jnp.exp(g_end - gc)

    def emit(rows, o):
        hn = o * lax.rsqrt(jnp.mean(o * o, axis=1, keepdims=True) + RMS_EPS) * nw_ref[...]
        y_ref[rows, :] = (hn * _silu(z_ref[rows, :])).astype(y_ref.dtype)

    if g == 1:
        st = s_s[0]
        for c in range(nch):
            u = u_pre[c] - _dot(w[c], st)
            emit(slice(c * CHUNK, (c + 1) * CHUNK), _dot(q_dec[c], st) + _dot(qk[c], u))
            st = jnp.exp(g_end[c]) * st + _dot_tn(k_dec[c], u)
        s_s[0] = st
    else:
        rows_i = lax.broadcasted_iota(jnp.int32, (CHUNK, 1), 0)
        us, os_ = [], []
        for i in range(g):
            sl = slice(i * lg, (i + 1) * lg)
            us.append(u_pre[0, sl] - _dot(w[0, sl], s_s[i]))
            os_.append(_dot(q_dec[0, sl], s_s[i]))
        u = jnp.concatenate(us, axis=0)
        emit(slice(0, CHUNK), jnp.concatenate(os_, axis=0) + _dot(qk[0], u))
        for i in range(g):
            kdi = jnp.where((rows_i >> lg_shift) == i, k_dec[0], 0.0)
            s_s[i] = jnp.exp(g_end[0, i * lg:i * lg + 1, :]) * s_s[i] + _dot_tn(kdi, u)

    @pl.when(t == pl.num_programs(2) - 1)
    def _():
        s_out[:, 0] = s_s[...]
        tq_out[...] = tq
        tk_out[...] = tk
        tv_out[...] = tv


def _rglru_body(r, g, x_ref, yg_ref, cw_ref, cb_ref, wa_ref, wx_ref, ba_ref, bx_ref, lam_ref,
                buf_ref, h0_ref, y_ref, h_out, tail_out, h_s, px_s):
    lseq = r if g == 1 else CHUNK // g
    t = pl.program_id(2)

    @pl.when(t == 0)
    def _():
        h_s[...] = h0_ref[:, 0]
        px_s[...] = buf_ref[...]

    xc, tail = _conv_rows(x_ref[...], px_s[...], cw_ref[...], g, lseq)
    px_s[...] = tail
    xc = xc + cb_ref[...]
    rg = _sigmoid(_dot(xc, wa_ref[0]) + ba_ref[...])
    ig = _sigmoid(_dot(xc, wx_ref[0]) + bx_ref[...])
    log_a = -LRU_C * rg * _softplus(-lam_ref[...])
    a = jnp.exp(log_a)
    u = jnp.sqrt(1.0 - jnp.exp(2.0 * log_a)) * (ig * xc)
    pos = lax.broadcasted_iota(jnp.int32, (r, 1), 0) & (SUBLANE - 1)
    sh = 1
    while sh < SUBLANE:
        a_sh = pltpu.roll(a, sh, 0)
        u_sh = pltpu.roll(u, sh, 0)
        msk = pos >= sh
        u = jnp.where(msk, a * u_sh + u, u)
        a = jnp.where(msk, a * a_sh, a)
        sh *= 2
    blocks = []
    for i in range(g):
        carry = h_s[i]
        for k in range(lseq // SUBLANE):
            rows = slice(i * lseq + k * SUBLANE, i * lseq + (k + 1) * SUBLANE)
            blk = u[rows] + a[rows] * carry
            blocks.append(blk)
            carry = blk[SUBLANE - 1:, :]
        h_s[i] = carry
    hh = jnp.concatenate(blocks, axis=0)
    yg = yg_ref[...]
    gelu = 0.5 * yg * (1.0 + jnp.tanh(math.sqrt(2.0 / math.pi) * (yg + 0.044715 * (yg * yg * yg))))
    y_ref[...] = (gelu * hh).astype(y_ref.dtype)

    @pl.when(t == pl.num_programs(2) - 1)
    def _():
        h_out[:, 0] = h_s[...]
        tail_out[...] = tail


class _Group:
    def __init__(self, n_seq, seq_len, row0):
        self.n_seq, self.seq_len, self.row0 = n_seq, seq_len, row0
        self.rows = n_seq * seq_len
        if seq_len >= 4 * CHUNK:
            self.r, self.g = 4 * CHUNK, 1
        elif seq_len >= CHUNK:
            self.r, self.g = CHUNK, 1
        else:
            self.r, self.g = CHUNK, CHUNK // seq_len
        self.s = self.g
        self.nb = n_seq // self.s
        self.nt = seq_len // self.r if self.g == 1 else 1
        assert seq_len >= CONV_W - 1 and row0 % self.r == 0 and self.rows % self.r == 0
        assert n_seq % self.s == 0 and self.r & (self.r - 1) == 0
        self.rb0 = row0 // self.r

    def rb(self, b, t):
        return self.rb0 + b * self.nt + t


def _mixer_call(body, grp, l, depth, m_total, in_specs, args, y_width, y_block, state_shapes, extra_outs,
                scratch, prev, name):
    s = grp.s
    out_shape = [jax.ShapeDtypeStruct((m_total, y_width), BF16)]
    out_specs = [pl.BlockSpec((grp.r, y_block), lambda b, h, t: (grp.rb(b, t), h))]
    for tail in state_shapes:
        out_shape.append(jax.ShapeDtypeStruct((depth, grp.n_seq, y_width // y_block) + tail, F32))
        out_specs.append(pl.BlockSpec((None, s, 1) + tail, lambda b, h, t: (l, b, h) + (0,) * len(tail)))
    for shp, spec in extra_outs:
        out_shape.append(shp)
        out_specs.append(spec)
    n_in = len(args)
    aliases = {}
    in_specs = list(in_specs)
    args = list(args)
    for j, arr in enumerate(prev):
        if arr is not None:
            aliases[len(args)] = j
            in_specs.append(pl.BlockSpec(memory_space=pl.ANY))
            args.append(arr)
    n_alias = len(args) - n_in

    def wrapped(*refs):
        body(*refs[:n_in], *refs[n_in + n_alias:])

    return pl.pallas_call(
        wrapped,
        out_shape=out_shape,
        grid=(grp.nb, y_width // y_block, grp.nt),
        in_specs=in_specs,
        out_specs=out_specs,
        scratch_shapes=scratch,
        input_output_aliases=aliases,
        compiler_params=_cparams(("arbitrary",) * 3),
        name=name,
    )(*args)


def _mlstm_call(grp, l, depth, m_total, hmain, gates, gate_bias, norm_w, c0, n0, m0, prev):
    r, s = grp.r, grp.s
    row = lambda off, w: pl.BlockSpec((r, w), lambda b, h, t: (grp.rb(b, t), off // w + h))
    st = lambda *tail: pl.BlockSpec((None, s, 1) + tail, lambda b, h, t: (l, b, h, 0, 0))
    in_specs = [row(C_MLQ, ML_DK), row(C_MLK, ML_DK), row(C_MLV, ML_DV), row(C_MLO, ML_DV),
                pl.BlockSpec((r, LANE), lambda b, h, t: (grp.rb(b, t), 0)),
                pl.BlockSpec((1, LANE), lambda b, h, t: (0, 0)),
                pl.BlockSpec((1, ML_DV), lambda b, h, t: (0, h)),
                st(ML_DV, ML_DK), st(1, ML_DK), st(1, 1)]
    scratch = [pltpu.VMEM((s, ML_DV, ML_DK), F32), pltpu.VMEM((s, 1, ML_DK), F32),
               pltpu.VMEM((s, 1, LANE), F32)]
    return _mixer_call(functools.partial(_mlstm_body, r, grp.g), grp, l, depth, m_total, in_specs,
                       [hmain, hmain, hmain, hmain, gates, gate_bias, norm_w, c0, n0, m0],
                       ML_HEADS * ML_DV, ML_DV, [(ML_DV, ML_DK), (1, ML_DK), (1, 1)], [], scratch, prev, "mlstm")


def _gdn_call(grp, l, depth, m_total, hmain, gates, conv_w, a_log_row, dt_bias_row, norm_w, bufpad, s0, prev):
    r, s = grp.r, grp.s
    row = lambda off, w: pl.BlockSpec((r, w), lambda b, h, t: (grp.rb(b, t), off // w + h))
    cw = lambda off, w: pl.BlockSpec((CONV_W, w), lambda b, h, t: (0, off // w + h))
    bf = lambda off, w: pl.BlockSpec((s * SUBLANE, w), lambda b, h, t: (b, off // w + h))
    one = pl.BlockSpec((1, LANE), lambda b, h, t: (0, 0))
    in_specs = [row(C_GDQ, GDN_DK), row(C_GDK, GDN_DK), row(C_GDV, GDN_DV), row(C_GDZ, GDN_DV),
                pl.BlockSpec((r, LANE), lambda b, h, t: (grp.rb(b, t), 1)),
                cw(0, GDN_DK), cw(GDN_QK, GDN_DK), cw(2 * GDN_QK, GDN_DV),
                one, one, pl.BlockSpec((1, GDN_DV), lambda b, h, t: (0, 0)),
                bf(0, GDN_DK), bf(GDN_QK, GDN_DK), bf(2 * GDN_QK, GDN_DV),
                pl.BlockSpec((None, s, 1, GDN_DK, GDN_DV), lambda b, h, t: (l, b, h, 0, 0))]
    nrow = grp.n_seq * SUBLANE
    tail = lambda w: (jax.ShapeDtypeStruct((nrow, GDN_HEADS * w), F32),
                      pl.BlockSpec((s * SUBLANE, w), lambda b, h, t: (b, h)))
    scratch = [pltpu.VMEM((s, GDN_DK, GDN_DV), F32),
               pltpu.VMEM((s * SUBLANE, GDN_DK), F32), pltpu.VMEM((s * SUBLANE, GDN_DK), F32),
               pltpu.VMEM((s * SUBLANE, GDN_DV), F32)]
    return _mixer_call(functools.partial(_gdn_body, r, grp.g), grp, l, depth, m_total, in_specs,
                       [hmain, hmain, hmain, hmain, gates, conv_w, conv_w, conv_w, a_log_row, dt_bias_row,
                        norm_w, bufpad, bufpad, bufpad, s0],
                       GDN_HEADS * GDN_DV, GDN_DV, [(GDN_DK, GDN_DV)],
                       [tail(GDN_DK), tail(GDN_DK), tail(GDN_DV)], scratch, prev, "gdn")


def _rglru_call(grp, l, depth, m_total, hmain, conv_w, conv_b, w_a, w_x, b_a, b_x, lam, bufpad, h0, prev):
    r, s = grp.r, grp.s
    w = LRU_BLOCK
    row = lambda off: pl.BlockSpec((r, w), lambda b, h, t: (grp.rb(b, t), off // w + h))
    vec = pl.BlockSpec((1, w), lambda b, h, t: (0, h))
    blk = pl.BlockSpec((1, w, w), lambda b, h, t: (h, 0, 0))
    in_specs = [row(C_RGX), row(C_RGY),
                pl.BlockSpec((CONV_W, w), lambda b, h, t: (0, h)), vec, blk, blk, vec, vec, vec,
                pl.BlockSpec((s * SUBLANE, w), lambda b, h, t: (b, h)),
                pl.BlockSpec((None, s, 1, 1, w), lambda b, h, t: (l, b, h, 0, 0))]
    tail = (jax.ShapeDtypeStruct((grp.n_seq * SUBLANE, LRU_WIDTH), F32),
            pl.BlockSpec((s * SUBLANE, w), lambda b, h, t: (b, h)))
    scratch = [pltpu.VMEM((s, 1, w), F32), pltpu.VMEM((s * SUBLANE, w), F32)]
    return _mixer_call(functools.partial(_rglru_body, r, grp.g), grp, l, depth, m_total, in_specs,
                       [hmain, hmain, conv_w, conv_b, w_a, w_x, b_a, b_x, lam, bufpad, h0],
                       LRU_WIDTH, w, [(1, w)], [tail], scratch, prev, "rglru")


def _pad_conv_state(buf):
    b, k, c = buf.shape
    return jnp.pad(buf.astype(F32), ((0, 0), (SUBLANE - k, 0), (0, 0))).reshape(b * SUBLANE, c)


def _unpad_conv_state(tail, n_seq):
    return tail.reshape(n_seq, SUBLANE, -1)[:, SUBLANE - (CONV_W - 1):, :]


R_I1, R_I2, R_W1, R_W2, R_P1, R_P2 = 0, 1, 2, 3, 4, 5
MOE_TM = 512
MOE_TD = 512


def _router_call(x, router, tm):
    m, d = x.shape
    n_exp = router.shape[1]
    wr = jnp.pad(router.astype(F32), ((0, 0), (0, LANE - n_exp)))

    def body(x_ref, w_ref, o_ref, cnt_ref, carry):
        i = pl.program_id(0)

        @pl.when(i == 0)
        def _():
            carry[...] = jnp.zeros_like(carry)

        xh, xl = _split2(x_ref[...])
        wh, wl = _split2(w_ref[...])
        logits = _dot(xh, wh) + (_dot(xh, wl) + _dot(xl, wh))
        lane = lax.broadcasted_iota(jnp.int32, logits.shape, 1)
        lg = jnp.where(lane < n_exp, logits, -jnp.inf)
        v1 = jnp.max(lg, axis=1, keepdims=True)
        i1 = jnp.min(jnp.where(lg == v1, lane, LANE), axis=1, keepdims=True)
        lg2 = jnp.where(lane == i1, -jnp.inf, lg)
        v2 = jnp.max(lg2, axis=1, keepdims=True)
        i2 = jnp.min(jnp.where(lg2 == v2, lane, LANE), axis=1, keepdims=True)
        e2 = jnp.exp(v2 - v1)
        w1 = 1.0 / (1.0 + e2)
        w2 = e2 / (1.0 + e2)
        sel = ((lane == i1) | (lane == i2)).astype(F32)
        below = (lax.broadcasted_iota(jnp.int32, (tm, tm), 1)
                 < lax.broadcasted_iota(jnp.int32, (tm, tm), 0)).astype(BF16)
        rank = jnp.dot(below, sel.astype(BF16), preferred_element_type=F32) + carry[0:1, :]
        p1 = jnp.sum(jnp.where(lane == i1, rank, 0.0), axis=1, keepdims=True)
        p2 = jnp.sum(jnp.where(lane == i2, rank, 0.0), axis=1, keepdims=True)
        rec = jnp.where(lane == R_I1, i1.astype(F32), jnp.where(lane == R_I2, i2.astype(F32), 0.0))
        rec = jnp.where(lane == R_W1, w1, jnp.where(lane == R_W2, w2, rec))
        o_ref[...] = jnp.where(lane == R_P1, p1, jnp.where(lane == R_P2, p2, rec))
        carry[...] = carry[...] + jnp.sum(sel, axis=0, keepdims=True)
        cnt_ref[...] = carry[...]

    return pl.pallas_call(
        body,
        out_shape=[jax.ShapeDtypeStruct((m, LANE), F32), jax.ShapeDtypeStruct((SUBLANE, LANE), F32)],
        grid=(m // tm,),
        in_specs=[pl.BlockSpec((tm, d), lambda i: (i, 0)), pl.BlockSpec((d, LANE), lambda i: (0, 0))],
        out_specs=[pl.BlockSpec((tm, LANE), lambda i: (i, 0)), pl.BlockSpec((SUBLANE, LANE), lambda i: (0, 0))],
        scratch_shapes=[pltpu.VMEM((SUBLANE, LANE), F32)],
        compiler_params=_cparams(("arbitrary",)),
        name="router",
    )(x, wr)


def _source_table(starts, e_arr, p_arr, m, ns):
    td = MOE_TD

    def body(start_ref, e_ref, p_ref, tbl_ref):
        i = pl.program_id(0)

        @pl.when(i == 0)
        def _():
            def clear(r, c):
                tbl_ref[r] = 0
                return c

            lax.fori_loop(0, ns, clear, 0, unroll=8)

        base = i * td

        def put(j, c):
            tbl_ref[start_ref[e_ref[0, j]] + p_ref[0, j]] = base + (j >> 1)
            return c

        lax.fori_loop(0, 2 * td, put, 0, unroll=8)

    smem = pl.BlockSpec((None, 1, 2 * td), lambda i, s: (i, 0, 0), memory_space=pltpu.SMEM)
    return pl.pallas_call(
        body,
        grid_spec=pltpu.PrefetchScalarGridSpec(
            num_scalar_prefetch=1, grid=(m // td,), in_specs=[smem, smem],
            out_specs=pl.BlockSpec(memory_space=pltpu.SMEM)),
        out_shape=jax.ShapeDtypeStruct((ns,), jnp.int32),
        compiler_params=_cparams(("arbitrary",)),
        name="moe_table",
    )(starts, e_arr, p_arr)


def _start_rows(copy_of, n_rows):
    def issue(r, c):
        copy_of(2 * r).start(priority=0)
        copy_of(2 * r + 1).start(priority=1)
        return c

    lax.fori_loop(0, n_rows // 2, issue, 0, unroll=4)


def _wait_rows(copy_of, n_rows):
    def drain(r, c):
        copy_of(r).wait()
        return c

    lax.fori_loop(0, n_rows, drain, 0, unroll=8)


def _dispatch(x, table, n_used):
    m, d = x.shape
    ns = table.shape[0]
    tm = MOE_TM
    nt = ns // tm

    def body(nu_ref, src_ref, nxt_ref, x_ref, o_ref, buf, sem):
        i = pl.program_id(0)
        slot = i & 1

        def rows(idx_ref, s):
            return lambda r: pltpu.make_async_copy(x_ref.at[pl.ds(idx_ref[0, r], 1)],
                                                   buf.at[s, pl.ds(r, 1)], sem.at[s])

        @pl.when(i == 0)
        def _():
            _start_rows(rows(src_ref, 0), tm)

        @pl.when(i + 1 < nu_ref[0])
        def _():
            _start_rows(rows(nxt_ref, 1 - slot), tm)

        @pl.when(i < nu_ref[0])
        def _():
            _wait_rows(rows(src_ref, slot), tm)
            o_ref[...] = buf[slot].astype(BF16)

    idx = lambda off: pl.BlockSpec((None, 1, tm), lambda i, nu: (jnp.minimum(i + off, nt - 1), 0, 0),
                                   memory_space=pltpu.SMEM)
    tbl = table.reshape(nt, 1, tm)
    return pl.pallas_call(
        body,
        grid_spec=pltpu.PrefetchScalarGridSpec(
            num_scalar_prefetch=1, grid=(nt,),
            in_specs=[idx(0), idx(1), pl.BlockSpec(memory_space=pl.ANY)],
            out_specs=pl.BlockSpec((tm, d), lambda i, nu: (jnp.minimum(i, nu[0] - 1), 0)),
            scratch_shapes=[pltpu.VMEM((2, tm, d), F32), pltpu.SemaphoreType.DMA((2,))]),
        out_shape=jax.ShapeDtypeStruct((ns, d), BF16),
        compiler_params=_cparams(("arbitrary",)),
        name="moe_dispatch",
    )(n_used, tbl, tbl, x)


def _grouped(rows, ws, tile_expert, tile_first, n_used, epilogue, out_dtype, n_cols, tn, name, w_buffers=2):
    ns, k = rows.shape
    nt = ns // MOE_TM
    used = lambda i, nu: jnp.minimum(i, nu[0] - 1)
    w_spec = pl.BlockSpec((None, k, tn), lambda n, i, te, tf, nu: (te[i], 0, n),
                          pipeline_mode=pl.Buffered(w_buffers))

    def body(te_ref, tf_ref, nu_ref, x_ref, *refs):
        w_refs, o_ref, wb_refs = refs[:len(ws)], refs[len(ws)], refs[len(ws) + 1:]
        i = pl.program_id(1)

        @pl.when(tf_ref[i] == 1)
        def _():
            for w, wb in zip(w_refs, wb_refs):
                wb[...] = w[...].astype(BF16)

        @pl.when(i < nu_ref[0])
        def _():
            xv = x_ref[...].astype(BF16)
            accs = [jnp.dot(xv, wb[...], preferred_element_type=F32) for wb in wb_refs]
            o_ref[...] = epilogue(accs).astype(o_ref.dtype)

    return pl.pallas_call(
        body,
        grid_spec=pltpu.PrefetchScalarGridSpec(
            num_scalar_prefetch=3, grid=(n_cols // tn, nt),
            in_specs=[pl.BlockSpec((MOE_TM, k), lambda n, i, te, tf, nu: (used(i, nu), 0))] + [w_spec] * len(ws),
            out_specs=pl.BlockSpec((MOE_TM, tn), lambda n, i, te, tf, nu: (used(i, nu), n)),
            scratch_shapes=[pltpu.VMEM((k, tn), BF16) for _ in ws]),
        out_shape=jax.ShapeDtypeStruct((ns, n_cols), out_dtype),
        compiler_params=_cparams(("arbitrary", "arbitrary")),
        name=name,
    )(tile_expert, tile_first, n_used, rows, *ws)


def _combine_ln(x, ys, starts, e_arr, p_arr, rec, ple, gamma, beta, split_rows):
    tm = MOE_TD
    m, n = x.shape
    n_tiles = m // tm
    tiles_a = None if split_rows is None else split_rows // tm

    def body(start_ref, es_ref, ps_ref, en_ref, pn_ref, ys_ref, x_ref, r_ref, e_ref, g_ref, b_ref,
             oa_ref, ob_ref, ybuf, sem):
        i = pl.program_id(0)
        slot = i & 1

        def rows(e_ref_, p_ref_, s):
            return lambda j: pltpu.make_async_copy(
                ys_ref.at[pl.ds(start_ref[e_ref_[0, j]] + p_ref_[0, j], 1)],
                ybuf.at[s, j & 1, pl.ds(j >> 1, 1)], sem.at[s])

        @pl.when(i == 0)
        def _():
            _start_rows(rows(es_ref, ps_ref, 0), 2 * tm)

        @pl.when(i + 1 < n_tiles)
        def _():
            _start_rows(rows(en_ref, pn_ref, 1 - slot), 2 * tm)

        _wait_rows(rows(es_ref, ps_ref, slot), 2 * tm)
        r = r_ref[...]
        lane = lax.broadcasted_iota(jnp.int32, r.shape, 1)
        w1 = jnp.sum(jnp.where(lane == R_W1, r, 0.0), axis=1, keepdims=True)
        w2 = jnp.sum(jnp.where(lane == R_W2, r, 0.0), axis=1, keepdims=True)
        y = DEEPNORM_ALPHA * x_ref[...] + (w1 * ybuf[slot, 0] + w2 * ybuf[slot, 1]) + e_ref[...]
        mu = jnp.mean(y, axis=-1, keepdims=True)
        yc = y - mu
        var = jnp.mean(yc * yc, axis=-1, keepdims=True)
        out = yc * lax.rsqrt(var + LN_EPS) * g_ref[...] + b_ref[...]
        if split_rows is None:
            oa_ref[...] = out
            ob_ref[...] = out.astype(BF16)
        else:
            @pl.when(i < tiles_a)
            def _():
                oa_ref[...] = out

            @pl.when(i >= tiles_a)
            def _():
                ob_ref[...] = out

    row = lambda i, s: (i, 0)
    vec = pl.BlockSpec((1, n), lambda i, s: (0, 0))
    smem = lambda off: pl.BlockSpec((None, 1, 2 * tm), lambda i, s: (jnp.minimum(i + off, n_tiles - 1), 0, 0),
                                    memory_space=pltpu.SMEM)
    if split_rows is None:
        out_shape = [jax.ShapeDtypeStruct((m, n), F32), jax.ShapeDtypeStruct((m, n), BF16)]
        out_specs = [pl.BlockSpec((tm, n), row), pl.BlockSpec((tm, n), row)]
    else:
        out_shape = [jax.ShapeDtypeStruct((split_rows, n), F32), jax.ShapeDtypeStruct((m - split_rows, n), F32)]
        out_specs = [pl.BlockSpec((tm, n), lambda i, s: (jnp.minimum(i, tiles_a - 1), 0)),
                     pl.BlockSpec((tm, n), lambda i, s: (jnp.maximum(i - tiles_a, 0), 0))]
    return pl.pallas_call(
        body,
        grid_spec=pltpu.PrefetchScalarGridSpec(
            num_scalar_prefetch=1, grid=(m // tm,),
            in_specs=[smem(0), smem(0), smem(1), smem(1), pl.BlockSpec(memory_space=pl.ANY),
                      pl.BlockSpec((tm, n), row), pl.BlockSpec((tm, LANE), row), pl.BlockSpec((tm, n), row), vec, vec],
            out_specs=out_specs,
            scratch_shapes=[pltpu.VMEM((2, 2, tm, n), F32), pltpu.SemaphoreType.DMA((2,))]),
        out_shape=out_shape,
        compiler_params=_cparams(("arbitrary",)),
        name="moe_combine_ln",
    )(starts, e_arr, p_arr, e_arr, p_arr, ys, x, rec, ple, gamma.reshape(1, n), beta.reshape(1, n))


def _moe_ln(x, ple, router, w1, w3, w2, j, gamma, beta, split_rows):
    m, d = x.shape
    _, ne, _, fe = w1.shape
    w1, w3, w2 = (w.reshape((-1,) + w.shape[2:]) for w in (w1, w3, w2))
    rec, counts = _router_call(x, router, 512)
    cnt = counts[0, :ne].astype(jnp.int32)
    tiles_e = (cnt + MOE_TM - 1) // MOE_TM
    tile_end = jnp.cumsum(tiles_e)
    tile_beg = tile_end - tiles_e
    n_used = tile_end[ne - 1]
    nt = (2 * m) // MOE_TM + ne
    ids = jnp.minimum(jnp.arange(nt, dtype=jnp.int32), n_used - 1)
    tile_expert = jnp.sum((ids[:, None] >= tile_end[None, :]).astype(jnp.int32), axis=1)
    tile_first = (jnp.arange(nt, dtype=jnp.int32) == tile_beg[tile_expert]).astype(jnp.int32)
    starts = (tile_beg * MOE_TM).astype(jnp.int32)
    slots = lambda a, b: jnp.stack([rec[:, a], rec[:, b]], axis=1).astype(jnp.int32).reshape(m // MOE_TD, 1, 2 * MOE_TD)
    e_arr, p_arr = slots(R_I1, R_I2), slots(R_P1, R_P2)
    ns = nt * MOE_TM

    nu = n_used.reshape(1)
    xs = _dispatch(x, _source_table(starts, e_arr, p_arr, m, ns), nu)
    tile_w = tile_expert + j * ne
    hmid = _grouped(xs, [w1, w3], tile_w, tile_first, nu, lambda a: _silu(a[0]) * a[1], BF16, fe, fe // 2,
                    "moe_up", w_buffers=1)
    ys = _grouped(hmid, [w2], tile_w, tile_first, nu, lambda a: a[0], F32, d, 1024, "moe_down")
    return _combine_ln(x, ys, starts, e_arr, p_arr, rec, ple, gamma, beta, split_rows)


def _run_group_mixers(grp, l, depth, m_total, hmain, gates, st, prm, prev):
    b = grp.n_seq
    ya, ml_c, ml_n, ml_m = _mlstm_call(
        grp, l, depth, m_total, hmain, gates, prm["gate_bias"], prm["ml_norm_w"],
        st["ml_C"].astype(F32), st["ml_n"].astype(F32).reshape(depth, b, ML_HEADS, 1, ML_DK),
        st["ml_m"].astype(F32).reshape(depth, b, ML_HEADS, 1, 1),
        [prev["ya"], prev["ml_C"], prev["ml_n"], prev["ml_m"]])
    yb, rg_h, rg_tail = _rglru_call(
        grp, l, depth, m_total, hmain, prm["rg_conv_w"], prm["rg_conv_b"], prm["rg_w_a"], prm["rg_w_x"],
        prm["rg_b_a"], prm["rg_b_x"], prm["rg_lambda"], _pad_conv_state(st["rg_conv"][l]),
        st["rg_h"].astype(F32).reshape(depth, b, LRU_BLOCKS, 1, LRU_BLOCK), [prev["yb"], prev["rg_h"]])
    yc, gd_s, tq, tk, tv = _gdn_call(
        grp, l, depth, m_total, hmain, gates, prm["gd_conv_w"], prm["gd_a_log"], prm["gd_dt_bias"],
        prm["gd_norm_w"], _pad_conv_state(st["gd_conv"][l]), st["gd_S"].astype(F32),
        [prev["yc"], prev["gd_S"]])
    out = dict(ya=ya, yb=yb, yc=yc, ml_C=ml_c, ml_n=ml_n, ml_m=ml_m, rg_h=rg_h, gd_S=gd_s)
    tails = dict(rg_conv=_unpad_conv_state(rg_tail, b),
                 gd_conv=jnp.concatenate([_unpad_conv_state(x, b) for x in (tq, tk, tv)], axis=-1))
    return out, tails


_STATE_KEYS = ("ml_C", "ml_n", "ml_m", "rg_h", "gd_S")


def kernel(x_prompt, x_sample, state_mlstm_C, state_mlstm_n, state_mlstm_m, state_rglru_h, state_rglru_conv, state_gdn_S, state_gdn_conv, p_prompt, p_sample, w_in, ml_b_i, ml_b_f, ml_norm_w, rg_conv_w, rg_conv_b, rg_w_a, rg_b_a, rg_w_x, rg_b_x, rg_lambda, gd_conv_w, gd_A_log, gd_dt_bias, gd_norm_w, w_up_mlstm, w_up_rglru, w_up_gdn, w_out, ln1_g, ln1_b, ffn_w1, ffn_w3, ffn_w2, moe_router, moe_w1, moe_w3, moe_w2, ple_w, ple_gate_w, ln2_g, ln2_b):
    bp, tp, d = x_prompt.shape
    bs, ts, _ = x_sample.shape
    depth = w_in.shape[0]
    gp = _Group(bp, tp, 0)
    gs = _Group(bs, ts, bp * tp)
    m = gp.rows + gs.rows
    pd = x_prompt.dtype

    zeros = lambda *shape: jnp.zeros(shape, F32)
    st_p = dict(ml_C=zeros(depth, bp, ML_HEADS, ML_DV, ML_DK), ml_n=zeros(depth, bp, ML_HEADS, ML_DK),
                ml_m=zeros(depth, bp, ML_HEADS), rg_h=zeros(depth, bp, LRU_WIDTH),
                rg_conv=zeros(depth, bp, CONV_W - 1, LRU_WIDTH),
                gd_S=zeros(depth, bp, GDN_HEADS, GDN_DK, GDN_DV),
                gd_conv=zeros(depth, bp, CONV_W - 1, GDN_CONV_DIM))
    st_s = dict(ml_C=state_mlstm_C, ml_n=state_mlstm_n, ml_m=state_mlstm_m, rg_h=state_rglru_h,
                rg_conv=state_rglru_conv, gd_S=state_gdn_S, gd_conv=state_gdn_conv)

    x = jnp.concatenate([x_prompt.reshape(gp.rows, d), x_sample.reshape(gs.rows, d)], axis=0).astype(F32)
    xb = x.astype(BF16)
    w_in_t = jnp.swapaxes(w_in, 1, 2)
    lane_row = lambda v, off: jnp.zeros((1, LANE), F32).at[0, off:off + v.shape[0]].set(v.astype(F32))
    acc_p = {k: None for k in _STATE_KEYS}
    acc_s = {k: None for k in _STATE_KEYS}
    tails_p, tails_s = [], []

    for l in range(depth):
        hmain = _in_proj(xb, w_in_t, l, 1024)
        gates = _gate_proj(xb, w_in_t, l, 1024)

        prm = dict(
            gate_bias=lane_row(ml_b_i[l], G_MLI) + lane_row(ml_b_f[l], G_MLF),
            ml_norm_w=ml_norm_w[l].astype(F32).reshape(1, -1),
            rg_conv_w=rg_conv_w[l].astype(F32), rg_conv_b=rg_conv_b[l].astype(F32).reshape(1, -1),
            rg_w_a=rg_w_a[l], rg_w_x=rg_w_x[l], rg_b_a=rg_b_a[l].astype(F32).reshape(1, -1),
            rg_b_x=rg_b_x[l].astype(F32).reshape(1, -1), rg_lambda=rg_lambda[l].astype(F32).reshape(1, -1),
            gd_conv_w=gd_conv_w[l].astype(F32), gd_a_log=lane_row(gd_A_log[l], G_GDA),
            gd_dt_bias=lane_row(gd_dt_bias[l], G_GDA), gd_norm_w=gd_norm_w[l].astype(F32).reshape(1, -1))
        out_p, tl_p = _run_group_mixers(gp, l, depth, m, hmain, gates, st_p, prm,
                                        dict(acc_p, ya=None, yb=None, yc=None))
        out_s, tl_s = _run_group_mixers(gs, l, depth, m, hmain, gates, st_s, prm,
                                        dict(acc_s, ya=out_p["ya"], yb=out_p["yb"], yc=out_p["yc"]))
        acc_p = {k: out_p[k] for k in _STATE_KEYS}
        acc_s = {k: out_s[k] for k in _STATE_KEYS}
        tails_p.append(tl_p)
        tails_s.append(tl_s)
        ya, yb, yc = out_s["ya"], out_s["yb"], out_s["yc"]

        tn, tm = 1024, 512
        up_spec = pl.BlockSpec((None, ya.shape[1], tn), lambda n, i: (l, 0, n))
        mg_spec = lambda br: pl.BlockSpec((tm, tn), lambda n, i: (i, (C_MG + br * d) // tn + n))
        merged = _proj([ya, yb, yc], [w_up_mlstm, w_up_rglru, w_up_gdn], [up_spec] * 3,
                       [hmain] * 3, [mg_spec(0), mg_spec(1), mg_spec(2)],
                       lambda accs, ex, ids: [_sigmoid(ex[0]) * accs[0] + _sigmoid(ex[1]) * accs[1]
                                              + _sigmoid(ex[2]) * accs[2]],
                       [jax.ShapeDtypeStruct((m, d), BF16)], d, tm, tn, "merge")[0]
        x, xb = _rowfull_ln(merged, w_out[l].astype(BF16), x, None, ln1_g[l].astype(F32), ln1_b[l].astype(F32),
                            512, d, "out_proj_ln")

        p_l = jnp.concatenate([p_prompt[l].reshape(gp.rows, -1), p_sample[l].reshape(gs.rows, -1)], axis=0)
        ple = _proj([xb, p_l], [ple_gate_w, ple_w],
                    [pl.BlockSpec((None, d, 1024), lambda n, i: (l, 0, n)),
                     pl.BlockSpec((None, P_DIM, 1024), lambda n, i: (l, 0, n))],
                    [], [], lambda accs, ex, ids: [_sigmoid(accs[0]) * accs[1]],
                    [jax.ShapeDtypeStruct((m, d), F32)], d, 1024, 1024, "ple")[0]
        j = l // 2
        split = gp.rows if l == depth - 1 else None
        g2, b2 = ln2_g[l].astype(F32), ln2_b[l].astype(F32)
        if l % 2 == 0:
            ff = ffn_w1.shape[2]
            hmid = _proj([xb], [ffn_w1, ffn_w3], [pl.BlockSpec((None, d, 512), lambda n, i: (j, 0, n))] * 2,
                         [], [], lambda accs, ex, ids: [_silu(accs[0]) * accs[1]],
                         [jax.ShapeDtypeStruct((m, ff), BF16)], ff, 1024, 512, "ffn_up", x_of_w=(0, 0))[0]
            x, xb = _rowfull_ln(hmid, ffn_w2[j].astype(BF16), x, ple, g2, b2, 256, ff, "ffn_down_ln",
                                split_rows=split)
        else:
            x, xb = _moe_ln(x, ple, moe_router[j], moe_w1, moe_w3, moe_w2, j, g2, b2, split)

    y_prompt = x.reshape(bp, tp, d).astype(pd)
    y_sample = xb.reshape(bs, ts, d).astype(x_sample.dtype)

    def finish(acc, tails, n_seq, dtypes):
        conv = {k: jnp.stack([tl[k] for tl in tails]) for k in ("rg_conv", "gd_conv")}
        vals = dict(ml_C=acc["ml_C"], ml_n=acc["ml_n"].reshape(depth, n_seq, ML_HEADS, ML_DK),
                    ml_m=acc["ml_m"].reshape(depth, n_seq, ML_HEADS),
                    rg_h=acc["rg_h"].reshape(depth, n_seq, LRU_WIDTH), rg_conv=conv["rg_conv"],
                    gd_S=acc["gd_S"], gd_conv=conv["gd_conv"])
        names = ("ml_C", "ml_n", "ml_m", "rg_h", "rg_conv", "gd_S", "gd_conv")
        return tuple(vals[k].astype(dtypes[k]) for k in names)

    return ((y_prompt, y_sample) + finish(acc_p, tails_p, bp, {k: pd for k in st_p})
            + finish(acc_s, tails_s, bs, {k: v.dtype for k, v in st_s.items()}))
```

```python
import functools
import math

import jax
import jax.numpy as jnp
from jax import lax
from jax.experimental import pallas as pl
from jax.experimental.pallas import tpu as pltpu

F32 = jnp.float32
BF16 = jnp.bfloat16

D_MODEL = 2048
DEPTH = 2
P_DIM = 256
CONV_W = 4
N_BRANCH = 3
ML_HEADS = 4
ML_DK = 128
ML_DV = 256
ML_GATE_CAP = 15.0
LRU_WIDTH = 1024
LRU_BLOCKS = 4
LRU_BLOCK = LRU_WIDTH // LRU_BLOCKS
LRU_C = 8.0
GDN_HEADS = 4
GDN_DK = 128
GDN_DV = 256
GDN_QK = GDN_HEADS * GDN_DK
GDN_CONV_DIM = 2 * GDN_QK + GDN_HEADS * GDN_DV
N_EXPERTS = 8
DEEPNORM_ALPHA = (2 * DEPTH) ** 0.25
LN_EPS = 1e-5
RMS_EPS = 1e-6
L2_EPS = 1e-6

LANE = 128
SUBLANE = 8
CHUNK = 128
VMEM_LIMIT = 56 * 1024 * 1024

C_MLQ, C_MLK, C_MLV, C_MLO = 0, 512, 1024, 2048
C_RGX, C_RGY = 3072, 4096
C_GDQ, C_GDK, C_GDV, C_GDZ = 5120, 5632, 6144, 7168
C_MG = 8192
N_MAIN = C_MG + N_BRANCH * D_MODEL
W_GATES_A, W_GATES_B = 2048, 7176
G_MLI, G_MLF, G_GDB, G_GDA = 0, 4, 8, 12
PROJ_TN = 1024


def _cparams(sem):
    return pltpu.CompilerParams(dimension_semantics=sem, vmem_limit_bytes=VMEM_LIMIT)


def _dot(a, b):
    return jnp.dot(a.astype(BF16), b.astype(BF16), preferred_element_type=F32)


def _dot_nt(a, b):
    return lax.dot_general(a.astype(BF16), b.astype(BF16), (((1,), (1,)), ((), ())),
                           preferred_element_type=F32)


def _dot_tn(a, b):
    return lax.dot_general(a.astype(BF16), b.astype(BF16), (((0,), (0,)), ((), ())),
                           preferred_element_type=F32)


def _bmm(a, b):
    return jnp.einsum('cij,cjk->cik', a.astype(BF16), b.astype(BF16), preferred_element_type=F32)


def _bmm_nt(a, b):
    return jnp.einsum('cik,cjk->cij', a.astype(BF16), b.astype(BF16), preferred_element_type=F32)


def _bmm_tn(a, b):
    return jnp.einsum('csi,csj->cij', a.astype(BF16), b.astype(BF16), preferred_element_type=F32)


def _sigmoid(x):
    return 1.0 / (1.0 + jnp.exp(-x))


def _silu(x):
    return x * _sigmoid(x)


def _softplus(x):
    return jnp.maximum(x, 0.0) + jnp.log1p(jnp.exp(-jnp.abs(x)))


def _proj(xs, ws, w_specs, extras, e_specs, epilogue, out_shapes, n_cols, tm, tn, name,
          lead_grid=(), x_of_w=None):
    m = xs[0].shape[0]
    nl = len(lead_grid)
    grid = tuple(lead_grid) + (n_cols // tn, m // tm)
    x_specs = [pl.BlockSpec((tm, x.shape[1]), lambda *g: (g[-1], 0)) for x in xs]
    o_specs = [pl.BlockSpec((tm, tn), lambda *g: (g[-1], g[-2]) if nl == 0 else
                            (g[-1], g[0] * (n_cols // tn) + g[-2])) for _ in out_shapes]
    scratch = [pltpu.VMEM(tuple(d for d in s.block_shape if d is not None), BF16) for s in w_specs]
    xw = tuple(range(len(ws))) if x_of_w is None else tuple(x_of_w)

    def body(*refs):
        nx, nw, ne, no = len(xs), len(ws), len(extras), len(out_shapes)
        x_refs = refs[:nx]
        w_refs = refs[nx:nx + nw]
        e_refs = refs[nx + nw:nx + nw + ne]
        o_refs = refs[nx + nw + ne:nx + nw + ne + no]
        wb_refs = refs[nx + nw + ne + no:]

        @pl.when(pl.program_id(nl + 1) == 0)
        def _():
            for w, wb in zip(w_refs, wb_refs):
                wb[...] = w[...].astype(BF16)

        xv = [x[...].astype(BF16) for x in x_refs]
        accs = [jnp.dot(xv[j], wb[...], preferred_element_type=F32) for j, wb in zip(xw, wb_refs)]
        outs = epilogue(accs, [e[...] for e in e_refs], [pl.program_id(i) for i in range(nl)])
        for o, v in zip(o_refs, outs):
            o[...] = v.astype(o.dtype)

    return pl.pallas_call(
        body,
        out_shape=out_shapes,
        grid=grid,
        in_specs=x_specs + list(w_specs) + list(e_specs),
        out_specs=o_specs,
        scratch_shapes=scratch,
        compiler_params=_cparams(("arbitrary",) * len(grid)),
        name=name,
    )(*xs, *ws, *extras)


def _in_proj(xb, wt, l, tm):
    m, d = xb.shape
    tn = PROJ_TN
    n_blocks = N_MAIN // tn
    first_a = W_GATES_A // tn
    first_b = (W_GATES_B - SUBLANE) // tn
    nxt = 2 * SUBLANE

    def body(x_ref, wm_ref, wn_ref, o_ref, wb_ref):
        n = pl.program_id(0)

        @pl.when(pl.program_id(1) == 0)
        def _():
            def shifted(s):
                wcat = jnp.concatenate([wm_ref[...], wn_ref[...]], axis=0)
                wb_ref[...] = wcat[s:s + tn, :].astype(BF16)

            @pl.when(n < first_a)
            def _():
                wb_ref[...] = wm_ref[...].astype(BF16)

            @pl.when((n >= first_a) & (n < first_b))
            def _():
                shifted(SUBLANE)

            @pl.when(n >= first_b)
            def _():
                shifted(2 * SUBLANE)

        o_ref[...] = lax.dot_general(x_ref[...], wb_ref[...], (((1,), (1,)), ((), ())),
                                     preferred_element_type=F32)

    return pl.pallas_call(
        body,
        out_shape=jax.ShapeDtypeStruct((m, N_MAIN), F32),
        grid=(n_blocks, m // tm),
        in_specs=[pl.BlockSpec((tm, d), lambda n, i: (i, 0)),
                  pl.BlockSpec((None, tn, d), lambda n, i: (l, n, 0)),
                  pl.BlockSpec((None, nxt, d), lambda n, i: (l, (n + 1) * (tn // nxt), 0))],
        out_specs=pl.BlockSpec((tm, tn), lambda n, i: (i, n)),
        scratch_shapes=[pltpu.VMEM((tn, d), BF16)],
        compiler_params=_cparams(("arbitrary", "arbitrary")),
        name="in_proj",
    )(xb, wt, wt)


def _gate_proj(xb, wt, l, tm):
    m, d = xb.shape
    blk_a, blk_b = W_GATES_A // LANE, W_GATES_B // LANE

    def body(x_ref, wa_ref, wb_ref, o_ref):
        w = jnp.concatenate([wa_ref[...], wb_ref[...]], axis=0).astype(BF16)
        o_ref[...] = lax.dot_general(x_ref[...], w, (((1,), (1,)), ((), ())), preferred_element_type=F32)

    return pl.pallas_call(
        body,
        out_shape=jax.ShapeDtypeStruct((m, 2 * LANE), F32),
        grid=(m // tm,),
        in_specs=[pl.BlockSpec((tm, d), lambda i: (i, 0)),
                  pl.BlockSpec((None, LANE, d), lambda i: (l, blk_a, 0)),
                  pl.BlockSpec((None, LANE, d), lambda i: (l, blk_b, 0))],
        out_specs=pl.BlockSpec((tm, 2 * LANE), lambda i: (i, 0)),
        compiler_params=_cparams(("arbitrary",)),
        name="gate_proj",
    )(xb, wt, wt)


def _rowfull_ln(h, w_bf16, res, extra, gamma, beta, tm, tk, name, split_rows=None):
    m, k = h.shape
    n = w_bf16.shape[1]
    nk = k // tk
    has_extra = extra is not None
    n_in = 6 if has_extra else 5
    tiles_a = None if split_rows is None else split_rows // tm

    def body(*refs):
        h_ref, w_ref, r_ref = refs[:3]
        e_ref = refs[3] if has_extra else None
        g_ref, b_ref = refs[n_in - 2:n_in]
        oa_ref, ob_ref, acc_ref = refs[n_in:]
        i = pl.program_id(0)
        kk = pl.program_id(1)

        @pl.when(kk == 0)
        def _():
            acc_ref[...] = jnp.zeros_like(acc_ref)

        acc_ref[...] += jnp.dot(h_ref[...], w_ref[...], preferred_element_type=F32)

        @pl.when(kk == nk - 1)
        def _():
            y = DEEPNORM_ALPHA * r_ref[...] + acc_ref[...]
            if e_ref is not None:
                y = y + e_ref[...].astype(F32)
            mu = jnp.mean(y, axis=-1, keepdims=True)
            yc = y - mu
            var = jnp.mean(yc * yc, axis=-1, keepdims=True)
            out = yc * lax.rsqrt(var + LN_EPS) * g_ref[...] + b_ref[...]
            if split_rows is None:
                oa_ref[...] = out
                ob_ref[...] = out.astype(BF16)
            else:
                @pl.when(i < tiles_a)
                def _():
                    oa_ref[...] = out

                @pl.when(i >= tiles_a)
                def _():
                    ob_ref[...] = out

    row = lambda i, j: (i, 0)
    in_specs = [pl.BlockSpec((tm, tk), lambda i, j: (i, j)),
                pl.BlockSpec((tk, n), lambda i, j: (j, 0), pipeline_mode=pl.Buffered(1 if nk == 1 else 2)),
                pl.BlockSpec((tm, n), row)]
    args = [h, w_bf16, res]
    if has_extra:
        in_specs.append(pl.BlockSpec((tm, n), row))
        args.append(extra)
    in_specs += [pl.BlockSpec((1, n), lambda i, j: (0, 0))] * 2
    args += [gamma.reshape(1, n), beta.reshape(1, n)]
    if split_rows is None:
        out_shape = [jax.ShapeDtypeStruct((m, n), F32), jax.ShapeDtypeStruct((m, n), BF16)]
        out_specs = [pl.BlockSpec((tm, n), row), pl.BlockSpec((tm, n), row)]
    else:
        out_shape = [jax.ShapeDtypeStruct((split_rows, n), F32), jax.ShapeDtypeStruct((m - split_rows, n), F32)]
        out_specs = [pl.BlockSpec((tm, n), lambda i, j: (jnp.minimum(i, tiles_a - 1), 0)),
                     pl.BlockSpec((tm, n), lambda i, j: (jnp.maximum(i - tiles_a, 0), 0))]
    return pl.pallas_call(
        body,
        out_shape=out_shape,
        grid=(m // tm, nk),
        in_specs=in_specs,
        out_specs=out_specs,
        scratch_shapes=[pltpu.VMEM((tm, n), F32)],
        compiler_params=_cparams(("arbitrary", "arbitrary")),
        name=name,
    )(*args)


def _seq_masks(g, lg_shift):
    row = lax.broadcasted_iota(jnp.int32, (CHUNK, CHUNK), 0)
    col = lax.broadcasted_iota(jnp.int32, (CHUNK, CHUNK), 1)
    if g == 1:
        return col <= row, col < row
    same = (row >> lg_shift) == (col >> lg_shift)
    return same & (col <= row), same & (col < row)


def _per_seq_rows(vals, g, lg):
    if g == 1:
        return vals[0]
    return jnp.concatenate([jnp.broadcast_to(v, (lg, v.shape[1])) for v in vals], axis=0)


def _last_rows(col, g, lg):
    if g == 1:
        return col[:, CHUNK - 1:, :]
    return _per_seq_rows([col[0, (i + 1) * lg - 1:(i + 1) * lg, :] for i in range(g)], g, lg)[None]


def _split3(x):
    x1 = x.astype(BF16).astype(F32)
    r1 = x - x1
    x2 = r1.astype(BF16).astype(F32)
    x3 = (r1 - x2).astype(BF16).astype(F32)
    return x1, x2, x3


def _split2(x):
    hi = x.astype(BF16)
    return hi, (x - hi.astype(F32)).astype(BF16)


def _lanes3(col, lane_ids, first, other):
    c1, c2, c3 = _split3(col)
    x = jnp.where(lane_ids == first, c1, jnp.where(lane_ids == first + 1, c2,
                                                   jnp.where(lane_ids == first + 2, c3, 0.0)))
    return jnp.where((lane_ids >= other) & (lane_ids < other + 3), 1.0, x).astype(BF16)


def _cumsum_rows(col, tri_b, lane_ids):
    return jnp.sum(_bmm(tri_b, _lanes3(col, lane_ids, 0, LANE)), axis=-1, keepdims=True)


def _outer_sum(a_col, b_col, lane_ids):
    return _bmm_nt(_lanes3(a_col, lane_ids, 0, 3), _lanes3(b_col, lane_ids, 3, 0))


def _bmm2(a, b):
    ah, al = _split2(a)
    bb = b.astype(BF16)
    return _bmm(ah, bb) + _bmm(al, bb)


def _conv_rows(x, prev, w, g, lg):
    f = x.shape[1]
    xx = jnp.concatenate([prev.reshape(g, SUBLANE, f), x.reshape(g, lg, f)], axis=1)
    y = xx[:, SUBLANE:, :] * w[CONV_W - 1:CONV_W, :]
    for s in range(1, CONV_W):
        y = y + xx[:, SUBLANE - s:SUBLANE - s + lg, :] * w[CONV_W - 1 - s:CONV_W - s, :]
    return y.reshape(g * lg, f), xx[:, lg:, :].reshape(g * SUBLANE, f)


def _mlstm_body(r, g, q_ref, k_ref, v_ref, o_ref, gt_ref, gb_ref, nw_ref, c0_ref, n0_ref, m0_ref,
                y_ref, c_out, n_out, m_out, c_s, n_s, m_s):
    lg = CHUNK // g
    lg_shift = int(math.log2(lg))
    nch = r // CHUNK
    h = pl.program_id(1)
    t = pl.program_id(2)

    @pl.when(t == 0)
    def _():
        c_s[...] = c0_ref[:, 0]
        n_s[...] = n0_ref[:, 0]
        m_s[...] = jnp.broadcast_to(m0_ref[:, 0], m_s.shape)

    lane3 = lax.broadcasted_iota(jnp.int32, (nch, CHUNK, LANE), 2)
    causal, _ = _seq_masks(g, lg_shift)
    tri_b = jnp.broadcast_to(causal.astype(BF16), (nch, CHUNK, CHUNK))

    pre = gt_ref[...].reshape(nch, CHUNK, LANE) + gb_ref[...]
    capd = ML_GATE_CAP * jnp.tanh(pre * (1.0 / ML_GATE_CAP))
    logsig = jnp.minimum(capd, 0.0) - jnp.log1p(jnp.exp(-jnp.abs(capd)))
    ic = jnp.sum(jnp.where(lane3 == G_MLI + h, capd, 0.0), axis=2, keepdims=True)
    lf = jnp.sum(jnp.where(lane3 == G_MLF + h, logsig, 0.0), axis=2, keepdims=True)
    bc = _cumsum_rows(lf, tri_b, lane3)
    dm = jnp.where(causal, _outer_sum(bc, ic - bc, lane3), -jnp.inf)
    rmax = jnp.max(dm, axis=2, keepdims=True)

    if g == 1:
        mp = m_s[0][:, 0:1]
        m_prevs = []
        for c in range(nch):
            m_prevs.append(mp.reshape(1, 1, 1))
            mp = jnp.maximum(mp + bc[c, CHUNK - 1:, :], rmax[c, CHUNK - 1:, :])
        m_prev = jnp.concatenate(m_prevs, axis=0) if nch > 1 else m_prevs[0]
        m_s[0] = jnp.broadcast_to(mp, (1, LANE))
    else:
        m_prev = _per_seq_rows([m_s[i][:, 0:1] for i in range(g)], g, lg)[None]
    m_t = jnp.maximum(m_prev + bc, rmax)
    dmat = jnp.exp(dm - m_t)
    q3 = q_ref[...].reshape(nch, CHUNK, ML_DK) * (ML_DK ** -0.5)
    k3 = k_ref[...].reshape(nch, CHUNK, ML_DK)
    v3 = v_ref[...].reshape(nch, CHUNK, ML_DV)
    s = _bmm_nt(q3, k3) * dmat
    inter = jnp.exp(bc + m_prev - m_t)
    bc_end = _last_rows(bc, g, lg)
    m_new = _last_rows(m_t, g, lg)
    w_col = jnp.exp(ic - bc + bc_end - m_new)
    dec = jnp.exp(bc_end + m_prev - m_new)
    sv = _bmm(s, v3)
    ssum = jnp.sum(s, axis=2, keepdims=True)
    vw = v3 * w_col
    kw = k3 * w_col
    floor = jnp.exp(-m_t)

    def emit(rows, hh):
        hn = hh * lax.rsqrt(jnp.mean(hh * hh, axis=1, keepdims=True) + RMS_EPS) * nw_ref[...]
        y_ref[rows, :] = (_sigmoid(o_ref[rows, :]) * hn).astype(y_ref.dtype)

    if g == 1:
        upd = _bmm_tn(vw, k3)
        ksum = jnp.sum(kw, axis=1, keepdims=True)
        cc, nn = c_s[0], n_s[0]
        for c in range(nch):
            num = sv[c] + inter[c] * _dot_nt(q3[c], cc)
            den = ssum[c] + inter[c] * jnp.sum(q3[c] * nn, axis=1, keepdims=True)
            emit(slice(c * CHUNK, (c + 1) * CHUNK), num / jnp.maximum(jnp.abs(den), floor[c]))
            cc = dec[c] * cc + upd[c]
            nn = dec[c] * nn + ksum[c]
        c_s[0] = cc
        n_s[0] = nn
    else:
        rows_i = lax.broadcasted_iota(jnp.int32, (CHUNK, 1), 0)
        q2, k2 = q3[0], k3[0]
        qc = jnp.concatenate([_dot_nt(q2[i * lg:(i + 1) * lg], c_s[i]) for i in range(g)], axis=0)
        qn = jnp.concatenate([jnp.sum(q2[i * lg:(i + 1) * lg] * n_s[i], axis=1, keepdims=True)
                              for i in range(g)], axis=0)
        num = sv[0] + inter[0] * qc
        den = ssum[0] + inter[0] * qn
        emit(slice(0, CHUNK), num / jnp.maximum(jnp.abs(den), floor[0]))
        for i in range(g):
            dec_i = dec[0, i * lg:i * lg + 1, :]
            c_s[i] = dec_i * c_s[i] + _dot_tn(jnp.where((rows_i >> lg_shift) == i, vw[0], 0.0), k2)
            n_s[i] = dec_i * n_s[i] + jnp.sum(kw[0, i * lg:(i + 1) * lg], axis=0, keepdims=True)
            m_s[i] = jnp.broadcast_to(m_new[0, i * lg:i * lg + 1, :], (1, LANE))

    @pl.when(t == pl.num_programs(2) - 1)
    def _():
        c_out[:, 0] = c_s[...]
        n_out[:, 0] = n_s[...]
        m_out[:, 0] = m_s[:, :, 0:1]


def _gdn_body(r, g, q_ref, k_ref, v_ref, z_ref, gt_ref, cwq_ref, cwk_ref, cwv_ref, al_ref, dtb_ref,
              nw_ref, bq_ref, bk_ref, bv_ref, s0_ref, y_ref, s_out, tq_out, tk_out, tv_out,
              s_s, pq_s, pk_s, pv_s):
    lg = CHUNK // g
    lg_shift = int(math.log2(lg))
    lseq = r if g == 1 else lg
    nch = r // CHUNK
    h = pl.program_id(1)
    t = pl.program_id(2)

    @pl.when(t == 0)
    def _():
        s_s[...] = s0_ref[:, 0]
        pq_s[...] = bq_ref[...]
        pk_s[...] = bk_ref[...]
        pv_s[...] = bv_ref[...]

    qa, tq = _conv_rows(q_ref[...], pq_s[...], cwq_ref[...], g, lseq)
    ka, tk = _conv_rows(k_ref[...], pk_s[...], cwk_ref[...], g, lseq)
    va, tv = _conv_rows(v_ref[...], pv_s[...], cwv_ref[...], g, lseq)
    pq_s[...] = tq
    pk_s[...] = tk
    pv_s[...] = tv
    qa, ka, va = _silu(qa), _silu(ka), _silu(va)
    qn = qa * lax.rsqrt(jnp.sum(qa * qa, axis=1, keepdims=True) + L2_EPS) * (GDN_DK ** -0.5)
    kn = ka * lax.rsqrt(jnp.sum(ka * ka, axis=1, keepdims=True) + L2_EPS)

    lane3 = lax.broadcasted_iota(jnp.int32, (nch, CHUNK, LANE), 2)
    causal, strict = _seq_masks(g, lg_shift)
    tri_b = jnp.broadcast_to(causal.astype(BF16), (nch, CHUNK, CHUNK))
    eye = (lax.broadcasted_iota(jnp.int32, (CHUNK, CHUNK), 0)
           == lax.broadcasted_iota(jnp.int32, (CHUNK, CHUNK), 1)).astype(F32)

    q3 = qn.reshape(nch, CHUNK, GDN_DK)
    k3 = kn.reshape(nch, CHUNK, GDN_DK)
    v3 = va.reshape(nch, CHUNK, GDN_DV)
    gt = gt_ref[...].reshape(nch, CHUNK, LANE)
    beta = jnp.sum(jnp.where(lane3 == G_GDB + h, _sigmoid(gt), 0.0), axis=2, keepdims=True)
    gval = -jnp.exp(al_ref[...]) * _softplus(gt + dtb_ref[...])
    gc = _cumsum_rows(jnp.sum(jnp.where(lane3 == G_GDA + h, gval, 0.0), axis=2, keepdims=True),
                      tri_b, lane3)
    decay = jnp.exp(jnp.where(causal, _outer_sum(gc, -gc, lane3), -jnp.inf))
    p = jnp.where(strict, -(beta * _bmm_nt(k3, k3) * decay), 0.0)
    inv = eye + p
    ph, pl_ = _split2(p)
    for _ in range(1, lg_shift):
        ph, pl_ = _split2(_bmm(ph, ph) + (_bmm(ph, pl_) + _bmm(pl_, ph)))
        ih, il = _split2(inv)
        inv = inv + (_bmm(ih, ph) + (_bmm(ih, pl_) + _bmm(il, ph)))
    egc = jnp.exp(gc)
    u_pre = _bmm2(inv, beta * v3)
    w = _bmm2(inv, (beta * egc) * k3)
    qk = _bmm_nt(q3, k3) * decay
    q_dec = q3 * egc
    g_end = _last_rows(gc, g, lg)
    k_dec = k3 * jnp.exp(g_end - gc)

    def emit(rows, o):
        hn = o * lax.rsqrt(jnp.mean(o * o, axis=1, keepdims=True) + RMS_EPS) * nw_ref[...]
        y_ref[rows, :] = (hn * _silu(z_ref[rows, :])).astype(y_ref.dtype)

    if g == 1:
        st = s_s[0]
        for c in range(nch):
            u = u_pre[c] - _dot(w[c], st)
            emit(slice(c * CHUNK, (c + 1) * CHUNK), _dot(q_dec[c], st) + _dot(qk[c], u))
            st = jnp.exp(g_end[c]) * st + _dot_tn(k_dec[c], u)
        s_s[0] = st
    else:
        rows_i = lax.broadcasted_iota(jnp.int32, (CHUNK, 1), 0)
        us, os_ = [], []
        for i in range(g):
            sl = slice(i * lg, (i + 1) * lg)
            us.append(u_pre[0, sl] - _dot(w[0, sl], s_s[i]))
            os_.append(_dot(q_dec[0, sl], s_s[i]))
        u = jnp.concatenate(us, axis=0)
        emit(slice(0, CHUNK), jnp.concatenate(os_, axis=0) + _dot(qk[0], u))
        for i in range(g):
            kdi = jnp.where((rows_i >> lg_shift) == i, k_dec[0], 0.0)
            s_s[i] = jnp.exp(g_end[0, i * lg:i * lg + 1, :]) * s_s[i] + _dot_tn(kdi, u)

    @pl.when(t == pl.num_programs(2) - 1)
    def _():
        s_out[:, 0] = s_s[...]
        tq_out[...] = tq
        tk_out[...] = tk
        tv_out[...] = tv


def _rglru_body(r, g, x_ref, yg_ref, cw_ref, cb_ref, wa_ref, wx_ref, ba_ref, bx_ref, lam_ref,
                buf_ref, h0_ref, y_ref, h_out, tail_out, h_s, px_s):
    lseq = r if g == 1 else CHUNK // g
    t = pl.program_id(2)

    @pl.when(t == 0)
    def _():
        h_s[...] = h0_ref[:, 0]
        px_s[...] = buf_ref[...]

    xc, tail = _conv_rows(x_ref[...], px_s[...], cw_ref[...], g, lseq)
    px_s[...] = tail
    xc = xc + cb_ref[...]
    rg = _sigmoid(_dot(xc, wa_ref[0]) + ba_ref[...])
    ig = _sigmoid(_dot(xc, wx_ref[0]) + bx_ref[...])
    log_a = -LRU_C * rg * _softplus(-lam_ref[...])
    a = jnp.exp(log_a)
    u = jnp.sqrt(1.0 - jnp.exp(2.0 * log_a)) * (ig * xc)
    pos = lax.broadcasted_iota(jnp.int32, (r, 1), 0) & (SUBLANE - 1)
    sh = 1
    while sh < SUBLANE:
        a_sh = pltpu.roll(a, sh, 0)
        u_sh = pltpu.roll(u, sh, 0)
        msk = pos >= sh
        u = jnp.where(msk, a * u_sh + u, u)
        a = jnp.where(msk, a * a_sh, a)
        sh *= 2
    blocks = []
    for i in range(g):
        carry = h_s[i]
        for k in range(lseq // SUBLANE):
            rows = slice(i * lseq + k * SUBLANE, i * lseq + (k + 1) * SUBLANE)
            blk = u[rows] + a[rows] * carry
            blocks.append(blk)
            carry = blk[SUBLANE - 1:, :]
        h_s[i] = carry
    hh = jnp.concatenate(blocks, axis=0)
    yg = yg_ref[...]
    gelu = 0.5 * yg * (1.0 + jnp.tanh(math.sqrt(2.0 / math.pi) * (yg + 0.044715 * (yg * yg * yg))))
    y_ref[...] = (gelu * hh).astype(y_ref.dtype)

    @pl.when(t == pl.num_programs(2) - 1)
    def _():
        h_out[:, 0] = h_s[...]
        tail_out[...] = tail


class _Group:
    def __init__(self, n_seq, seq_len, row0):
        self.n_seq, self.seq_len, self.row0 = n_seq, seq_len, row0
        self.rows = n_seq * seq_len
        if seq_len >= 4 * CHUNK:
            self.r, self.g = 4 * CHUNK, 1
        elif seq_len >= CHUNK:
            self.r, self.g = CHUNK, 1
        else:
            self.r, self.g = CHUNK, CHUNK // seq_len
        self.s = self.g
        self.nb = n_seq // self.s
        self.nt = seq_len // self.r if self.g == 1 else 1
        assert seq_len >= CONV_W - 1 and row0 % self.r == 0 and self.rows % self.r == 0
        assert n_seq % self.s == 0 and self.r & (self.r - 1) == 0
        self.rb0 = row0 // self.r

    def rb(self, b, t):
        return self.rb0 + b * self.nt + t


def _mixer_call(body, grp, l, depth, m_total, in_specs, args, y_width, y_block, state_shapes, extra_outs,
                scratch, prev, name):
    s = grp.s
    out_shape = [jax.ShapeDtypeStruct((m_total, y_width), BF16)]
    out_specs = [pl.BlockSpec((grp.r, y_block), lambda b, h, t: (grp.rb(b, t), h))]
    for tail in state_shapes:
        out_shape.append(jax.ShapeDtypeStruct((depth, grp.n_seq, y_width // y_block) + tail, F32))
        out_specs.append(pl.BlockSpec((None, s, 1) + tail, lambda b, h, t: (l, b, h) + (0,) * len(tail)))
    for shp, spec in extra_outs:
        out_shape.append(shp)
        out_specs.append(spec)
    n_in = len(args)
    aliases = {}
    in_specs = list(in_specs)
    args = list(args)
    for j, arr in enumerate(prev):
        if arr is not None:
            aliases[len(args)] = j
            in_specs.append(pl.BlockSpec(memory_space=pl.ANY))
            args.append(arr)
    n_alias = len(args) - n_in

    def wrapped(*refs):
        body(*refs[:n_in], *refs[n_in + n_alias:])

    return pl.pallas_call(
        wrapped,
        out_shape=out_shape,
        grid=(grp.nb, y_width // y_block, grp.nt),
        in_specs=in_specs,
        out_specs=out_specs,
        scratch_shapes=scratch,
        input_output_aliases=aliases,
        compiler_params=_cparams(("arbitrary",) * 3),
        name=name,
    )(*args)


def _mlstm_call(grp, l, depth, m_total, hmain, gates, gate_bias, norm_w, c0, n0, m0, prev):
    r, s = grp.r, grp.s
    row = lambda off, w: pl.BlockSpec((r, w), lambda b, h, t: (grp.rb(b, t), off // w + h))
    st = lambda *tail: pl.BlockSpec((None, s, 1) + tail, lambda b, h, t: (l, b, h, 0, 0))
    in_specs = [row(C_MLQ, ML_DK), row(C_MLK, ML_DK), row(C_MLV, ML_DV), row(C_MLO, ML_DV),
                pl.BlockSpec((r, LANE), lambda b, h, t: (grp.rb(b, t), 0)),
                pl.BlockSpec((1, LANE), lambda b, h, t: (0, 0)),
                pl.BlockSpec((1, ML_DV), lambda b, h, t: (0, h)),
                st(ML_DV, ML_DK), st(1, ML_DK), st(1, 1)]
    scratch = [pltpu.VMEM((s, ML_DV, ML_DK), F32), pltpu.VMEM((s, 1, ML_DK), F32),
               pltpu.VMEM((s, 1, LANE), F32)]
    return _mixer_call(functools.partial(_mlstm_body, r, grp.g), grp, l, depth, m_total, in_specs,
                       [hmain, hmain, hmain, hmain, gates, gate_bias, norm_w, c0, n0, m0],
                       ML_HEADS * ML_DV, ML_DV, [(ML_DV, ML_DK), (1, ML_DK), (1, 1)], [], scratch, prev, "mlstm")


def _gdn_call(grp, l, depth, m_total, hmain, gates, conv_w, a_log_row, dt_bias_row, norm_w, bufpad, s0, prev):
    r, s = grp.r, grp.s
    row = lambda off, w: pl.BlockSpec((r, w), lambda b, h, t: (grp.rb(b, t), off // w + h))
    cw = lambda off, w: pl.BlockSpec((CONV_W, w), lambda b, h, t: (0, off // w + h))
    bf = lambda off, w: pl.BlockSpec((s * SUBLANE, w), lambda b, h, t: (b, off // w + h))
    one = pl.BlockSpec((1, LANE), lambda b, h, t: (0, 0))
    in_specs = [row(C_GDQ, GDN_DK), row(C_GDK, GDN_DK), row(C_GDV, GDN_DV), row(C_GDZ, GDN_DV),
                pl.BlockSpec((r, LANE), lambda b, h, t: (grp.rb(b, t), 1)),
                cw(0, GDN_DK), cw(GDN_QK, GDN_DK), cw(2 * GDN_QK, GDN_DV),
                one, one, pl.BlockSpec((1, GDN_DV), lambda b, h, t: (0, 0)),
                bf(0, GDN_DK), bf(GDN_QK, GDN_DK), bf(2 * GDN_QK, GDN_DV),
                pl.BlockSpec((None, s, 1, GDN_DK, GDN_DV), lambda b, h, t: (l, b, h, 0, 0))]
    nrow = grp.n_seq * SUBLANE
    tail = lambda w: (jax.ShapeDtypeStruct((nrow, GDN_HEADS * w), F32),
                      pl.BlockSpec((s * SUBLANE, w), lambda b, h, t: (b, h)))
    scratch = [pltpu.VMEM((s, GDN_DK, GDN_DV), F32),
               pltpu.VMEM((s * SUBLANE, GDN_DK), F32), pltpu.VMEM((s * SUBLANE, GDN_DK), F32),
               pltpu.VMEM((s * SUBLANE, GDN_DV), F32)]
    return _mixer_call(functools.partial(_gdn_body, r, grp.g), grp, l, depth, m_total, in_specs,
                       [hmain, hmain, hmain, hmain, gates, conv_w, conv_w, conv_w, a_log_row, dt_bias_row,
                        norm_w, bufpad, bufpad, bufpad, s0],
                       GDN_HEADS * GDN_DV, GDN_DV, [(GDN_DK, GDN_DV)],
                       [tail(GDN_DK), tail(GDN_DK), tail(GDN_DV)], scratch, prev, "gdn")


def _rglru_call(grp, l, depth, m_total, hmain, conv_w, conv_b, w_a, w_x, b_a, b_x, lam, bufpad, h0, prev):
    r, s = grp.r, grp.s
    w = LRU_BLOCK
    row = lambda off: pl.BlockSpec((r, w), lambda b, h, t: (grp.rb(b, t), off // w + h))
    vec = pl.BlockSpec((1, w), lambda b, h, t: (0, h))
    blk = pl.BlockSpec((1, w, w), lambda b, h, t: (h, 0, 0))
    in_specs = [row(C_RGX), row(C_RGY),
                pl.BlockSpec((CONV_W, w), lambda b, h, t: (0, h)), vec, blk, blk, vec, vec, vec,
                pl.BlockSpec((s * SUBLANE, w), lambda b, h, t: (b, h)),
                pl.BlockSpec((None, s, 1, 1, w), lambda b, h, t: (l, b, h, 0, 0))]
    tail = (jax.ShapeDtypeStruct((grp.n_seq * SUBLANE, LRU_WIDTH), F32),
            pl.BlockSpec((s * SUBLANE, w), lambda b, h, t: (b, h)))
    scratch = [pltpu.VMEM((s, 1, w), F32), pltpu.VMEM((s * SUBLANE, w), F32)]
    return _mixer_call(functools.partial(_rglru_body, r, grp.g), grp, l, depth, m_total, in_specs,
                       [hmain, hmain, conv_w, conv_b, w_a, w_x, b_a, b_x, lam, bufpad, h0],
                       LRU_WIDTH, w, [(1, w)], [tail], scratch, prev, "rglru")


def _pad_conv_state(buf):
    b, k, c = buf.shape
    return jnp.pad(buf.astype(F32), ((0, 0), (SUBLANE - k, 0), (0, 0))).reshape(b * SUBLANE, c)


def _unpad_conv_state(tail, n_seq):
    return tail.reshape(n_seq, SUBLANE, -1)[:, SUBLANE - (CONV_W - 1):, :]


R_I1, R_I2, R_W1, R_W2, R_P1, R_P2 = 0, 1, 2, 3, 4, 5
MOE_TM = 512
MOE_TD = 512


def _router_call(x, router, tm):
    m, d = x.shape
    n_exp = router.shape[1]
    wr = jnp.pad(router.astype(F32), ((0, 0), (0, LANE - n_exp)))

    def body(x_ref, w_ref, o_ref, cnt_ref, carry):
        i = pl.program_id(0)

        @pl.when(i == 0)
        def _():
            carry[...] = jnp.zeros_like(carry)

        xh, xl = _split2(x_ref[...])
        wh, wl = _split2(w_ref[...])
        logits = _dot(xh, wh) + (_dot(xh, wl) + _dot(xl, wh))
        lane = lax.broadcasted_iota(jnp.int32, logits.shape, 1)
        lg = jnp.where(lane < n_exp, logits, -jnp.inf)
        v1 = jnp.max(lg, axis=1, keepdims=True)
        i1 = jnp.min(jnp.where(lg == v1, lane, LANE), axis=1, keepdims=True)
        lg2 = jnp.where(lane == i1, -jnp.inf, lg)
        v2 = jnp.max(lg2, axis=1, keepdims=True)
        i2 = jnp.min(jnp.where(lg2 == v2, lane, LANE), axis=1, keepdims=True)
        e2 = jnp.exp(v2 - v1)
        w1 = 1.0 / (1.0 + e2)
        w2 = e2 / (1.0 + e2)
        sel = ((lane == i1) | (lane == i2)).astype(F32)
        below = (lax.broadcasted_iota(jnp.int32, (tm, tm), 1)
                 < lax.broadcasted_iota(jnp.int32, (tm, tm), 0)).astype(BF16)
        rank = jnp.dot(below, sel.astype(BF16), preferred_element_type=F32) + carry[0:1, :]
        p1 = jnp.sum(jnp.where(lane == i1, rank, 0.0), axis=1, keepdims=True)
        p2 = jnp.sum(jnp.where(lane == i2, rank, 0.0), axis=1, keepdims=True)
        rec = jnp.where(lane == R_I1, i1.astype(F32), jnp.where(lane == R_I2, i2.astype(F32), 0.0))
        rec = jnp.where(lane == R_W1, w1, jnp.where(lane == R_W2, w2, rec))
        o_ref[...] = jnp.where(lane == R_P1, p1, jnp.where(lane == R_P2, p2, rec))
        carry[...] = carry[...] + jnp.sum(sel, axis=0, keepdims=True)
        cnt_ref[...] = carry[...]

    return pl.pallas_call(
        body,
        out_shape=[jax.ShapeDtypeStruct((m, LANE), F32), jax.ShapeDtypeStruct((SUBLANE, LANE), F32)],
        grid=(m // tm,),
        in_specs=[pl.BlockSpec((tm, d), lambda i: (i, 0)), pl.BlockSpec((d, LANE), lambda i: (0, 0))],
        out_specs=[pl.BlockSpec((tm, LANE), lambda i: (i, 0)), pl.BlockSpec((SUBLANE, LANE), lambda i: (0, 0))],
        scratch_shapes=[pltpu.VMEM((SUBLANE, LANE), F32)],
        compiler_params=_cparams(("arbitrary",)),
        name="router",
    )(x, wr)


def _source_table(starts, e_arr, p_arr, m, ns):
    td = MOE_TD

    def body(start_ref, e_ref, p_ref, dst_ref, tbl_ref):
        i = pl.program_id(0)

        @pl.when(i == 0)
        def _():
            def clear(r, c):
                tbl_ref[r] = 0
                return c

            lax.fori_loop(0, ns, clear, 0, unroll=8)

        base = i * td

        def put(j, c):
            row = start_ref[e_ref[0, j]] + p_ref[0, j]
            dst_ref[0, j] = row
            tbl_ref[row] = base + (j >> 1)
            return c

        lax.fori_loop(0, 2 * td, put, 0, unroll=8)

    smem = pl.BlockSpec((None, 1, 2 * td), lambda i, s: (i, 0, 0), memory_space=pltpu.SMEM)
    return pl.pallas_call(
        body,
        grid_spec=pltpu.PrefetchScalarGridSpec(
            num_scalar_prefetch=1, grid=(m // td,), in_specs=[smem, smem],
            out_specs=[smem, pl.BlockSpec(memory_space=pltpu.SMEM)]),
        out_shape=[jax.ShapeDtypeStruct(e_arr.shape, jnp.int32), jax.ShapeDtypeStruct((ns,), jnp.int32)],
        compiler_params=_cparams(("arbitrary",)),
        name="moe_table",
    )(starts, e_arr, p_arr)


def _start_rows(copy_of, n_rows):
    def issue(r, c):
        copy_of(2 * r).start(priority=0)
        copy_of(2 * r + 1).start(priority=1)
        return c

    lax.fori_loop(0, n_rows // 2, issue, 0, unroll=4)


def _wait_rows(src_ref, dst_ref, sem):
    pltpu.make_async_copy(src_ref.at[pl.ds(0, dst_ref.shape[0])], dst_ref, sem).wait()


def _dispatch(x, table, n_used):
    m, d = x.shape
    ns = table.shape[0]
    tm = MOE_TM
    nt = ns // tm

    def body(nu_ref, src_ref, nxt_ref, x_ref, o_ref, buf, sem):
        i = pl.program_id(0)
        slot = i & 1

        def rows(idx_ref, s):
            return lambda r: pltpu.make_async_copy(x_ref.at[pl.ds(idx_ref[0, r], 1)],
                                                   buf.at[s, pl.ds(r, 1)], sem.at[s])

        @pl.when(i == 0)
        def _():
            _start_rows(rows(src_ref, 0), tm)

        @pl.when(i + 1 < nu_ref[0])
        def _():
            _start_rows(rows(nxt_ref, 1 - slot), tm)

        @pl.when(i < nu_ref[0])
        def _():
            _wait_rows(x_ref, buf.at[slot], sem.at[slot])
            o_ref[...] = buf[slot].astype(BF16)

    idx = lambda off: pl.BlockSpec((None, 1, tm), lambda i, nu: (jnp.minimum(i + off, nt - 1), 0, 0),
                                   memory_space=pltpu.SMEM)
    tbl = table.reshape(nt, 1, tm)
    return pl.pallas_call(
        body,
        grid_spec=pltpu.PrefetchScalarGridSpec(
            num_scalar_prefetch=1, grid=(nt,),
            in_specs=[idx(0), idx(1), pl.BlockSpec(memory_space=pl.ANY)],
            out_specs=pl.BlockSpec((tm, d), lambda i, nu: (jnp.minimum(i, nu[0] - 1), 0)),
            scratch_shapes=[pltpu.VMEM((2, tm, d), F32), pltpu.SemaphoreType.DMA((2,))]),
        out_shape=jax.ShapeDtypeStruct((ns, d), BF16),
        compiler_params=_cparams(("arbitrary",)),
        name="moe_dispatch",
    )(n_used, tbl, tbl, x)


def _grouped(rows, ws, tile_expert, tile_first, n_used, epilogue, out_dtype, n_cols, tn, name, w_buffers=2):
    ns, k = rows.shape
    nt = ns // MOE_TM
    used = lambda i, nu: jnp.minimum(i, nu[0] - 1)
    w_spec = pl.BlockSpec((None, k, tn), lambda n, i, te, tf, nu: (te[i], 0, n),
                          pipeline_mode=pl.Buffered(w_buffers))

    def body(te_ref, tf_ref, nu_ref, x_ref, *refs):
        w_refs, o_ref, wb_refs = refs[:len(ws)], refs[len(ws)], refs[len(ws) + 1:]
        i = pl.program_id(1)

        @pl.when(tf_ref[i] == 1)
        def _():
            for w, wb in zip(w_refs, wb_refs):
                wb[...] = w[...].astype(BF16)

        @pl.when(i < nu_ref[0])
        def _():
            xv = x_ref[...].astype(BF16)
            accs = [jnp.dot(xv, wb[...], preferred_element_type=F32) for wb in wb_refs]
            o_ref[...] = epilogue(accs).astype(o_ref.dtype)

    return pl.pallas_call(
        body,
        grid_spec=pltpu.PrefetchScalarGridSpec(
            num_scalar_prefetch=3, grid=(n_cols // tn, nt),
            in_specs=[pl.BlockSpec((MOE_TM, k), lambda n, i, te, tf, nu: (used(i, nu), 0))] + [w_spec] * len(ws),
            out_specs=pl.BlockSpec((MOE_TM, tn), lambda n, i, te, tf, nu: (used(i, nu), n)),
            scratch_shapes=[pltpu.VMEM((k, tn), BF16) for _ in ws]),
        out_shape=jax.ShapeDtypeStruct((ns, n_cols), out_dtype),
        compiler_params=_cparams(("arbitrary", "arbitrary")),
        name=name,
    )(tile_expert, tile_first, n_used, rows, *ws)


def _combine_ln(x, ys, dst_rows, rec, ple, gamma, beta, split_rows):
    tm = MOE_TD
    m, n = x.shape
    n_tiles = m // tm
    tiles_a = None if split_rows is None else split_rows // tm

    def body(ds_ref, dn_ref, ys_ref, x_ref, r_ref, e_ref, g_ref, b_ref, oa_ref, ob_ref, ybuf, sem):
        i = pl.program_id(0)
        slot = i & 1

        def rows(d_ref, s):
            return lambda j: pltpu.make_async_copy(ys_ref.at[pl.ds(d_ref[0, j], 1)],
                                                   ybuf.at[s, j & 1, pl.ds(j >> 1, 1)], sem.at[s])

        @pl.when(i == 0)
        def _():
            _start_rows(rows(ds_ref, 0), 2 * tm)

        @pl.when(i + 1 < n_tiles)
        def _():
            _start_rows(rows(dn_ref, 1 - slot), 2 * tm)

        for k in range(2):
            _wait_rows(ys_ref, ybuf.at[slot, k], sem.at[slot])
        r = r_ref[...]
        lane = lax.broadcasted_iota(jnp.int32, r.shape, 1)
        w1 = jnp.sum(jnp.where(lane == R_W1, r, 0.0), axis=1, keepdims=True)
        w2 = jnp.sum(jnp.where(lane == R_W2, r, 0.0), axis=1, keepdims=True)
        y = DEEPNORM_ALPHA * x_ref[...] + (w1 * ybuf[slot, 0] + w2 * ybuf[slot, 1]) + e_ref[...]
        mu = jnp.mean(y, axis=-1, keepdims=True)
        yc = y - mu
        var = jnp.mean(yc * yc, axis=-1, keepdims=True)
        out = yc * lax.rsqrt(var + LN_EPS) * g_ref[...] + b_ref[...]
        if split_rows is None:
            oa_ref[...] = out
            ob_ref[...] = out.astype(BF16)
        else:
            @pl.when(i < tiles_a)
            def _():
                oa_ref[...] = out

            @pl.when(i >= tiles_a)
            def _():
                ob_ref[...] = out

    row = lambda i: (i, 0)
    vec = pl.BlockSpec((1, n), lambda i: (0, 0))
    smem = lambda off: pl.BlockSpec((None, 1, 2 * tm), lambda i: (jnp.minimum(i + off, n_tiles - 1), 0, 0),
                                    memory_space=pltpu.SMEM)
    if split_rows is None:
        out_shape = [jax.ShapeDtypeStruct((m, n), F32), jax.ShapeDtypeStruct((m, n), BF16)]
        out_specs = [pl.BlockSpec((tm, n), row), pl.BlockSpec((tm, n), row)]
    else:
        out_shape = [jax.ShapeDtypeStruct((split_rows, n), F32), jax.ShapeDtypeStruct((m - split_rows, n), F32)]
        out_specs = [pl.BlockSpec((tm, n), lambda i: (jnp.minimum(i, tiles_a - 1), 0)),
                     pl.BlockSpec((tm, n), lambda i: (jnp.maximum(i - tiles_a, 0), 0))]
    return pl.pallas_call(
        body,
        grid=(n_tiles,),
        in_specs=[smem(0), smem(1), pl.BlockSpec(memory_space=pl.ANY),
                  pl.BlockSpec((tm, n), row), pl.BlockSpec((tm, LANE), row), pl.BlockSpec((tm, n), row), vec, vec],
        out_specs=out_specs,
        scratch_shapes=[pltpu.VMEM((2, 2, tm, n), F32), pltpu.SemaphoreType.DMA((2,))],
        out_shape=out_shape,
        compiler_params=_cparams(("arbitrary",)),
        name="moe_combine_ln",
    )(dst_rows, dst_rows, ys, x, rec, ple, gamma.reshape(1, n), beta.reshape(1, n))


def _moe_ln(x, ple, router, w1, w3, w2, j, gamma, beta, split_rows):
    m, d = x.shape
    _, ne, _, fe = w1.shape
    w1, w3, w2 = (w.reshape((-1,) + w.shape[2:]) for w in (w1, w3, w2))
    rec, counts = _router_call(x, router, 512)
    cnt = counts[0, :ne].astype(jnp.int32)
    tiles_e = (cnt + MOE_TM - 1) // MOE_TM
    tile_end = jnp.cumsum(tiles_e)
    tile_beg = tile_end - tiles_e
    n_used = tile_end[ne - 1]
    nt = (2 * m) // MOE_TM + ne
    ids = jnp.minimum(jnp.arange(nt, dtype=jnp.int32), n_used - 1)
    tile_expert = jnp.sum((ids[:, None] >= tile_end[None, :]).astype(jnp.int32), axis=1)
    tile_first = (jnp.arange(nt, dtype=jnp.int32) == tile_beg[tile_expert]).astype(jnp.int32)
    starts = (tile_beg * MOE_TM).astype(jnp.int32)
    slots = lambda a, b: jnp.stack([rec[:, a], rec[:, b]], axis=1).astype(jnp.int32).reshape(m // MOE_TD, 1, 2 * MOE_TD)
    e_arr, p_arr = slots(R_I1, R_I2), slots(R_P1, R_P2)
    ns = nt * MOE_TM

    nu = n_used.reshape(1)
    dst_rows, table = _source_table(starts, e_arr, p_arr, m, ns)
    xs = _dispatch(x, table, nu)
    tile_w = tile_expert + j * ne
    hmid = _grouped(xs, [w1, w3], tile_w, tile_first, nu, lambda a: _silu(a[0]) * a[1], BF16, fe, fe // 2,
                    "moe_up", w_buffers=1)
    ys = _grouped(hmid, [w2], tile_w, tile_first, nu, lambda a: a[0], F32, d, 1024, "moe_down")
    return _combine_ln(x, ys, dst_rows, rec, ple, gamma, beta, split_rows)


def _run_group_mixers(grp, l, depth, m_total, hmain, gates, st, prm, prev):
    b = grp.n_seq
    ya, ml_c, ml_n, ml_m = _mlstm_call(
        grp, l, depth, m_total, hmain, gates, prm["gate_bias"], prm["ml_norm_w"],
        st["ml_C"].astype(F32), st["ml_n"].astype(F32).reshape(depth, b, ML_HEADS, 1, ML_DK),
        st["ml_m"].astype(F32).reshape(depth, b, ML_HEADS, 1, 1),
        [prev["ya"], prev["ml_C"], prev["ml_n"], prev["ml_m"]])
    yb, rg_h, rg_tail = _rglru_call(
        grp, l, depth, m_total, hmain, prm["rg_conv_w"], prm["rg_conv_b"], prm["rg_w_a"], prm["rg_w_x"],
        prm["rg_b_a"], prm["rg_b_x"], prm["rg_lambda"], _pad_conv_state(st["rg_conv"][l]),
        st["rg_h"].astype(F32).reshape(depth, b, LRU_BLOCKS, 1, LRU_BLOCK), [prev["yb"], prev["rg_h"]])
    yc, gd_s, tq, tk, tv = _gdn_call(
        grp, l, depth, m_total, hmain, gates, prm["gd_conv_w"], prm["gd_a_log"], prm["gd_dt_bias"],
        prm["gd_norm_w"], _pad_conv_state(st["gd_conv"][l]), st["gd_S"].astype(F32),
        [prev["yc"], prev["gd_S"]])
    out = dict(ya=ya, yb=yb, yc=yc, ml_C=ml_c, ml_n=ml_n, ml_m=ml_m, rg_h=rg_h, gd_S=gd_s)
    tails = dict(rg_conv=_unpad_conv_state(rg_tail, b),
                 gd_conv=jnp.concatenate([_unpad_conv_state(x, b) for x in (tq, tk, tv)], axis=-1))
    return out, tails


_STATE_KEYS = ("ml_C", "ml_n", "ml_m", "rg_h", "gd_S")


def kernel(x_prompt, x_sample, state_mlstm_C, state_mlstm_n, state_mlstm_m, state_rglru_h, state_rglru_conv, state_gdn_S, state_gdn_conv, p_prompt, p_sample, w_in, ml_b_i, ml_b_f, ml_norm_w, rg_conv_w, rg_conv_b, rg_w_a, rg_b_a, rg_w_x, rg_b_x, rg_lambda, gd_conv_w, gd_A_log, gd_dt_bias, gd_norm_w, w_up_mlstm, w_up_rglru, w_up_gdn, w_out, ln1_g, ln1_b, ffn_w1, ffn_w3, ffn_w2, moe_router, moe_w1, moe_w3, moe_w2, ple_w, ple_gate_w, ln2_g, ln2_b):
    bp, tp, d = x_prompt.shape
    bs, ts, _ = x_sample.shape
    depth = w_in.shape[0]
    gp = _Group(bp, tp, 0)
    gs = _Group(bs, ts, bp * tp)
    m = gp.rows + gs.rows
    pd = x_prompt.dtype

    zeros = lambda *shape: jnp.zeros(shape, F32)
    st_p = dict(ml_C=zeros(depth, bp, ML_HEADS, ML_DV, ML_DK), ml_n=zeros(depth, bp, ML_HEADS, ML_DK),
                ml_m=zeros(depth, bp, ML_HEADS), rg_h=zeros(depth, bp, LRU_WIDTH),
                rg_conv=zeros(depth, bp, CONV_W - 1, LRU_WIDTH),
                gd_S=zeros(depth, bp, GDN_HEADS, GDN_DK, GDN_DV),
                gd_conv=zeros(depth, bp, CONV_W - 1, GDN_CONV_DIM))
    st_s = dict(ml_C=state_mlstm_C, ml_n=state_mlstm_n, ml_m=state_mlstm_m, rg_h=state_rglru_h,
                rg_conv=state_rglru_conv, gd_S=state_gdn_S, gd_conv=state_gdn_conv)

    x = jnp.concatenate([x_prompt.reshape(gp.rows, d), x_sample.reshape(gs.rows, d)], axis=0).astype(F32)
    xb = x.astype(BF16)
    w_in_t = jnp.swapaxes(w_in, 1, 2)
    lane_row = lambda v, off: jnp.zeros((1, LANE), F32).at[0, off:off + v.shape[0]].set(v.astype(F32))
    acc_p = {k: None for k in _STATE_KEYS}
    acc_s = {k: None for k in _STATE_KEYS}
    tails_p, tails_s = [], []

    for l in range(depth):
        hmain = _in_proj(xb, w_in_t, l, 1024)
        gates = _gate_proj(xb, w_in_t, l, 1024)

        prm = dict(
            gate_bias=lane_row(ml_b_i[l], G_MLI) + lane_row(ml_b_f[l], G_MLF),
            ml_norm_w=ml_norm_w[l].astype(F32).reshape(1, -1),
            rg_conv_w=rg_conv_w[l].astype(F32), rg_conv_b=rg_conv_b[l].astype(F32).reshape(1, -1),
            rg_w_a=rg_w_a[l], rg_w_x=rg_w_x[l], rg_b_a=rg_b_a[l].astype(F32).reshape(1, -1),
            rg_b_x=rg_b_x[l].astype(F32).reshape(1, -1), rg_lambda=rg_lambda[l].astype(F32).reshape(1, -1),
            gd_conv_w=gd_conv_w[l].astype(F32), gd_a_log=lane_row(gd_A_log[l], G_GDA),
            gd_dt_bias=lane_row(gd_dt_bias[l], G_GDA), gd_norm_w=gd_norm_w[l].astype(F32).reshape(1, -1))
        out_p, tl_p = _run_group_mixers(gp, l, depth, m, hmain, gates, st_p, prm,
                                        dict(acc_p, ya=None, yb=None, yc=None))
        out_s, tl_s = _run_group_mixers(gs, l, depth, m, hmain, gates, st_s, prm,
                                        dict(acc_s, ya=out_p["ya"], yb=out_p["yb"], yc=out_p["yc"]))
        acc_p = {k: out_p[k] for k in _STATE_KEYS}
        acc_s = {k: out_s[k] for k in _STATE_KEYS}
        tails_p.append(tl_p)
        tails_s.append(tl_s)
        ya, yb, yc = out_s["ya"], out_s["yb"], out_s["yc"]

        tn, tm = 1024, 512
        up_spec = pl.BlockSpec((None, ya.shape[1], tn), lambda n, i: (l, 0, n))
        mg_spec = lambda br: pl.BlockSpec((tm, tn), lambda n, i: (i, (C_MG + br * d) // tn + n))
        merged = _proj([ya, yb, yc], [w_up_mlstm, w_up_rglru, w_up_gdn], [up_spec] * 3,
                       [hmain] * 3, [mg_spec(0), mg_spec(1), mg_spec(2)],
                       lambda accs, ex, ids: [_sigmoid(ex[0]) * accs[0] + _sigmoid(ex[1]) * accs[1]
                                              + _sigmoid(ex[2]) * accs[2]],
                       [jax.ShapeDtypeStruct((m, d), BF16)], d, tm, tn, "merge")[0]
        x, xb = _rowfull_ln(merged, w_out[l].astype(BF16), x, None, ln1_g[l].astype(F32), ln1_b[l].astype(F32),
                            512, d, "out_proj_ln")

        p_l = jnp.concatenate([p_prompt[l].reshape(gp.rows, -1), p_sample[l].reshape(gs.rows, -1)], axis=0)
        ple = _proj([xb, p_l], [ple_gate_w, ple_w],
                    [pl.BlockSpec((None, d, 1024), lambda n, i: (l, 0, n)),
                     pl.BlockSpec((None, P_DIM, 1024), lambda n, i: (l, 0, n))],
                    [], [], lambda accs, ex, ids: [_sigmoid(accs[0]) * accs[1]],
                    [jax.ShapeDtypeStruct((m, d), F32)], d, 1024, 1024, "ple")[0]
        j = l // 2
        split = gp.rows if l == depth - 1 else None
        g2, b2 = ln2_g[l].astype(F32), ln2_b[l].astype(F32)
        if l % 2 == 0:
            ff = ffn_w1.shape[2]
            hmid = _proj([xb], [ffn_w1, ffn_w3], [pl.BlockSpec((None, d, 512), lambda n, i: (j, 0, n))] * 2,
                         [], [], lambda accs, ex, ids: [_silu(accs[0]) * accs[1]],
                         [jax.ShapeDtypeStruct((m, ff), BF16)], ff, 1024, 512, "ffn_up", x_of_w=(0, 0))[0]
            x, xb = _rowfull_ln(hmid, ffn_w2[j].astype(BF16), x, ple, g2, b2, 256, ff, "ffn_down_ln",
                                split_rows=split)
        else:
            x, xb = _moe_ln(x, ple, moe_router[j], moe_w1, moe_w3, moe_w2, j, g2, b2, split)

    y_prompt = x.reshape(bp, tp, d).astype(pd)
    y_sample = xb.reshape(bs, ts, d).astype(x_sample.dtype)

    def finish(acc, tails, n_seq, dtypes):
        conv = {k: jnp.stack([tl[k] for tl in tails]) for k in ("rg_conv", "gd_conv")}
        vals = dict(ml_C=acc["ml_C"], ml_n=acc["ml_n"].reshape(depth, n_seq, ML_HEADS, ML_DK),
                    ml_m=acc["ml_m"].reshape(depth, n_seq, ML_HEADS),
                    rg_h=acc["rg_h"].reshape(depth, n_seq, LRU_WIDTH), rg_conv=conv["rg_conv"],
                    gd_S=acc["gd_S"], gd_conv=conv["gd_conv"])
        names = ("ml_C", "ml_n", "ml_m", "rg_h", "rg_conv", "gd_S", "gd_conv")
        return tuple(vals[k].astype(dtypes[k]) for k in names)

    return ((y_prompt, y_sample) + finish(acc_p, tails_p, bp, {k: pd for k in st_p})
            + finish(acc_s, tails_s, bs, {k: v.dtype for k, v in st_s.items()}))
```

```python
import functools
import math

import jax
import jax.numpy as jnp
from jax import lax
from jax.experimental import pallas as pl
from jax.experimental.pallas import tpu as pltpu

F32 = jnp.float32
BF16 = jnp.bfloat16

D_MODEL = 2048
DEPTH = 2
P_DIM = 256
CONV_W = 4
N_BRANCH = 3
ML_HEADS = 4
ML_DK = 128
ML_DV = 256
ML_GATE_CAP = 15.0
LRU_WIDTH = 1024
LRU_BLOCKS = 4
LRU_BLOCK = LRU_WIDTH // LRU_BLOCKS
LRU_C = 8.0
GDN_HEADS = 4
GDN_DK = 128
GDN_DV = 256
GDN_QK = GDN_HEADS * GDN_DK
GDN_CONV_DIM = 2 * GDN_QK + GDN_HEADS * GDN_DV
N_EXPERTS = 8
DEEPNORM_ALPHA = (2 * DEPTH) ** 0.25
LN_EPS = 1e-5
RMS_EPS = 1e-6
L2_EPS = 1e-6

LANE = 128
SUBLANE = 8
CHUNK = 128
VMEM_LIMIT = 56 * 1024 * 1024

C_MLQ, C_MLK, C_MLV, C_MLO = 0, 512, 1024, 2048
C_RGX, C_RGY = 3072, 4096
C_GDQ, C_GDK, C_GDV, C_GDZ = 5120, 5632, 6144, 7168
C_MG = 8192
N_MAIN = C_MG + N_BRANCH * D_MODEL
W_GATES_A, W_GATES_B = 2048, 7176
G_MLI, G_MLF, G_GDB, G_GDA = 0, 4, 8, 12
PROJ_TN = 1024


def _cparams(sem):
    return pltpu.CompilerParams(dimension_semantics=sem, vmem_limit_bytes=VMEM_LIMIT)


def _dot(a, b):
    return jnp.dot(a.astype(BF16), b.astype(BF16), preferred_element_type=F32)


def _dot_nt(a, b):
    return lax.dot_general(a.astype(BF16), b.astype(BF16), (((1,), (1,)), ((), ())),
                           preferred_element_type=F32)


def _dot_tn(a, b):
    return lax.dot_general(a.astype(BF16), b.astype(BF16), (((0,), (0,)), ((), ())),
                           preferred_element_type=F32)


def _bmm(a, b):
    return jnp.einsum('cij,cjk->cik', a.astype(BF16), b.astype(BF16), preferred_element_type=F32)


def _bmm_nt(a, b):
    return jnp.einsum('cik,cjk->cij', a.astype(BF16), b.astype(BF16), preferred_element_type=F32)


def _bmm_tn(a, b):
    return jnp.einsum('csi,csj->cij', a.astype(BF16), b.astype(BF16), preferred_element_type=F32)


def _sigmoid(x):
    return 1.0 / (1.0 + jnp.exp(-x))


def _silu(x):
    return x * _sigmoid(x)


def _softplus(x):
    return jnp.maximum(x, 0.0) + jnp.log1p(jnp.exp(-jnp.abs(x)))


def _proj(xs, ws, w_specs, extras, e_specs, epilogue, out_shapes, n_cols, tm, tn, name,
          lead_grid=(), x_of_w=None):
    m = xs[0].shape[0]
    nl = len(lead_grid)
    grid = tuple(lead_grid) + (n_cols // tn, m // tm)
    x_specs = [pl.BlockSpec((tm, x.shape[1]), lambda *g: (g[-1], 0)) for x in xs]
    o_specs = [pl.BlockSpec((tm, tn), lambda *g: (g[-1], g[-2]) if nl == 0 else
                            (g[-1], g[0] * (n_cols // tn) + g[-2])) for _ in out_shapes]
    scratch = [pltpu.VMEM(tuple(d for d in s.block_shape if d is not None), BF16) for s in w_specs]
    xw = tuple(range(len(ws))) if x_of_w is None else tuple(x_of_w)

    def body(*refs):
        nx, nw, ne, no = len(xs), len(ws), len(extras), len(out_shapes)
        x_refs = refs[:nx]
        w_refs = refs[nx:nx + nw]
        e_refs = refs[nx + nw:nx + nw + ne]
        o_refs = refs[nx + nw + ne:nx + nw + ne + no]
        wb_refs = refs[nx + nw + ne + no:]

        @pl.when(pl.program_id(nl + 1) == 0)
        def _():
            for w, wb in zip(w_refs, wb_refs):
                wb[...] = w[...].astype(BF16)

        xv = [x[...].astype(BF16) for x in x_refs]
        accs = [jnp.dot(xv[j], wb[...], preferred_element_type=F32) for j, wb in zip(xw, wb_refs)]
        outs = epilogue(accs, [e[...] for e in e_refs], [pl.program_id(i) for i in range(nl)])
        for o, v in zip(o_refs, outs):
            o[...] = v.astype(o.dtype)

    return pl.pallas_call(
        body,
        out_shape=out_shapes,
        grid=grid,
        in_specs=x_specs + list(w_specs) + list(e_specs),
        out_specs=o_specs,
        scratch_shapes=scratch,
        compiler_params=_cparams(("arbitrary",) * len(grid)),
        name=name,
    )(*xs, *ws, *extras)


def _in_proj(xb, wt, l, tm):
    m, d = xb.shape
    tn = PROJ_TN
    n_blocks = N_MAIN // tn
    first_a = W_GATES_A // tn
    first_b = (W_GATES_B - SUBLANE) // tn
    nxt = 2 * SUBLANE

    def body(x_ref, wm_ref, wn_ref, o_ref, wb_ref):
        n = pl.program_id(0)

        @pl.when(pl.program_id(1) == 0)
        def _():
            def shifted(s):
                wcat = jnp.concatenate([wm_ref[...], wn_ref[...]], axis=0)
                wb_ref[...] = wcat[s:s + tn, :].astype(BF16)

            @pl.when(n < first_a)
            def _():
                wb_ref[...] = wm_ref[...].astype(BF16)

            @pl.when((n >= first_a) & (n < first_b))
            def _():
                shifted(SUBLANE)

            @pl.when(n >= first_b)
            def _():
                shifted(2 * SUBLANE)

        o_ref[...] = lax.dot_general(x_ref[...], wb_ref[...], (((1,), (1,)), ((), ())),
                                     preferred_element_type=F32)

    return pl.pallas_call(
        body,
        out_shape=jax.ShapeDtypeStruct((m, N_MAIN), F32),
        grid=(n_blocks, m // tm),
        in_specs=[pl.BlockSpec((tm, d), lambda n, i: (i, 0)),
                  pl.BlockSpec((None, tn, d), lambda n, i: (l, n, 0)),
                  pl.BlockSpec((None, nxt, d), lambda n, i: (l, (n + 1) * (tn // nxt), 0))],
        out_specs=pl.BlockSpec((tm, tn), lambda n, i: (i, n)),
        scratch_shapes=[pltpu.VMEM((tn, d), BF16)],
        compiler_params=_cparams(("arbitrary", "arbitrary")),
        name="in_proj",
    )(xb, wt, wt)


def _gate_proj(xb, wt, l, tm):
    m, d = xb.shape
    blk_a, blk_b = W_GATES_A // LANE, W_GATES_B // LANE

    def body(x_ref, wa_ref, wb_ref, o_ref):
        w = jnp.concatenate([wa_ref[...], wb_ref[...]], axis=0).astype(BF16)
        o_ref[...] = lax.dot_general(x_ref[...], w, (((1,), (1,)), ((), ())), preferred_element_type=F32)

    return pl.pallas_call(
        body,
        out_shape=jax.ShapeDtypeStruct((m, 2 * LANE), F32),
        grid=(m // tm,),
        in_specs=[pl.BlockSpec((tm, d), lambda i: (i, 0)),
                  pl.BlockSpec((None, LANE, d), lambda i: (l, blk_a, 0)),
                  pl.BlockSpec((None, LANE, d), lambda i: (l, blk_b, 0))],
        out_specs=pl.BlockSpec((tm, 2 * LANE), lambda i: (i, 0)),
        compiler_params=_cparams(("arbitrary",)),
        name="gate_proj",
    )(xb, wt, wt)


def _rowfull_ln(h, w_bf16, res, extra, gamma, beta, tm, tk, name, split_rows=None):
    m, k = h.shape
    n = w_bf16.shape[1]
    nk = k // tk
    has_extra = extra is not None
    n_in = 6 if has_extra else 5
    tiles_a = None if split_rows is None else split_rows // tm

    def body(*refs):
        h_ref, w_ref, r_ref = refs[:3]
        e_ref = refs[3] if has_extra else None
        g_ref, b_ref = refs[n_in - 2:n_in]
        oa_ref, ob_ref, acc_ref = refs[n_in:]
        i = pl.program_id(0)
        kk = pl.program_id(1)

        @pl.when(kk == 0)
        def _():
            acc_ref[...] = jnp.zeros_like(acc_ref)

        acc_ref[...] += jnp.dot(h_ref[...], w_ref[...], preferred_element_type=F32)

        @pl.when(kk == nk - 1)
        def _():
            y = DEEPNORM_ALPHA * r_ref[...] + acc_ref[...]
            if e_ref is not None:
                y = y + e_ref[...].astype(F32)
            mu = jnp.mean(y, axis=-1, keepdims=True)
            yc = y - mu
            var = jnp.mean(yc * yc, axis=-1, keepdims=True)
            out = yc * lax.rsqrt(var + LN_EPS) * g_ref[...] + b_ref[...]
            if split_rows is None:
                oa_ref[...] = out
                ob_ref[...] = out.astype(BF16)
            else:
                @pl.when(i < tiles_a)
                def _():
                    oa_ref[...] = out

                @pl.when(i >= tiles_a)
                def _():
                    ob_ref[...] = out

    row = lambda i, j: (i, 0)
    in_specs = [pl.BlockSpec((tm, tk), lambda i, j: (i, j)),
                pl.BlockSpec((tk, n), lambda i, j: (j, 0), pipeline_mode=pl.Buffered(1 if nk == 1 else 2)),
                pl.BlockSpec((tm, n), row)]
    args = [h, w_bf16, res]
    if has_extra:
        in_specs.append(pl.BlockSpec((tm, n), row))
        args.append(extra)
    in_specs += [pl.BlockSpec((1, n), lambda i, j: (0, 0))] * 2
    args += [gamma.reshape(1, n), beta.reshape(1, n)]
    if split_rows is None:
        out_shape = [jax.ShapeDtypeStruct((m, n), F32), jax.ShapeDtypeStruct((m, n), BF16)]
        out_specs = [pl.BlockSpec((tm, n), row), pl.BlockSpec((tm, n), row)]
    else:
        out_shape = [jax.ShapeDtypeStruct((split_rows, n), F32), jax.ShapeDtypeStruct((m - split_rows, n), F32)]
        out_specs = [pl.BlockSpec((tm, n), lambda i, j: (jnp.minimum(i, tiles_a - 1), 0)),
                     pl.BlockSpec((tm, n), lambda i, j: (jnp.maximum(i - tiles_a, 0), 0))]
    return pl.pallas_call(
        body,
        out_shape=out_shape,
        grid=(m // tm, nk),
        in_specs=in_specs,
        out_specs=out_specs,
        scratch_shapes=[pltpu.VMEM((tm, n), F32)],
        compiler_params=_cparams(("arbitrary", "arbitrary")),
        name=name,
    )(*args)


def _seq_masks(g, lg_shift):
    row = lax.broadcasted_iota(jnp.int32, (CHUNK, CHUNK), 0)
    col = lax.broadcasted_iota(jnp.int32, (CHUNK, CHUNK), 1)
    if g == 1:
        return col <= row, col < row
    same = (row >> lg_shift) == (col >> lg_shift)
    return same & (col <= row), same & (col < row)


def _per_seq_rows(vals, g, lg):
    if g == 1:
        return vals[0]
    return jnp.concatenate([jnp.broadcast_to(v, (lg, v.shape[1])) for v in vals], axis=0)


def _last_rows(col, g, lg):
    if g == 1:
        return col[:, CHUNK - 1:, :]
    return _per_seq_rows([col[0, (i + 1) * lg - 1:(i + 1) * lg, :] for i in range(g)], g, lg)[None]


def _split3(x):
    x1 = x.astype(BF16).astype(F32)
    r1 = x - x1
    x2 = r1.astype(BF16).astype(F32)
    x3 = (r1 - x2).astype(BF16).astype(F32)
    return x1, x2, x3


def _split2(x):
    hi = x.astype(BF16)
    return hi, (x - hi.astype(F32)).astype(BF16)


def _lanes3(col, lane_ids, first, other):
    c1, c2, c3 = _split3(col)
    x = jnp.where(lane_ids == first, c1, jnp.where(lane_ids == first + 1, c2,
                                                   jnp.where(lane_ids == first + 2, c3, 0.0)))
    return jnp.where((lane_ids >= other) & (lane_ids < other + 3), 1.0, x).astype(BF16)


def _cumsum_rows(col, tri_b, lane_ids):
    return jnp.sum(_bmm(tri_b, _lanes3(col, lane_ids, 0, LANE)), axis=-1, keepdims=True)


def _outer_sum(a_col, b_col, lane_ids):
    return _bmm_nt(_lanes3(a_col, lane_ids, 0, 3), _lanes3(b_col, lane_ids, 3, 0))


def _bmm2(a, b):
    ah, al = _split2(a)
    bb = b.astype(BF16)
    return _bmm(ah, bb) + _bmm(al, bb)


def _conv_rows(x, prev, w, g, lg):
    f = x.shape[1]
    xx = jnp.concatenate([prev.reshape(g, SUBLANE, f), x.reshape(g, lg, f)], axis=1)
    y = xx[:, SUBLANE:, :] * w[CONV_W - 1:CONV_W, :]
    for s in range(1, CONV_W):
        y = y + xx[:, SUBLANE - s:SUBLANE - s + lg, :] * w[CONV_W - 1 - s:CONV_W - s, :]
    return y.reshape(g * lg, f), xx[:, lg:, :].reshape(g * SUBLANE, f)


def _mlstm_body(r, g, q_ref, k_ref, v_ref, o_ref, gt_ref, gb_ref, nw_ref, c0_ref, n0_ref, m0_ref,
                y_ref, c_out, n_out, m_out, c_s, n_s, m_s):
    lg = CHUNK // g
    lg_shift = int(math.log2(lg))
    nch = r // CHUNK
    h = pl.program_id(1)
    t = pl.program_id(2)

    @pl.when(t == 0)
    def _():
        c_s[...] = c0_ref[:, 0]
        n_s[...] = n0_ref[:, 0]
        m_s[...] = jnp.broadcast_to(m0_ref[:, 0], m_s.shape)

    lane3 = lax.broadcasted_iota(jnp.int32, (nch, CHUNK, LANE), 2)
    causal, _ = _seq_masks(g, lg_shift)
    tri_b = jnp.broadcast_to(causal.astype(BF16), (nch, CHUNK, CHUNK))

    pre = gt_ref[...].reshape(nch, CHUNK, LANE) + gb_ref[...]
    capd = ML_GATE_CAP * jnp.tanh(pre * (1.0 / ML_GATE_CAP))
    logsig = jnp.minimum(capd, 0.0) - jnp.log1p(jnp.exp(-jnp.abs(capd)))
    ic = jnp.sum(jnp.where(lane3 == G_MLI + h, capd, 0.0), axis=2, keepdims=True)
    lf = jnp.sum(jnp.where(lane3 == G_MLF + h, logsig, 0.0), axis=2, keepdims=True)
    bc = _cumsum_rows(lf, tri_b, lane3)
    dm = jnp.where(causal, _outer_sum(bc, ic - bc, lane3), -jnp.inf)
    rmax = jnp.max(dm, axis=2, keepdims=True)

    if g == 1:
        mp = m_s[0][:, 0:1]
        m_prevs = []
        for c in range(nch):
            m_prevs.append(mp.reshape(1, 1, 1))
            mp = jnp.maximum(mp + bc[c, CHUNK - 1:, :], rmax[c, CHUNK - 1:, :])
        m_prev = jnp.concatenate(m_prevs, axis=0) if nch > 1 else m_prevs[0]
        m_s[0] = jnp.broadcast_to(mp, (1, LANE))
    else:
        m_prev = _per_seq_rows([m_s[i][:, 0:1] for i in range(g)], g, lg)[None]
    m_t = jnp.maximum(m_prev + bc, rmax)
    dmat = jnp.exp(dm - m_t)
    q3 = q_ref[...].reshape(nch, CHUNK, ML_DK) * (ML_DK ** -0.5)
    k3 = k_ref[...].reshape(nch, CHUNK, ML_DK)
    v3 = v_ref[...].reshape(nch, CHUNK, ML_DV)
    s = _bmm_nt(q3, k3) * dmat
    inter = jnp.exp(bc + m_prev - m_t)
    bc_end = _last_rows(bc, g, lg)
    m_new = _last_rows(m_t, g, lg)
    w_col = jnp.exp(ic - bc + bc_end - m_new)
    dec = jnp.exp(bc_end + m_prev - m_new)
    sv = _bmm(s, v3)
    ssum = jnp.sum(s, axis=2, keepdims=True)
    vw = v3 * w_col
    kw = k3 * w_col
    floor = jnp.exp(-m_t)

    def emit(rows, hh):
        hn = hh * lax.rsqrt(jnp.mean(hh * hh, axis=1, keepdims=True) + RMS_EPS) * nw_ref[...]
        y_ref[rows, :] = (_sigmoid(o_ref[rows, :]) * hn).astype(y_ref.dtype)

    if g == 1:
        upd = _bmm_tn(vw, k3)
        ksum = jnp.sum(kw, axis=1, keepdims=True)
        cc, nn = c_s[0], n_s[0]
        for c in range(nch):
            num = sv[c] + inter[c] * _dot_nt(q3[c], cc)
            den = ssum[c] + inter[c] * jnp.sum(q3[c] * nn, axis=1, keepdims=True)
            emit(slice(c * CHUNK, (c + 1) * CHUNK), num / jnp.maximum(jnp.abs(den), floor[c]))
            cc = dec[c] * cc + upd[c]
            nn = dec[c] * nn + ksum[c]
        c_s[0] = cc
        n_s[0] = nn
    else:
        rows_i = lax.broadcasted_iota(jnp.int32, (CHUNK, 1), 0)
        q2, k2 = q3[0], k3[0]
        qc = jnp.concatenate([_dot_nt(q2[i * lg:(i + 1) * lg], c_s[i]) for i in range(g)], axis=0)
        qn = jnp.concatenate([jnp.sum(q2[i * lg:(i + 1) * lg] * n_s[i], axis=1, keepdims=True)
                              for i in range(g)], axis=0)
        num = sv[0] + inter[0] * qc
        den = ssum[0] + inter[0] * qn
        emit(slice(0, CHUNK), num / jnp.maximum(jnp.abs(den), floor[0]))
        for i in range(g):
            dec_i = dec[0, i * lg:i * lg + 1, :]
            c_s[i] = dec_i * c_s[i] + _dot_tn(jnp.where((rows_i >> lg_shift) == i, vw[0], 0.0), k2)
            n_s[i] = dec_i * n_s[i] + jnp.sum(kw[0, i * lg:(i + 1) * lg], axis=0, keepdims=True)
            m_s[i] = jnp.broadcast_to(m_new[0, i * lg:i * lg + 1, :], (1, LANE))

    @pl.when(t == pl.num_programs(2) - 1)
    def _():
        c_out[:, 0] = c_s[...]
        n_out[:, 0] = n_s[...]
        m_out[:, 0] = m_s[:, :, 0:1]


def _gdn_body(r, g, q_ref, k_ref, v_ref, z_ref, gt_ref, cwq_ref, cwk_ref, cwv_ref, al_ref, dtb_ref,
              nw_ref, bq_ref, bk_ref, bv_ref, s0_ref, y_ref, s_out, tq_out, tk_out, tv_out,
              s_s, pq_s, pk_s, pv_s):
    lg = CHUNK // g
    lg_shift = int(math.log2(lg))
    lseq = r if g == 1 else lg
    nch = r // CHUNK
    h = pl.program_id(1)
    t = pl.program_id(2)

    @pl.when(t == 0)
    def _():
        s_s[...] = s0_ref[:, 0]
        pq_s[...] = bq_ref[...]
        pk_s[...] = bk_ref[...]
        pv_s[...] = bv_ref[...]

    qa, tq = _conv_rows(q_ref[...], pq_s[...], cwq_ref[...], g, lseq)
    ka, tk = _conv_rows(k_ref[...], pk_s[...], cwk_ref[...], g, lseq)
    va, tv = _conv_rows(v_ref[...], pv_s[...], cwv_ref[...], g, lseq)
    pq_s[...] = tq
    pk_s[...] = tk
    pv_s[...] = tv
    qa, ka, va = _silu(qa), _silu(ka), _silu(va)
    qn = qa * lax.rsqrt(jnp.sum(qa * qa, axis=1, keepdims=True) + L2_EPS) * (GDN_DK ** -0.5)
    kn = ka * lax.rsqrt(jnp.sum(ka * ka, axis=1, keepdims=True) + L2_EPS)

    lane3 = lax.broadcasted_iota(jnp.int32, (nch, CHUNK, LANE), 2)
    causal, strict = _seq_masks(g, lg_shift)
    tri_b = jnp.broadcast_to(causal.astype(BF16), (nch, CHUNK, CHUNK))
    eye = (lax.broadcasted_iota(jnp.int32, (CHUNK, CHUNK), 0)
           == lax.broadcasted_iota(jnp.int32, (CHUNK, CHUNK), 1)).astype(F32)

    q3 = qn.reshape(nch, CHUNK, GDN_DK)
    k3 = kn.reshape(nch, CHUNK, GDN_DK)
    v3 = va.reshape(nch, CHUNK, GDN_DV)
    gt = gt_ref[...].reshape(nch, CHUNK, LANE)
    beta = jnp.sum(jnp.where(lane3 == G_GDB + h, _sigmoid(gt), 0.0), axis=2, keepdims=True)
    gval = -jnp.exp(al_ref[...]) * _softplus(gt + dtb_ref[...])
    gc = _cumsum_rows(jnp.sum(jnp.where(lane3 == G_GDA + h, gval, 0.0), axis=2, keepdims=True),
                      tri_b, lane3)
    decay = jnp.exp(jnp.where(causal, _outer_sum(gc, -gc, lane3), -jnp.inf))
    p = jnp.where(strict, -(beta * _bmm_nt(k3, k3) * decay), 0.0)
    inv = eye + p
    ph, pl_ = _split2(p)
    for _ in range(1, lg_shift):
        ph, pl_ = _split2(_bmm(ph, ph) + (_bmm(ph, pl_) + _bmm(pl_, ph)))
        ih, il = _split2(inv)
        inv = inv + (_bmm(ih, ph) + (_bmm(ih, pl_) + _bmm(il, ph)))
    egc = jnp.exp(gc)
    u_pre = _bmm2(inv, beta * v3)
    w = _bmm2(inv, (beta * egc) * k3)
    qk = _bmm_nt(q3, k3) * decay
    q_dec = q3 * egc
    g_end = _last_rows(gc, g, lg)
    k_dec = k3 * jnp.exp(g_end - gc)

    def emit(rows, o):
        hn = o * lax.rsqrt(jnp.mean(o * o, axis=1, keepdims=True) + RMS_EPS) * nw_ref[...]
        y_ref[rows, :] = (hn * _silu(z_ref[rows, :])).astype(y_ref.dtype)

    if g == 1:
        st = s_s[0]
        for c in range(nch):
            u = u_pre[c] - _dot(w[c], st)
            emit(slice(c * CHUNK, (c + 1) * CHUNK), _dot(q_dec[c], st) + _dot(qk[c], u))
            st = jnp.exp(g_end[c]) * st + _dot_tn(k_dec[c], u)
        s_s[0] = st
    else:
        rows_i = lax.broadcasted_iota(jnp.int32, (CHUNK, 1), 0)
        us, os_ = [], []
        for i in range(g):
            sl = slice(i * lg, (i + 1) * lg)
            us.append(u_pre[0, sl] - _dot(w[0, sl], s_s[i]))
            os_.append(_dot(q_dec[0, sl], s_s[i]))
        u = jnp.concatenate(us, axis=0)
        emit(slice(0, CHUNK), jnp.concatenate(os_, axis=0) + _dot(qk[0], u))
        for i in range(g):
            kdi = jnp.where((rows_i >> lg_shift) == i, k_dec[0], 0.0)
            s_s[i] = jnp.exp(g_end[0, i * lg:i * lg + 1, :]) * s_s[i] + _dot_tn(kdi, u)

    @pl.when(t == pl.num_programs(2) - 1)
    def _():
        s_out[:, 0] = s_s[...]
        tq_out[...] = tq
        tk_out[...] = tk
        tv_out[...] = tv


def _rglru_body(r, g, x_ref, yg_ref, cw_ref, cb_ref, wa_ref, wx_ref, ba_ref, bx_ref, lam_ref,
                buf_ref, h0_ref, y_ref, h_out, tail_out, h_s, px_s):
    lseq = r if g == 1 else CHUNK // g
    t = pl.program_id(2)

    @pl.when(t == 0)
    def _():
        h_s[...] = h0_ref[:, 0]
        px_s[...] = buf_ref[...]

    xc, tail = _conv_rows(x_ref[...], px_s[...], cw_ref[...], g, lseq)
    px_s[...] = tail
    xc = xc + cb_ref[...]
    rg = _sigmoid(_dot(xc, wa_ref[0]) + ba_ref[...])
    ig = _sigmoid(_dot(xc, wx_ref[0]) + bx_ref[...])
    log_a = -LRU_C * rg * _softplus(-lam_ref[...])
    a = jnp.exp(log_a)
    u = jnp.sqrt(1.0 - jnp.exp(2.0 * log_a)) * (ig * xc)
    pos = lax.broadcasted_iota(jnp.int32, (r, 1), 0) & (SUBLANE - 1)
    sh = 1
    while sh < SUBLANE:
        a_sh = pltpu.roll(a, sh, 0)
        u_sh = pltpu.roll(u, sh, 0)
        msk = pos >= sh
        u = jnp.where(msk, a * u_sh + u, u)
        a = jnp.where(msk, a * a_sh, a)
        sh *= 2
    blocks = []
    for i in range(g):
        carry = h_s[i]
        for k in range(lseq // SUBLANE):
            rows = slice(i * lseq + k * SUBLANE, i * lseq + (k + 1) * SUBLANE)
            blk = u[rows] + a[rows] * carry
            blocks.append(blk)
            carry = blk[SUBLANE - 1:, :]
        h_s[i] = carry
    hh = jnp.concatenate(blocks, axis=0)
    yg = yg_ref[...]
    gelu = 0.5 * yg * (1.0 + jnp.tanh(math.sqrt(2.0 / math.pi) * (yg + 0.044715 * (yg * yg * yg))))
    y_ref[...] = (gelu * hh).astype(y_ref.dtype)

    @pl.when(t == pl.num_programs(2) - 1)
    def _():
        h_out[:, 0] = h_s[...]
        tail_out[...] = tail


class _Group:
    def __init__(self, n_seq, seq_len, row0):
        self.n_seq, self.seq_len, self.row0 = n_seq, seq_len, row0
        self.rows = n_seq * seq_len
        if seq_len >= 4 * CHUNK:
            self.r, self.g = 4 * CHUNK, 1
        elif seq_len >= CHUNK:
            self.r, self.g = CHUNK, 1
        else:
            self.r, self.g = CHUNK, CHUNK // seq_len
        self.s = self.g
        self.nb = n_seq // self.s
        self.nt = seq_len // self.r if self.g == 1 else 1
        assert seq_len >= CONV_W - 1 and row0 % self.r == 0 and self.rows % self.r == 0
        assert n_seq % self.s == 0 and self.r & (self.r - 1) == 0
        self.rb0 = row0 // self.r

    def rb(self, b, t):
        return self.rb0 + b * self.nt + t


def _mixer_call(body, grp, l, depth, m_total, in_specs, args, y_width, y_block, state_shapes, extra_outs,
                scratch, prev, name):
    s = grp.s
    out_shape = [jax.ShapeDtypeStruct((m_total, y_width), BF16)]
    out_specs = [pl.BlockSpec((grp.r, y_block), lambda b, h, t: (grp.rb(b, t), h))]
    for tail in state_shapes:
        out_shape.append(jax.ShapeDtypeStruct((depth, grp.n_seq, y_width // y_block) + tail, F32))
        out_specs.append(pl.BlockSpec((None, s, 1) + tail, lambda b, h, t: (l, b, h) + (0,) * len(tail)))
    for shp, spec in extra_outs:
        out_shape.append(shp)
        out_specs.append(spec)
    n_in = len(args)
    aliases = {}
    in_specs = list(in_specs)
    args = list(args)
    for j, arr in enumerate(prev):
        if arr is not None:
            aliases[len(args)] = j
            in_specs.append(pl.BlockSpec(memory_space=pl.ANY))
            args.append(arr)
    n_alias = len(args) - n_in

    def wrapped(*refs):
        body(*refs[:n_in], *refs[n_in + n_alias:])

    return pl.pallas_call(
        wrapped,
        out_shape=out_shape,
        grid=(grp.nb, y_width // y_block, grp.nt),
        in_specs=in_specs,
        out_specs=out_specs,
        scratch_shapes=scratch,
        input_output_aliases=aliases,
        compiler_params=_cparams(("arbitrary",) * 3),
        name=name,
    )(*args)


def _mlstm_call(grp, l, depth, m_total, hmain, gates, gate_bias, norm_w, c0, n0, m0, prev):
    r, s = grp.r, grp.s
    row = lambda off, w: pl.BlockSpec((r, w), lambda b, h, t: (grp.rb(b, t), off // w + h))
    st = lambda *tail: pl.BlockSpec((None, s, 1) + tail, lambda b, h, t: (l, b, h, 0, 0))
    in_specs = [row(C_MLQ, ML_DK), row(C_MLK, ML_DK), row(C_MLV, ML_DV), row(C_MLO, ML_DV),
                pl.BlockSpec((r, LANE), lambda b, h, t: (grp.rb(b, t), 0)),
                pl.BlockSpec((1, LANE), lambda b, h, t: (0, 0)),
                pl.BlockSpec((1, ML_DV), lambda b, h, t: (0, h)),
                st(ML_DV, ML_DK), st(1, ML_DK), st(1, 1)]
    scratch = [pltpu.VMEM((s, ML_DV, ML_DK), F32), pltpu.VMEM((s, 1, ML_DK), F32),
               pltpu.VMEM((s, 1, LANE), F32)]
    return _mixer_call(functools.partial(_mlstm_body, r, grp.g), grp, l, depth, m_total, in_specs,
                       [hmain, hmain, hmain, hmain, gates, gate_bias, norm_w, c0, n0, m0],
                       ML_HEADS * ML_DV, ML_DV, [(ML_DV, ML_DK), (1, ML_DK), (1, 1)], [], scratch, prev, "mlstm")


def _gdn_call(grp, l, depth, m_total, hmain, gates, conv_w, a_log_row, dt_bias_row, norm_w, bufpad, s0, prev):
    r, s = grp.r, grp.s
    row = lambda off, w: pl.BlockSpec((r, w), lambda b, h, t: (grp.rb(b, t), off // w + h))
    cw = lambda off, w: pl.BlockSpec((CONV_W, w), lambda b, h, t: (0, off // w + h))
    bf = lambda off, w: pl.BlockSpec((s * SUBLANE, w), lambda b, h, t: (b, off // w + h))
    one = pl.BlockSpec((1, LANE), lambda b, h, t: (0, 0))
    in_specs = [row(C_GDQ, GDN_DK), row(C_GDK, GDN_DK), row(C_GDV, GDN_DV), row(C_GDZ, GDN_DV),
                pl.BlockSpec((r, LANE), lambda b, h, t: (grp.rb(b, t), 1)),
                cw(0, GDN_DK), cw(GDN_QK, GDN_DK), cw(2 * GDN_QK, GDN_DV),
                one, one, pl.BlockSpec((1, GDN_DV), lambda b, h, t: (0, 0)),
                bf(0, GDN_DK), bf(GDN_QK, GDN_DK), bf(2 * GDN_QK, GDN_DV),
                pl.BlockSpec((None, s, 1, GDN_DK, GDN_DV), lambda b, h, t: (l, b, h, 0, 0))]
    nrow = grp.n_seq * SUBLANE
    tail = lambda w: (jax.ShapeDtypeStruct((nrow, GDN_HEADS * w), F32),
                      pl.BlockSpec((s * SUBLANE, w), lambda b, h, t: (b, h)))
    scratch = [pltpu.VMEM((s, GDN_DK, GDN_DV), F32),
               pltpu.VMEM((s * SUBLANE, GDN_DK), F32), pltpu.VMEM((s * SUBLANE, GDN_DK), F32),
               pltpu.VMEM((s * SUBLANE, GDN_DV), F32)]
    return _mixer_call(functools.partial(_gdn_body, r, grp.g), grp, l, depth, m_total, in_specs,
                       [hmain, hmain, hmain, hmain, gates, conv_w, conv_w, conv_w, a_log_row, dt_bias_row,
                        norm_w, bufpad, bufpad, bufpad, s0],
                       GDN_HEADS * GDN_DV, GDN_DV, [(GDN_DK, GDN_DV)],
                       [tail(GDN_DK), tail(GDN_DK), tail(GDN_DV)], scratch, prev, "gdn")


def _rglru_call(grp, l, depth, m_total, hmain, conv_w, conv_b, w_a, w_x, b_a, b_x, lam, bufpad, h0, prev):
    r, s = grp.r, grp.s
    w = LRU_BLOCK
    row = lambda off: pl.BlockSpec((r, w), lambda b, h, t: (grp.rb(b, t), off // w + h))
    vec = pl.BlockSpec((1, w), lambda b, h, t: (0, h))
    blk = pl.BlockSpec((1, w, w), lambda b, h, t: (h, 0, 0))
    in_specs = [row(C_RGX), row(C_RGY),
                pl.BlockSpec((CONV_W, w), lambda b, h, t: (0, h)), vec, blk, blk, vec, vec, vec,
                pl.BlockSpec((s * SUBLANE, w), lambda b, h, t: (b, h)),
                pl.BlockSpec((None, s, 1, 1, w), lambda b, h, t: (l, b, h, 0, 0))]
    tail = (jax.ShapeDtypeStruct((grp.n_seq * SUBLANE, LRU_WIDTH), F32),
            pl.BlockSpec((s * SUBLANE, w), lambda b, h, t: (b, h)))
    scratch = [pltpu.VMEM((s, 1, w), F32), pltpu.VMEM((s * SUBLANE, w), F32)]
    return _mixer_call(functools.partial(_rglru_body, r, grp.g), grp, l, depth, m_total, in_specs,
                       [hmain, hmain, conv_w, conv_b, w_a, w_x, b_a, b_x, lam, bufpad, h0],
                       LRU_WIDTH, w, [(1, w)], [tail], scratch, prev, "rglru")


def _pad_conv_state(buf):
    b, k, c = buf.shape
    return jnp.pad(buf.astype(F32), ((0, 0), (SUBLANE - k, 0), (0, 0))).reshape(b * SUBLANE, c)


def _unpad_conv_state(tail, n_seq):
    return tail.reshape(n_seq, SUBLANE, -1)[:, SUBLANE - (CONV_W - 1):, :]


R_I1, R_I2, R_W1, R_W2, R_P1, R_P2 = 0, 1, 2, 3, 4, 5
MOE_TM = 512
MOE_TD = 512


def _router_call(x, router, tm):
    m, d = x.shape
    n_exp = router.shape[1]
    wr = jnp.pad(router.astype(F32), ((0, 0), (0, LANE - n_exp)))

    def body(x_ref, w_ref, o_ref, cnt_ref, carry):
        i = pl.program_id(0)

        @pl.when(i == 0)
        def _():
            carry[...] = jnp.zeros_like(carry)

        xh, xl = _split2(x_ref[...])
        wh, wl = _split2(w_ref[...])
        logits = _dot(xh, wh) + (_dot(xh, wl) + _dot(xl, wh))
        lane = lax.broadcasted_iota(jnp.int32, logits.shape, 1)
        lg = jnp.where(lane < n_exp, logits, -jnp.inf)
        v1 = jnp.max(lg, axis=1, keepdims=True)
        i1 = jnp.min(jnp.where(lg == v1, lane, LANE), axis=1, keepdims=True)
        lg2 = jnp.where(lane == i1, -jnp.inf, lg)
        v2 = jnp.max(lg2, axis=1, keepdims=True)
        i2 = jnp.min(jnp.where(lg2 == v2, lane, LANE), axis=1, keepdims=True)
        e2 = jnp.exp(v2 - v1)
        w1 = 1.0 / (1.0 + e2)
        w2 = e2 / (1.0 + e2)
        sel = ((lane == i1) | (lane == i2)).astype(F32)
        below = (lax.broadcasted_iota(jnp.int32, (tm, tm), 1)
                 < lax.broadcasted_iota(jnp.int32, (tm, tm), 0)).astype(BF16)
        rank = jnp.dot(below, sel.astype(BF16), preferred_element_type=F32) + carry[0:1, :]
        p1 = jnp.sum(jnp.where(lane == i1, rank, 0.0), axis=1, keepdims=True)
        p2 = jnp.sum(jnp.where(lane == i2, rank, 0.0), axis=1, keepdims=True)
        rec = jnp.where(lane == R_I1, i1.astype(F32), jnp.where(lane == R_I2, i2.astype(F32), 0.0))
        rec = jnp.where(lane == R_W1, w1, jnp.where(lane == R_W2, w2, rec))
        o_ref[...] = jnp.where(lane == R_P1, p1, jnp.where(lane == R_P2, p2, rec))
        carry[...] = carry[...] + jnp.sum(sel, axis=0, keepdims=True)
        cnt_ref[...] = carry[...]

    return pl.pallas_call(
        body,
        out_shape=[jax.ShapeDtypeStruct((m, LANE), F32), jax.ShapeDtypeStruct((SUBLANE, LANE), F32)],
        grid=(m // tm,),
        in_specs=[pl.BlockSpec((tm, d), lambda i: (i, 0)), pl.BlockSpec((d, LANE), lambda i: (0, 0))],
        out_specs=[pl.BlockSpec((tm, LANE), lambda i: (i, 0)), pl.BlockSpec((SUBLANE, LANE), lambda i: (0, 0))],
        scratch_shapes=[pltpu.VMEM((SUBLANE, LANE), F32)],
        compiler_params=_cparams(("arbitrary",)),
        name="router",
    )(x, wr)


def _source_table(starts, e_arr, p_arr, m, ns):
    td = MOE_TD

    def body(start_ref, e_ref, p_ref, dst_ref, tbl_ref):
        i = pl.program_id(0)

        @pl.when(i == 0)
        def _():
            def clear(r, c):
                tbl_ref[r] = 0
                return c

            lax.fori_loop(0, ns, clear, 0, unroll=8)

        base = i * td

        def put(j, c):
            row = start_ref[e_ref[0, j]] + p_ref[0, j]
            dst_ref[0, j] = row
            tbl_ref[row] = base + (j >> 1)
            return c

        lax.fori_loop(0, 2 * td, put, 0, unroll=8)

    smem = pl.BlockSpec((None, 1, 2 * td), lambda i, s: (i, 0, 0), memory_space=pltpu.SMEM)
    return pl.pallas_call(
        body,
        grid_spec=pltpu.PrefetchScalarGridSpec(
            num_scalar_prefetch=1, grid=(m // td,), in_specs=[smem, smem],
            out_specs=[smem, pl.BlockSpec(memory_space=pltpu.SMEM)]),
        out_shape=[jax.ShapeDtypeStruct(e_arr.shape, jnp.int32), jax.ShapeDtypeStruct((ns,), jnp.int32)],
        compiler_params=_cparams(("arbitrary",)),
        name="moe_table",
    )(starts, e_arr, p_arr)


def _start_rows(copy_of, n_rows):
    def issue(r, c):
        copy_of(2 * r).start(priority=0)
        copy_of(2 * r + 1).start(priority=1)
        return c

    lax.fori_loop(0, n_rows // 2, issue, 0, unroll=4)


def _wait_rows(src_ref, dst_ref, sem):
    pltpu.make_async_copy(src_ref.at[pl.ds(0, dst_ref.shape[0])], dst_ref, sem).wait()


def _dispatch(x, table, n_used):
    m, d = x.shape
    ns = table.shape[0]
    tm = MOE_TM
    nt = ns // tm

    def body(nu_ref, src_ref, nxt_ref, x_ref, o_ref, buf, sem):
        i = pl.program_id(0)
        slot = i & 1

        def rows(idx_ref, s):
            return lambda r: pltpu.make_async_copy(x_ref.at[pl.ds(idx_ref[0, r], 1)],
                                                   buf.at[s, pl.ds(r, 1)], sem.at[s])

        @pl.when(i == 0)
        def _():
            _start_rows(rows(src_ref, 0), tm)

        @pl.when(i + 1 < nu_ref[0])
        def _():
            _start_rows(rows(nxt_ref, 1 - slot), tm)

        @pl.when(i < nu_ref[0])
        def _():
            def drain(r, c):
                rows(src_ref, slot)(r).wait()
                return c

            lax.fori_loop(0, tm, drain, 0, unroll=8)
            o_ref[...] = buf[slot].astype(BF16)

    idx = lambda off: pl.BlockSpec((None, 1, tm), lambda i, nu: (jnp.minimum(i + off, nt - 1), 0, 0),
                                   memory_space=pltpu.SMEM)
    tbl = table.reshape(nt, 1, tm)
    return pl.pallas_call(
        body,
        grid_spec=pltpu.PrefetchScalarGridSpec(
            num_scalar_prefetch=1, grid=(nt,),
            in_specs=[idx(0), idx(1), pl.BlockSpec(memory_space=pl.ANY)],
            out_specs=pl.BlockSpec((tm, d), lambda i, nu: (jnp.minimum(i, nu[0] - 1), 0)),
            scratch_shapes=[pltpu.VMEM((2, tm, d), F32), pltpu.SemaphoreType.DMA((2,))]),
        out_shape=jax.ShapeDtypeStruct((ns, d), BF16),
        compiler_params=_cparams(("arbitrary",)),
        name="moe_dispatch",
    )(n_used, tbl, tbl, x)


def _grouped(rows, ws, tile_expert, tile_first, n_used, epilogue, out_dtype, n_cols, tn, name, w_buffers=2):
    ns, k = rows.shape
    nt = ns // MOE_TM
    used = lambda i, nu: jnp.minimum(i, nu[0] - 1)
    w_spec = pl.BlockSpec((None, k, tn), lambda n, i, te, tf, nu: (te[i], 0, n),
                          pipeline_mode=pl.Buffered(w_buffers))

    def body(te_ref, tf_ref, nu_ref, x_ref, *refs):
        w_refs, o_ref, wb_refs = refs[:len(ws)], refs[len(ws)], refs[len(ws) + 1:]
        i = pl.program_id(1)

        @pl.when(tf_ref[i] == 1)
        def _():
            for w, wb in zip(w_refs, wb_refs):
                wb[...] = w[...].astype(BF16)

        @pl.when(i < nu_ref[0])
        def _():
            xv = x_ref[...].astype(BF16)
            accs = [jnp.dot(xv, wb[...], preferred_element_type=F32) for wb in wb_refs]
            o_ref[...] = epilogue(accs).astype(o_ref.dtype)

    return pl.pallas_call(
        body,
        grid_spec=pltpu.PrefetchScalarGridSpec(
            num_scalar_prefetch=3, grid=(n_cols // tn, nt),
            in_specs=[pl.BlockSpec((MOE_TM, k), lambda n, i, te, tf, nu: (used(i, nu), 0))] + [w_spec] * len(ws),
            out_specs=pl.BlockSpec((MOE_TM, tn), lambda n, i, te, tf, nu: (used(i, nu), n)),
            scratch_shapes=[pltpu.VMEM((k, tn), BF16) for _ in ws]),
        out_shape=jax.ShapeDtypeStruct((ns, n_cols), out_dtype),
        compiler_params=_cparams(("arbitrary", "arbitrary")),
        name=name,
    )(tile_expert, tile_first, n_used, rows, *ws)


def _combine_ln(x, ys, dst_rows, rec, ple, gamma, beta, split_rows):
    tm = MOE_TD
    m, n = x.shape
    n_tiles = m // tm
    tiles_a = None if split_rows is None else split_rows // tm

    def body(ds_ref, dn_ref, ys_ref, x_ref, r_ref, e_ref, g_ref, b_ref, oa_ref, ob_ref, ybuf, sem):
        i = pl.program_id(0)
        slot = i & 1

        def rows(d_ref, s):
            return lambda j: pltpu.make_async_copy(ys_ref.at[pl.ds(d_ref[0, j], 1)],
                                                   ybuf.at[s, j & 1, pl.ds(j >> 1, 1)], sem.at[s])

        @pl.when(i == 0)
        def _():
            _start_rows(rows(ds_ref, 0), 2 * tm)

        @pl.when(i + 1 < n_tiles)
        def _():
            _start_rows(rows(dn_ref, 1 - slot), 2 * tm)

        for k in range(2):
            _wait_rows(ys_ref, ybuf.at[slot, k], sem.at[slot])
        r = r_ref[...]
        lane = lax.broadcasted_iota(jnp.int32, r.shape, 1)
        w1 = jnp.sum(jnp.where(lane == R_W1, r, 0.0), axis=1, keepdims=True)
        w2 = jnp.sum(jnp.where(lane == R_W2, r, 0.0), axis=1, keepdims=True)
        y = DEEPNORM_ALPHA * x_ref[...] + (w1 * ybuf[slot, 0] + w2 * ybuf[slot, 1]) + e_ref[...]
        mu = jnp.mean(y, axis=-1, keepdims=True)
        yc = y - mu
        var = jnp.mean(yc * yc, axis=-1, keepdims=True)
        out = yc * lax.rsqrt(var + LN_EPS) * g_ref[...] + b_ref[...]
        if split_rows is None:
            oa_ref[...] = out
            ob_ref[...] = out.astype(BF16)
        else:
            @pl.when(i < tiles_a)
            def _():
                oa_ref[...] = out

            @pl.when(i >= tiles_a)
            def _():
                ob_ref[...] = out

    row = lambda i: (i, 0)
    vec = pl.BlockSpec((1, n), lambda i: (0, 0))
    smem = lambda off: pl.BlockSpec((None, 1, 2 * tm), lambda i: (jnp.minimum(i + off, n_tiles - 1), 0, 0),
                                    memory_space=pltpu.SMEM)
    if split_rows is None:
        out_shape = [jax.ShapeDtypeStruct((m, n), F32), jax.ShapeDtypeStruct((m, n), BF16)]
        out_specs = [pl.BlockSpec((tm, n), row), pl.BlockSpec((tm, n), row)]
    else:
        out_shape = [jax.ShapeDtypeStruct((split_rows, n), F32), jax.ShapeDtypeStruct((m - split_rows, n), F32)]
        out_specs = [pl.BlockSpec((tm, n), lambda i: (jnp.minimum(i, tiles_a - 1), 0)),
                     pl.BlockSpec((tm, n), lambda i: (jnp.maximum(i - tiles_a, 0), 0))]
    return pl.pallas_call(
        body,
        grid=(n_tiles,),
        in_specs=[smem(0), smem(1), pl.BlockSpec(memory_space=pl.ANY),
                  pl.BlockSpec((tm, n), row), pl.BlockSpec((tm, LANE), row), pl.BlockSpec((tm, n), row), vec, vec],
        out_specs=out_specs,
        scratch_shapes=[pltpu.VMEM((2, 2, tm, n), F32), pltpu.SemaphoreType.DMA((2,))],
        out_shape=out_shape,
        compiler_params=_cparams(("arbitrary",)),
        name="moe_combine_ln",
    )(dst_rows, dst_rows, ys, x, rec, ple, gamma.reshape(1, n), beta.reshape(1, n))


def _moe_ln(x, ple, router, w1, w3, w2, j, gamma, beta, split_rows):
    m, d = x.shape
    _, ne, _, fe = w1.shape
    w1, w3, w2 = (w.reshape((-1,) + w.shape[2:]) for w in (w1, w3, w2))
    rec, counts = _router_call(x, router, 512)
    cnt = counts[0, :ne].astype(jnp.int32)
    tiles_e = (cnt + MOE_TM - 1) // MOE_TM
    tile_end = jnp.cumsum(tiles_e)
    tile_beg = tile_end - tiles_e
    n_used = tile_end[ne - 1]
    nt = (2 * m) // MOE_TM + ne
    ids = jnp.minimum(jnp.arange(nt, dtype=jnp.int32), n_used - 1)
    tile_expert = jnp.sum((ids[:, None] >= tile_end[None, :]).astype(jnp.int32), axis=1)
    tile_first = (jnp.arange(nt, dtype=jnp.int32) == tile_beg[tile_expert]).astype(jnp.int32)
    starts = (tile_beg * MOE_TM).astype(jnp.int32)
    slots = lambda a, b: jnp.stack([rec[:, a], rec[:, b]], axis=1).astype(jnp.int32).reshape(m // MOE_TD, 1, 2 * MOE_TD)
    e_arr, p_arr = slots(R_I1, R_I2), slots(R_P1, R_P2)
    ns = nt * MOE_TM

    nu = n_used.reshape(1)
    dst_rows, table = _source_table(starts, e_arr, p_arr, m, ns)
    xs = _dispatch(x, table, nu)
    tile_w = tile_expert + j * ne
    hmid = _grouped(xs, [w1, w3], tile_w, tile_first, nu, lambda a: _silu(a[0]) * a[1], BF16, fe, fe // 2,
                    "moe_up", w_buffers=1)
    ys = _grouped(hmid, [w2], tile_w, tile_first, nu, lambda a: a[0], F32, d, 1024, "moe_down")
    return _combine_ln(x, ys, dst_rows, rec, ple, gamma, beta, split_rows)


def _run_group_mixers(grp, l, depth, m_total, hmain, gates, st, prm, prev):
    b = grp.n_seq
    ya, ml_c, ml_n, ml_m = _mlstm_call(
        grp, l, depth, m_total, hmain, gates, prm["gate_bias"], prm["ml_norm_w"],
        st["ml_C"].astype(F32), st["ml_n"].astype(F32).reshape(depth, b, ML_HEADS, 1, ML_DK),
        st["ml_m"].astype(F32).reshape(depth, b, ML_HEADS, 1, 1),
        [prev["ya"], prev["ml_C"], prev["ml_n"], prev["ml_m"]])
    yb, rg_h, rg_tail = _rglru_call(
        grp, l, depth, m_total, hmain, prm["rg_conv_w"], prm["rg_conv_b"], prm["rg_w_a"], prm["rg_w_x"],
        prm["rg_b_a"], prm["rg_b_x"], prm["rg_lambda"], _pad_conv_state(st["rg_conv"][l]),
        st["rg_h"].astype(F32).reshape(depth, b, LRU_BLOCKS, 1, LRU_BLOCK), [prev["yb"], prev["rg_h"]])
    yc, gd_s, tq, tk, tv = _gdn_call(
        grp, l, depth, m_total, hmain, gates, prm["gd_conv_w"], prm["gd_a_log"], prm["gd_dt_bias"],
        prm["gd_norm_w"], _pad_conv_state(st["gd_conv"][l]), st["gd_S"].astype(F32),
        [prev["yc"], prev["gd_S"]])
    out = dict(ya=ya, yb=yb, yc=yc, ml_C=ml_c, ml_n=ml_n, ml_m=ml_m, rg_h=rg_h, gd_S=gd_s)
    tails = dict(rg_conv=_unpad_conv_state(rg_tail, b),
                 gd_conv=jnp.concatenate([_unpad_conv_state(x, b) for x in (tq, tk, tv)], axis=-1))
    return out, tails


_STATE_KEYS = ("ml_C", "ml_n", "ml_m", "rg_h", "gd_S")


def kernel(x_prompt, x_sample, state_mlstm_C, state_mlstm_n, state_mlstm_m, state_rglru_h, state_rglru_conv, state_gdn_S, state_gdn_conv, p_prompt, p_sample, w_in, ml_b_i, ml_b_f, ml_norm_w, rg_conv_w, rg_conv_b, rg_w_a, rg_b_a, rg_w_x, rg_b_x, rg_lambda, gd_conv_w, gd_A_log, gd_dt_bias, gd_norm_w, w_up_mlstm, w_up_rglru, w_up_gdn, w_out, ln1_g, ln1_b, ffn_w1, ffn_w3, ffn_w2, moe_router, moe_w1, moe_w3, moe_w2, ple_w, ple_gate_w, ln2_g, ln2_b):
    bp, tp, d = x_prompt.shape
    bs, ts, _ = x_sample.shape
    depth = w_in.shape[0]
    gp = _Group(bp, tp, 0)
    gs = _Group(bs, ts, bp * tp)
    m = gp.rows + gs.rows
    pd = x_prompt.dtype

    zeros = lambda *shape: jnp.zeros(shape, F32)
    st_p = dict(ml_C=zeros(depth, bp, ML_HEADS, ML_DV, ML_DK), ml_n=zeros(depth, bp, ML_HEADS, ML_DK),
                ml_m=zeros(depth, bp, ML_HEADS), rg_h=zeros(depth, bp, LRU_WIDTH),
                rg_conv=zeros(depth, bp, CONV_W - 1, LRU_WIDTH),
                gd_S=zeros(depth, bp, GDN_HEADS, GDN_DK, GDN_DV),
                gd_conv=zeros(depth, bp, CONV_W - 1, GDN_CONV_DIM))
    st_s = dict(ml_C=state_mlstm_C, ml_n=state_mlstm_n, ml_m=state_mlstm_m, rg_h=state_rglru_h,
                rg_conv=state_rglru_conv, gd_S=state_gdn_S, gd_conv=state_gdn_conv)

    x = jnp.concatenate([x_prompt.reshape(gp.rows, d), x_sample.reshape(gs.rows, d)], axis=0).astype(F32)
    xb = x.astype(BF16)
    w_in_t = jnp.swapaxes(w_in, 1, 2)
    lane_row = lambda v, off: jnp.zeros((1, LANE), F32).at[0, off:off + v.shape[0]].set(v.astype(F32))
    acc_p = {k: None for k in _STATE_KEYS}
    acc_s = {k: None for k in _STATE_KEYS}
    tails_p, tails_s = [], []

    for l in range(depth):
        hmain = _in_proj(xb, w_in_t, l, 1024)
        gates = _gate_proj(xb, w_in_t, l, 1024)

        prm = dict(
            gate_bias=lane_row(ml_b_i[l], G_MLI) + lane_row(ml_b_f[l], G_MLF),
            ml_norm_w=ml_norm_w[l].astype(F32).reshape(1, -1),
            rg_conv_w=rg_conv_w[l].astype(F32), rg_conv_b=rg_conv_b[l].astype(F32).reshape(1, -1),
            rg_w_a=rg_w_a[l], rg_w_x=rg_w_x[l], rg_b_a=rg_b_a[l].astype(F32).reshape(1, -1),
            rg_b_x=rg_b_x[l].astype(F32).reshape(1, -1), rg_lambda=rg_lambda[l].astype(F32).reshape(1, -1),
            gd_conv_w=gd_conv_w[l].astype(F32), gd_a_log=lane_row(gd_A_log[l], G_GDA),
            gd_dt_bias=lane_row(gd_dt_bias[l], G_GDA), gd_norm_w=gd_norm_w[l].astype(F32).reshape(1, -1))
        out_p, tl_p = _run_group_mixers(gp, l, depth, m, hmain, gates, st_p, prm,
                                        dict(acc_p, ya=None, yb=None, yc=None))
        out_s, tl_s = _run_group_mixers(gs, l, depth, m, hmain, gates, st_s, prm,
                                        dict(acc_s, ya=out_p["ya"], yb=out_p["yb"], yc=out_p["yc"]))
        acc_p = {k: out_p[k] for k in _STATE_KEYS}
        acc_s = {k: out_s[k] for k in _STATE_KEYS}
        tails_p.append(tl_p)
        tails_s.append(tl_s)
        ya, yb, yc = out_s["ya"], out_s["yb"], out_s["yc"]

        tn, tm = 1024, 512
        up_spec = pl.BlockSpec((None, ya.shape[1], tn), lambda n, i: (l, 0, n))
        mg_spec = lambda br: pl.BlockSpec((tm, tn), lambda n, i: (i, (C_MG + br * d) // tn + n))
        merged = _proj([ya, yb, yc], [w_up_mlstm, w_up_rglru, w_up_gdn], [up_spec] * 3,
                       [hmain] * 3, [mg_spec(0), mg_spec(1), mg_spec(2)],
                       lambda accs, ex, ids: [_sigmoid(ex[0]) * accs[0] + _sigmoid(ex[1]) * accs[1]
                                              + _sigmoid(ex[2]) * accs[2]],
                       [jax.ShapeDtypeStruct((m, d), BF16)], d, tm, tn, "merge")[0]
        x, xb = _rowfull_ln(merged, w_out[l].astype(BF16), x, None, ln1_g[l].astype(F32), ln1_b[l].astype(F32),
                            512, d, "out_proj_ln")

        p_l = jnp.concatenate([p_prompt[l].reshape(gp.rows, -1), p_sample[l].reshape(gs.rows, -1)], axis=0)
        ple = _proj([xb, p_l], [ple_gate_w, ple_w],
                    [pl.BlockSpec((None, d, 1024), lambda n, i: (l, 0, n)),
                     pl.BlockSpec((None, P_DIM, 1024), lambda n, i: (l, 0, n))],
                    [], [], lambda accs, ex, ids: [_sigmoid(accs[0]) * accs[1]],
                    [jax.ShapeDtypeStruct((m, d), F32)], d, 1024, 1024, "ple")[0]
        j = l // 2
        split = gp.rows if l == depth - 1 else None
        g2, b2 = ln2_g[l].astype(F32), ln2_b[l].astype(F32)
        if l % 2 == 0:
            ff = ffn_w1.shape[2]
            hmid = _proj([xb], [ffn_w1, ffn_w3], [pl.BlockSpec((None, d, 512), lambda n, i: (j, 0, n))] * 2,
                         [], [], lambda accs, ex, ids: [_silu(accs[0]) * accs[1]],
                         [jax.ShapeDtypeStruct((m, ff), BF16)], ff, 1024, 512, "ffn_up", x_of_w=(0, 0))[0]
            x, xb = _rowfull_ln(hmid, ffn_w2[j].astype(BF16), x, ple, g2, b2, 256, ff, "ffn_down_ln",
                                split_rows=split)
        else:
            x, xb = _moe_ln(x, ple, moe_router[j], moe_w1, moe_w3, moe_w2, j, g2, b2, split)

    y_prompt = x.reshape(bp, tp, d).astype(pd)
    y_sample = xb.reshape(bs, ts, d).astype(x_sample.dtype)

    def finish(acc, tails, n_seq, dtypes):
        conv = {k: jnp.stack([tl[k] for tl in tails]) for k in ("rg_conv", "gd_conv")}
        vals = dict(ml_C=acc["ml_C"], ml_n=acc["ml_n"].reshape(depth, n_seq, ML_HEADS, ML_DK),
                    ml_m=acc["ml_m"].reshape(depth, n_seq, ML_HEADS),
                    rg_h=acc["rg_h"].reshape(depth, n_seq, LRU_WIDTH), rg_conv=conv["rg_conv"],
                    gd_S=acc["gd_S"], gd_conv=conv["gd_conv"])
        names = ("ml_C", "ml_n", "ml_m", "rg_h", "rg_conv", "gd_S", "gd_conv")
        return tuple(vals[k].astype(dtypes[k]) for k in names)

    return ((y_prompt, y_sample) + finish(acc_p, tails_p, bp, {k: pd for k in st_p})
            + finish(acc_s, tails_s, bs, {k: v.dtype for k, v in st_s.items()}))
```

```python
import functools
import math

import jax
import jax.numpy as jnp
from jax import lax
from jax.experimental import pallas as pl
from jax.experimental.pallas import tpu as pltpu

F32 = jnp.float32
BF16 = jnp.bfloat16

D_MODEL = 2048
DEPTH = 2
P_DIM = 256
CONV_W = 4
N_BRANCH = 3
ML_HEADS = 4
ML_DK = 128
ML_DV = 256
ML_GATE_CAP = 15.0
LRU_WIDTH = 1024
LRU_BLOCKS = 4
LRU_BLOCK = LRU_WIDTH // LRU_BLOCKS
LRU_C = 8.0
GDN_HEADS = 4
GDN_DK = 128
GDN_DV = 256
GDN_QK = GDN_HEADS * GDN_DK
GDN_CONV_DIM = 2 * GDN_QK + GDN_HEADS * GDN_DV
N_EXPERTS = 8
DEEPNORM_ALPHA = (2 * DEPTH) ** 0.25
LN_EPS = 1e-5
RMS_EPS = 1e-6
L2_EPS = 1e-6

LANE = 128
SUBLANE = 8
CHUNK = 128
VMEM_LIMIT = 56 * 1024 * 1024

C_MLQ, C_MLK, C_MLV, C_MLO = 0, 512, 1024, 2048
C_RGX, C_RGY = 3072, 4096
C_GDQ, C_GDK, C_GDV, C_GDZ = 5120, 5632, 6144, 7168
C_MG = 8192
N_MAIN = C_MG + N_BRANCH * D_MODEL
W_GATES_A, W_GATES_B = 2048, 7176
G_MLI, G_MLF, G_GDB, G_GDA = 0, 4, 8, 12
PROJ_TN = 1024


def _cparams(sem):
    return pltpu.CompilerParams(dimension_semantics=sem, vmem_limit_bytes=VMEM_LIMIT)


def _dot(a, b):
    return jnp.dot(a.astype(BF16), b.astype(BF16), preferred_element_type=F32)


def _dot_nt(a, b):
    return lax.dot_general(a.astype(BF16), b.astype(BF16), (((1,), (1,)), ((), ())),
                           preferred_element_type=F32)


def _dot_tn(a, b):
    return lax.dot_general(a.astype(BF16), b.astype(BF16), (((0,), (0,)), ((), ())),
                           preferred_element_type=F32)


def _bmm(a, b):
    return jnp.einsum('cij,cjk->cik', a.astype(BF16), b.astype(BF16), preferred_element_type=F32)


def _bmm_nt(a, b):
    return jnp.einsum('cik,cjk->cij', a.astype(BF16), b.astype(BF16), preferred_element_type=F32)


def _bmm_tn(a, b):
    return jnp.einsum('csi,csj->cij', a.astype(BF16), b.astype(BF16), preferred_element_type=F32)


def _sigmoid(x):
    return 1.0 / (1.0 + jnp.exp(-x))


def _silu(x):
    return x * _sigmoid(x)


def _softplus(x):
    return jnp.maximum(x, 0.0) + jnp.log1p(jnp.exp(-jnp.abs(x)))


def _proj(xs, ws, w_specs, extras, e_specs, epilogue, out_shapes, n_cols, tm, tn, name,
          lead_grid=(), x_of_w=None):
    m = xs[0].shape[0]
    nl = len(lead_grid)
    grid = tuple(lead_grid) + (n_cols // tn, m // tm)
    x_specs = [pl.BlockSpec((tm, x.shape[1]), lambda *g: (g[-1], 0)) for x in xs]
    o_specs = [pl.BlockSpec((tm, tn), lambda *g: (g[-1], g[-2]) if nl == 0 else
                            (g[-1], g[0] * (n_cols // tn) + g[-2])) for _ in out_shapes]
    scratch = [pltpu.VMEM(tuple(d for d in s.block_shape if d is not None), BF16) for s in w_specs]
    xw = tuple(range(len(ws))) if x_of_w is None else tuple(x_of_w)

    def body(*refs):
        nx, nw, ne, no = len(xs), len(ws), len(extras), len(out_shapes)
        x_refs = refs[:nx]
        w_refs = refs[nx:nx + nw]
        e_refs = refs[nx + nw:nx + nw + ne]
        o_refs = refs[nx + nw + ne:nx + nw + ne + no]
        wb_refs = refs[nx + nw + ne + no:]

        @pl.when(pl.program_id(nl + 1) == 0)
        def _():
            for w, wb in zip(w_refs, wb_refs):
                wb[...] = w[...].astype(BF16)

        xv = [x[...].astype(BF16) for x in x_refs]
        accs = [jnp.dot(xv[j], wb[...], preferred_element_type=F32) for j, wb in zip(xw, wb_refs)]
        outs = epilogue(accs, [e[...] for e in e_refs], [pl.program_id(i) for i in range(nl)])
        for o, v in zip(o_refs, outs):
            o[...] = v.astype(o.dtype)

    return pl.pallas_call(
        body,
        out_shape=out_shapes,
        grid=grid,
        in_specs=x_specs + list(w_specs) + list(e_specs),
        out_specs=o_specs,
        scratch_shapes=scratch,
        compiler_params=_cparams(("arbitrary",) * len(grid)),
        name=name,
    )(*xs, *ws, *extras)


def _in_proj(xb, wt, l, tm):
    m, d = xb.shape
    tn = PROJ_TN
    n_blocks = N_MAIN // tn
    first_a = W_GATES_A // tn
    first_b = (W_GATES_B - SUBLANE) // tn
    nxt = 2 * SUBLANE

    def body(x_ref, wm_ref, wn_ref, o_ref, wb_ref):
        n = pl.program_id(0)

        @pl.when(pl.program_id(1) == 0)
        def _():
            def shifted(s):
                wcat = jnp.concatenate([wm_ref[...], wn_ref[...]], axis=0)
                wb_ref[...] = wcat[s:s + tn, :].astype(BF16)

            @pl.when(n < first_a)
            def _():
                wb_ref[...] = wm_ref[...].astype(BF16)

            @pl.when((n >= first_a) & (n < first_b))
            def _():
                shifted(SUBLANE)

            @pl.when(n >= first_b)
            def _():
                shifted(2 * SUBLANE)

        o_ref[...] = lax.dot_general(x_ref[...], wb_ref[...], (((1,), (1,)), ((), ())),
                                     preferred_element_type=F32)

    return pl.pallas_call(
        body,
        out_shape=jax.ShapeDtypeStruct((m, N_MAIN), F32),
        grid=(n_blocks, m // tm),
        in_specs=[pl.BlockSpec((tm, d), lambda n, i: (i, 0)),
                  pl.BlockSpec((None, tn, d), lambda n, i: (l, n, 0)),
                  pl.BlockSpec((None, nxt, d), lambda n, i: (l, (n + 1) * (tn // nxt), 0))],
        out_specs=pl.BlockSpec((tm, tn), lambda n, i: (i, n)),
        scratch_shapes=[pltpu.VMEM((tn, d), BF16)],
        compiler_params=_cparams(("arbitrary", "arbitrary")),
        name="in_proj",
    )(xb, wt, wt)


def _gate_proj(xb, wt, l, tm):
    m, d = xb.shape
    blk_a, blk_b = W_GATES_A // LANE, W_GATES_B // LANE

    def body(x_ref, wa_ref, wb_ref, o_ref):
        w = jnp.concatenate([wa_ref[...], wb_ref[...]], axis=0).astype(BF16)
        o_ref[...] = lax.dot_general(x_ref[...], w, (((1,), (1,)), ((), ())), preferred_element_type=F32)

    return pl.pallas_call(
        body,
        out_shape=jax.ShapeDtypeStruct((m, 2 * LANE), F32),
        grid=(m // tm,),
        in_specs=[pl.BlockSpec((tm, d), lambda i: (i, 0)),
                  pl.BlockSpec((None, LANE, d), lambda i: (l, blk_a, 0)),
                  pl.BlockSpec((None, LANE, d), lambda i: (l, blk_b, 0))],
        out_specs=pl.BlockSpec((tm, 2 * LANE), lambda i: (i, 0)),
        compiler_params=_cparams(("arbitrary",)),
        name="gate_proj",
    )(xb, wt, wt)


def _rowfull_ln(h, w_bf16, res, extra, gamma, beta, tm, tk, name, split_rows=None):
    m, k = h.shape
    n = w_bf16.shape[1]
    nk = k // tk
    has_extra = extra is not None
    n_in = 6 if has_extra else 5
    tiles_a = None if split_rows is None else split_rows // tm

    def body(*refs):
        h_ref, w_ref, r_ref = refs[:3]
        e_ref = refs[3] if has_extra else None
        g_ref, b_ref = refs[n_in - 2:n_in]
        oa_ref, ob_ref, acc_ref = refs[n_in:]
        i = pl.program_id(0)
        kk = pl.program_id(1)

        @pl.when(kk == 0)
        def _():
            acc_ref[...] = jnp.zeros_like(acc_ref)

        acc_ref[...] += jnp.dot(h_ref[...], w_ref[...], preferred_element_type=F32)

        @pl.when(kk == nk - 1)
        def _():
            y = DEEPNORM_ALPHA * r_ref[...] + acc_ref[...]
            if e_ref is not None:
                y = y + e_ref[...].astype(F32)
            mu = jnp.mean(y, axis=-1, keepdims=True)
            yc = y - mu
            var = jnp.mean(yc * yc, axis=-1, keepdims=True)
            out = yc * lax.rsqrt(var + LN_EPS) * g_ref[...] + b_ref[...]
            if split_rows is None:
                oa_ref[...] = out
                ob_ref[...] = out.astype(BF16)
            else:
                @pl.when(i < tiles_a)
                def _():
                    oa_ref[...] = out

                @pl.when(i >= tiles_a)
                def _():
                    ob_ref[...] = out

    row = lambda i, j: (i, 0)
    in_specs = [pl.BlockSpec((tm, tk), lambda i, j: (i, j)),
                pl.BlockSpec((tk, n), lambda i, j: (j, 0), pipeline_mode=pl.Buffered(1 if nk == 1 else 2)),
                pl.BlockSpec((tm, n), row)]
    args = [h, w_bf16, res]
    if has_extra:
        in_specs.append(pl.BlockSpec((tm, n), row))
        args.append(extra)
    in_specs += [pl.BlockSpec((1, n), lambda i, j: (0, 0))] * 2
    args += [gamma.reshape(1, n), beta.reshape(1, n)]
    if split_rows is None:
        out_shape = [jax.ShapeDtypeStruct((m, n), F32), jax.ShapeDtypeStruct((m, n), BF16)]
        out_specs = [pl.BlockSpec((tm, n), row), pl.BlockSpec((tm, n), row)]
    else:
        out_shape = [jax.ShapeDtypeStruct((split_rows, n), F32), jax.ShapeDtypeStruct((m - split_rows, n), F32)]
        out_specs = [pl.BlockSpec((tm, n), lambda i, j: (jnp.minimum(i, tiles_a - 1), 0)),
                     pl.BlockSpec((tm, n), lambda i, j: (jnp.maximum(i - tiles_a, 0), 0))]
    return pl.pallas_call(
        body,
        out_shape=out_shape,
        grid=(m // tm, nk),
        in_specs=in_specs,
        out_specs=out_specs,
        scratch_shapes=[pltpu.VMEM((tm, n), F32)],
        compiler_params=_cparams(("arbitrary", "arbitrary")),
        name=name,
    )(*args)


def _seq_masks(g, lg_shift):
    row = lax.broadcasted_iota(jnp.int32, (CHUNK, CHUNK), 0)
    col = lax.broadcasted_iota(jnp.int32, (CHUNK, CHUNK), 1)
    if g == 1:
        return col <= row, col < row
    same = (row >> lg_shift) == (col >> lg_shift)
    return same & (col <= row), same & (col < row)


def _per_seq_rows(vals, g, lg):
    if g == 1:
        return vals[0]
    return jnp.concatenate([jnp.broadcast_to(v, (lg, v.shape[1])) for v in vals], axis=0)


def _last_rows(col, g, lg):
    if g == 1:
        return col[:, CHUNK - 1:, :]
    return _per_seq_rows([col[0, (i + 1) * lg - 1:(i + 1) * lg, :] for i in range(g)], g, lg)[None]


def _split3(x):
    x1 = x.astype(BF16).astype(F32)
    r1 = x - x1
    x2 = r1.astype(BF16).astype(F32)
    x3 = (r1 - x2).astype(BF16).astype(F32)
    return x1, x2, x3


def _split2(x):
    hi = x.astype(BF16)
    return hi, (x - hi.astype(F32)).astype(BF16)


def _lanes3(col, lane_ids, first, other):
    c1, c2, c3 = _split3(col)
    x = jnp.where(lane_ids == first, c1, jnp.where(lane_ids == first + 1, c2,
                                                   jnp.where(lane_ids == first + 2, c3, 0.0)))
    return jnp.where((lane_ids >= other) & (lane_ids < other + 3), 1.0, x).astype(BF16)


def _cumsum_rows(col, tri_b, lane_ids):
    return jnp.sum(_bmm(tri_b, _lanes3(col, lane_ids, 0, LANE)), axis=-1, keepdims=True)


def _outer_sum(a_col, b_col, lane_ids):
    return _bmm_nt(_lanes3(a_col, lane_ids, 0, 3), _lanes3(b_col, lane_ids, 3, 0))


def _bmm2(a, b):
    ah, al = _split2(a)
    bb = b.astype(BF16)
    return _bmm(ah, bb) + _bmm(al, bb)


def _conv_rows(x, prev, w, g, lg):
    f = x.shape[1]
    xx = jnp.concatenate([prev.reshape(g, SUBLANE, f), x.reshape(g, lg, f)], axis=1)
    y = xx[:, SUBLANE:, :] * w[CONV_W - 1:CONV_W, :]
    for s in range(1, CONV_W):
        y = y + xx[:, SUBLANE - s:SUBLANE - s + lg, :] * w[CONV_W - 1 - s:CONV_W - s, :]
    return y.reshape(g * lg, f), xx[:, lg:, :].reshape(g * SUBLANE, f)


def _mlstm_body(r, g, q_ref, k_ref, v_ref, o_ref, gt_ref, gb_ref, nw_ref, c0_ref, n0_ref, m0_ref,
                y_ref, c_out, n_out, m_out, c_s, n_s, m_s):
    lg = CHUNK // g
    lg_shift = int(math.log2(lg))
    nch = r // CHUNK
    h = pl.program_id(1)
    t = pl.program_id(2)

    @pl.when(t == 0)
    def _():
        c_s[...] = c0_ref[:, 0]
        n_s[...] = n0_ref[:, 0]
        m_s[...] = jnp.broadcast_to(m0_ref[:, 0], m_s.shape)

    lane3 = lax.broadcasted_iota(jnp.int32, (nch, CHUNK, LANE), 2)
    causal, _ = _seq_masks(g, lg_shift)
    tri_b = jnp.broadcast_to(causal.astype(BF16), (nch, CHUNK, CHUNK))

    pre = gt_ref[...].reshape(nch, CHUNK, LANE) + gb_ref[...]
    capd = ML_GATE_CAP * jnp.tanh(pre * (1.0 / ML_GATE_CAP))
    logsig = jnp.minimum(capd, 0.0) - jnp.log1p(jnp.exp(-jnp.abs(capd)))
    ic = jnp.sum(jnp.where(lane3 == G_MLI + h, capd, 0.0), axis=2, keepdims=True)
    lf = jnp.sum(jnp.where(lane3 == G_MLF + h, logsig, 0.0), axis=2, keepdims=True)
    bc = _cumsum_rows(lf, tri_b, lane3)
    dm = jnp.where(causal, _outer_sum(bc, ic - bc, lane3), -jnp.inf)
    rmax = jnp.max(dm, axis=2, keepdims=True)

    if g == 1:
        mp = m_s[0][:, 0:1]
        m_prevs = []
        for c in range(nch):
            m_prevs.append(mp.reshape(1, 1, 1))
            mp = jnp.maximum(mp + bc[c, CHUNK - 1:, :], rmax[c, CHUNK - 1:, :])
        m_prev = jnp.concatenate(m_prevs, axis=0) if nch > 1 else m_prevs[0]
        m_s[0] = jnp.broadcast_to(mp, (1, LANE))
    else:
        m_prev = _per_seq_rows([m_s[i][:, 0:1] for i in range(g)], g, lg)[None]
    m_t = jnp.maximum(m_prev + bc, rmax)
    dmat = jnp.exp(dm - m_t)
    q3 = q_ref[...].reshape(nch, CHUNK, ML_DK) * (ML_DK ** -0.5)
    k3 = k_ref[...].reshape(nch, CHUNK, ML_DK)
    v3 = v_ref[...].reshape(nch, CHUNK, ML_DV)
    s = _bmm_nt(q3, k3) * dmat
    inter = jnp.exp(bc + m_prev - m_t)
    bc_end = _last_rows(bc, g, lg)
    m_new = _last_rows(m_t, g, lg)
    w_col = jnp.exp(ic - bc + bc_end - m_new)
    dec = jnp.exp(bc_end + m_prev - m_new)
    sv = _bmm(s, v3)
    ssum = jnp.sum(s, axis=2, keepdims=True)
    vw = v3 * w_col
    kw = k3 * w_col
    floor = jnp.exp(-m_t)

    def emit(rows, hh):
        hn = hh * lax.rsqrt(jnp.mean(hh * hh, axis=1, keepdims=True) + RMS_EPS) * nw_ref[...]
        y_ref[rows, :] = (_sigmoid(o_ref[rows, :]) * hn).astype(y_ref.dtype)

    if g == 1:
        upd = _bmm_tn(vw, k3)
        ksum = jnp.sum(kw, axis=1, keepdims=True)
        cc, nn = c_s[0], n_s[0]
        for c in range(nch):
            num = sv[c] + inter[c] * _dot_nt(q3[c], cc)
            den = ssum[c] + inter[c] * jnp.sum(q3[c] * nn, axis=1, keepdims=True)
            emit(slice(c * CHUNK, (c + 1) * CHUNK), num / jnp.maximum(jnp.abs(den), floor[c]))
            cc = dec[c] * cc + upd[c]
            nn = dec[c] * nn + ksum[c]
        c_s[0] = cc
        n_s[0] = nn
    else:
        rows_i = lax.broadcasted_iota(jnp.int32, (CHUNK, 1), 0)
        q2, k2 = q3[0], k3[0]
        qc = jnp.concatenate([_dot_nt(q2[i * lg:(i + 1) * lg], c_s[i]) for i in range(g)], axis=0)
        qn = jnp.concatenate([jnp.sum(q2[i * lg:(i + 1) * lg] * n_s[i], axis=1, keepdims=True)
                              for i in range(g)], axis=0)
        num = sv[0] + inter[0] * qc
        den = ssum[0] + inter[0] * qn
        emit(slice(0, CHUNK), num / jnp.maximum(jnp.abs(den), floor[0]))
        for i in range(g):
            dec_i = dec[0, i * lg:i * lg + 1, :]
            c_s[i] = dec_i * c_s[i] + _dot_tn(jnp.where((rows_i >> lg_shift) == i, vw[0], 0.0), k2)
            n_s[i] = dec_i * n_s[i] + jnp.sum(kw[0, i * lg:(i + 1) * lg], axis=0, keepdims=True)
            m_s[i] = jnp.broadcast_to(m_new[0, i * lg:i * lg + 1, :], (1, LANE))

    @pl.when(t == pl.num_programs(2) - 1)
    def _():
        c_out[:, 0] = c_s[...]
        n_out[:, 0] = n_s[...]
        m_out[:, 0] = m_s[:, :, 0:1]


def _gdn_body(r, g, q_ref, k_ref, v_ref, z_ref, gt_ref, cwq_ref, cwk_ref, cwv_ref, al_ref, dtb_ref,
              nw_ref, bq_ref, bk_ref, bv_ref, s0_ref, y_ref, s_out, tq_out, tk_out, tv_out,
              s_s, pq_s, pk_s, pv_s):
    lg = CHUNK // g
    lg_shift = int(math.log2(lg))
    lseq = r if g == 1 else lg
    nch = r // CHUNK
    h = pl.program_id(1)
    t = pl.program_id(2)

    @pl.when(t == 0)
    def _():
        s_s[...] = s0_ref[:, 0]
        pq_s[...] = bq_ref[...]
        pk_s[...] = bk_ref[...]
        pv_s[...] = bv_ref[...]

    qa, tq = _conv_rows(q_ref[...], pq_s[...], cwq_ref[...], g, lseq)
    ka, tk = _conv_rows(k_ref[...], pk_s[...], cwk_ref[...], g, lseq)
    va, tv = _conv_rows(v_ref[...], pv_s[...], cwv_ref[...], g, lseq)
    pq_s[...] = tq
    pk_s[...] = tk
    pv_s[...] = tv
    qa, ka, va = _silu(qa), _silu(ka), _silu(va)
    qn = qa * lax.rsqrt(jnp.sum(qa * qa, axis=1, keepdims=True) + L2_EPS) * (GDN_DK ** -0.5)
    kn = ka * lax.rsqrt(jnp.sum(ka * ka, axis=1, keepdims=True) + L2_EPS)

    lane3 = lax.broadcasted_iota(jnp.int32, (nch, CHUNK, LANE), 2)
    causal, strict = _seq_masks(g, lg_shift)
    tri_b = jnp.broadcast_to(causal.astype(BF16), (nch, CHUNK, CHUNK))
    eye = (lax.broadcasted_iota(jnp.int32, (CHUNK, CHUNK), 0)
           == lax.broadcasted_iota(jnp.int32, (CHUNK, CHUNK), 1)).astype(F32)

    q3 = qn.reshape(nch, CHUNK, GDN_DK)
    k3 = kn.reshape(nch, CHUNK, GDN_DK)
    v3 = va.reshape(nch, CHUNK, GDN_DV)
    gt = gt_ref[...].reshape(nch, CHUNK, LANE)
    beta = jnp.sum(jnp.where(lane3 == G_GDB + h, _sigmoid(gt), 0.0), axis=2, keepdims=True)
    gval = -jnp.exp(al_ref[...]) * _softplus(gt + dtb_ref[...])
    gc = _cumsum_rows(jnp.sum(jnp.where(lane3 == G_GDA + h, gval, 0.0), axis=2, keepdims=True),
                      tri_b, lane3)
    decay = jnp.exp(jnp.where(causal, _outer_sum(gc, -gc, lane3), -jnp.inf))
    p = jnp.where(strict, -(beta * _bmm_nt(k3, k3) * decay), 0.0)
    inv = eye + p
    ph, pl_ = _split2(p)
    for _ in range(1, lg_shift):
        ph, pl_ = _split2(_bmm(ph, ph) + (_bmm(ph, pl_) + _bmm(pl_, ph)))
        ih, il = _split2(inv)
        inv = inv + (_bmm(ih, ph) + (_bmm(ih, pl_) + _bmm(il, ph)))
    egc = jnp.exp(gc)
    u_pre = _bmm2(inv, beta * v3)
    w = _bmm2(inv, (beta * egc) * k3)
    qk = _bmm_nt(q3, k3) * decay
    q_dec = q3 * egc
    g_end = _last_rows(gc, g, lg)
    k_dec = k3 * jnp.exp(g_end - gc)

    def emit(rows, o):
        hn = o * lax.rsqrt(jnp.mean(o * o, axis=1, keepdims=True) + RMS_EPS) * nw_ref[...]
        y_ref[rows, :] = (hn * _silu(z_ref[rows, :])).astype(y_ref.dtype)

    if g == 1:
        st = s_s[0]
        for c in range(nch):
            u = u_pre[c] - _dot(w[c], st)
            emit(slice(c * CHUNK, (c + 1) * CHUNK), _dot(q_dec[c], st) + _dot(qk[c], u))
            st = jnp.exp(g_end[c]) * st + _dot_tn(k_dec[c], u)
        s_s[0] = st
    else:
        rows_i = lax.broadcasted_iota(jnp.int32, (CHUNK, 1), 0)
        us, os_ = [], []
        for i in range(g):
            sl = slice(i * lg, (i + 1) * lg)
            us.append(u_pre[0, sl] - _dot(w[0, sl], s_s[i]))
            os_.append(_dot(q_dec[0, sl], s_s[i]))
        u = jnp.concatenate(us, axis=0)
        emit(slice(0, CHUNK), jnp.concatenate(os_, axis=0) + _dot(qk[0], u))
        for i in range(g):
            kdi = jnp.where((rows_i >> lg_shift) == i, k_dec[0], 0.0)
            s_s[i] = jnp.exp(g_end[0, i * lg:i * lg + 1, :]) * s_s[i] + _dot_tn(kdi, u)

    @pl.when(t == pl.num_programs(2) - 1)
    def _():
        s_out[:, 0] = s_s[...]
        tq_out[...] = tq
        tk_out[...] = tk
        tv_out[...] = tv


def _rglru_body(r, g, x_ref, yg_ref, cw_ref, cb_ref, wa_ref, wx_ref, ba_ref, bx_ref, lam_ref,
                buf_ref, h0_ref, y_ref, h_out, tail_out, h_s, px_s):
    lseq = r if g == 1 else CHUNK // g
    t = pl.program_id(2)

    @pl.when(t == 0)
    def _():
        h_s[...] = h0_ref[:, 0]
        px_s[...] = buf_ref[...]

    xc, tail = _conv_rows(x_ref[...], px_s[...], cw_ref[...], g, lseq)
    px_s[...] = tail
    xc = xc + cb_ref[...]
    rg = _sigmoid(_dot(xc, wa_ref[0]) + ba_ref[...])
    ig = _sigmoid(_dot(xc, wx_ref[0]) + bx_ref[...])
    log_a = -LRU_C * rg * _softplus(-lam_ref[...])
    a = jnp.exp(log_a)
    u = jnp.sqrt(1.0 - jnp.exp(2.0 * log_a)) * (ig * xc)
    pos = lax.broadcasted_iota(jnp.int32, (r, 1), 0) & (SUBLANE - 1)
    sh = 1
    while sh < SUBLANE:
        a_sh = pltpu.roll(a, sh, 0)
        u_sh = pltpu.roll(u, sh, 0)
        msk = pos >= sh
        u = jnp.where(msk, a * u_sh + u, u)
        a = jnp.where(msk, a * a_sh, a)
        sh *= 2
    blocks = []
    for i in range(g):
        carry = h_s[i]
        for k in range(lseq // SUBLANE):
            rows = slice(i * lseq + k * SUBLANE, i * lseq + (k + 1) * SUBLANE)
            blk = u[rows] + a[rows] * carry
            blocks.append(blk)
            carry = blk[SUBLANE - 1:, :]
        h_s[i] = carry
    hh = jnp.concatenate(blocks, axis=0)
    yg = yg_ref[...]
    gelu = 0.5 * yg * (1.0 + jnp.tanh(math.sqrt(2.0 / math.pi) * (yg + 0.044715 * (yg * yg * yg))))
    y_ref[...] = (gelu * hh).astype(y_ref.dtype)

    @pl.when(t == pl.num_programs(2) - 1)
    def _():
        h_out[:, 0] = h_s[...]
        tail_out[...] = tail


class _Group:
    def __init__(self, n_seq, seq_len, row0, long_chunks=4):
        self.n_seq, self.seq_len, self.row0 = n_seq, seq_len, row0
        self.rows = n_seq * seq_len
        if seq_len >= long_chunks * CHUNK:
            self.r, self.g = long_chunks * CHUNK, 1
        elif seq_len >= CHUNK:
            self.r, self.g = CHUNK, 1
        else:
            self.r, self.g = CHUNK, CHUNK // seq_len
        self.s = self.g
        self.nb = n_seq // self.s
        self.nt = seq_len // self.r if self.g == 1 else 1
        assert seq_len >= CONV_W - 1 and row0 % self.r == 0 and self.rows % self.r == 0
        assert n_seq % self.s == 0 and self.r & (self.r - 1) == 0
        self.rb0 = row0 // self.r

    def rb(self, b, t):
        return self.rb0 + b * self.nt + t


def _mixer_call(body, grp, l, depth, m_total, in_specs, args, y_width, y_block, state_shapes, extra_outs,
                scratch, prev, name):
    s = grp.s
    out_shape = [jax.ShapeDtypeStruct((m_total, y_width), BF16)]
    out_specs = [pl.BlockSpec((grp.r, y_block), lambda b, h, t: (grp.rb(b, t), h))]
    for tail in state_shapes:
        out_shape.append(jax.ShapeDtypeStruct((depth, grp.n_seq, y_width // y_block) + tail, F32))
        out_specs.append(pl.BlockSpec((None, s, 1) + tail, lambda b, h, t: (l, b, h) + (0,) * len(tail)))
    for shp, spec in extra_outs:
        out_shape.append(shp)
        out_specs.append(spec)
    n_in = len(args)
    aliases = {}
    in_specs = list(in_specs)
    args = list(args)
    for j, arr in enumerate(prev):
        if arr is not None:
            aliases[len(args)] = j
            in_specs.append(pl.BlockSpec(memory_space=pl.ANY))
            args.append(arr)
    n_alias = len(args) - n_in

    def wrapped(*refs):
        body(*refs[:n_in], *refs[n_in + n_alias:])

    return pl.pallas_call(
        wrapped,
        out_shape=out_shape,
        grid=(grp.nb, y_width // y_block, grp.nt),
        in_specs=in_specs,
        out_specs=out_specs,
        scratch_shapes=scratch,
        input_output_aliases=aliases,
        compiler_params=_cparams(("arbitrary",) * 3),
        name=name,
    )(*args)


def _mlstm_call(grp, l, depth, m_total, hmain, gates, gate_bias, norm_w, c0, n0, m0, prev):
    r, s = grp.r, grp.s
    row = lambda off, w: pl.BlockSpec((r, w), lambda b, h, t: (grp.rb(b, t), off // w + h))
    st = lambda *tail: pl.BlockSpec((None, s, 1) + tail, lambda b, h, t: (l, b, h, 0, 0))
    in_specs = [row(C_MLQ, ML_DK), row(C_MLK, ML_DK), row(C_MLV, ML_DV), row(C_MLO, ML_DV),
                pl.BlockSpec((r, LANE), lambda b, h, t: (grp.rb(b, t), 0)),
                pl.BlockSpec((1, LANE), lambda b, h, t: (0, 0)),
                pl.BlockSpec((1, ML_DV), lambda b, h, t: (0, h)),
                st(ML_DV, ML_DK), st(1, ML_DK), st(1, 1)]
    scratch = [pltpu.VMEM((s, ML_DV, ML_DK), F32), pltpu.VMEM((s, 1, ML_DK), F32),
               pltpu.VMEM((s, 1, LANE), F32)]
    return _mixer_call(functools.partial(_mlstm_body, r, grp.g), grp, l, depth, m_total, in_specs,
                       [hmain, hmain, hmain, hmain, gates, gate_bias, norm_w, c0, n0, m0],
                       ML_HEADS * ML_DV, ML_DV, [(ML_DV, ML_DK), (1, ML_DK), (1, 1)], [], scratch, prev, "mlstm")


def _gdn_call(grp, l, depth, m_total, hmain, gates, conv_w, a_log_row, dt_bias_row, norm_w, bufpad, s0, prev):
    r, s = grp.r, grp.s
    row = lambda off, w: pl.BlockSpec((r, w), lambda b, h, t: (grp.rb(b, t), off // w + h))
    cw = lambda off, w: pl.BlockSpec((CONV_W, w), lambda b, h, t: (0, off // w + h))
    bf = lambda off, w: pl.BlockSpec((s * SUBLANE, w), lambda b, h, t: (b, off // w + h))
    one = pl.BlockSpec((1, LANE), lambda b, h, t: (0, 0))
    in_specs = [row(C_GDQ, GDN_DK), row(C_GDK, GDN_DK), row(C_GDV, GDN_DV), row(C_GDZ, GDN_DV),
                pl.BlockSpec((r, LANE), lambda b, h, t: (grp.rb(b, t), 1)),
                cw(0, GDN_DK), cw(GDN_QK, GDN_DK), cw(2 * GDN_QK, GDN_DV),
                one, one, pl.BlockSpec((1, GDN_DV), lambda b, h, t: (0, 0)),
                bf(0, GDN_DK), bf(GDN_QK, GDN_DK), bf(2 * GDN_QK, GDN_DV),
                pl.BlockSpec((None, s, 1, GDN_DK, GDN_DV), lambda b, h, t: (l, b, h, 0, 0))]
    nrow = grp.n_seq * SUBLANE
    tail = lambda w: (jax.ShapeDtypeStruct((nrow, GDN_HEADS * w), F32),
                      pl.BlockSpec((s * SUBLANE, w), lambda b, h, t: (b, h)))
    scratch = [pltpu.VMEM((s, GDN_DK, GDN_DV), F32),
               pltpu.VMEM((s * SUBLANE, GDN_DK), F32), pltpu.VMEM((s * SUBLANE, GDN_DK), F32),
               pltpu.VMEM((s * SUBLANE, GDN_DV), F32)]
    return _mixer_call(functools.partial(_gdn_body, r, grp.g), grp, l, depth, m_total, in_specs,
                       [hmain, hmain, hmain, hmain, gates, conv_w, conv_w, conv_w, a_log_row, dt_bias_row,
                        norm_w, bufpad, bufpad, bufpad, s0],
                       GDN_HEADS * GDN_DV, GDN_DV, [(GDN_DK, GDN_DV)],
                       [tail(GDN_DK), tail(GDN_DK), tail(GDN_DV)], scratch, prev, "gdn")


def _rglru_call(grp, l, depth, m_total, hmain, conv_w, conv_b, w_a, w_x, b_a, b_x, lam, bufpad, h0, prev):
    r, s = grp.r, grp.s
    w = LRU_BLOCK
    row = lambda off: pl.BlockSpec((r, w), lambda b, h, t: (grp.rb(b, t), off // w + h))
    vec = pl.BlockSpec((1, w), lambda b, h, t: (0, h))
    blk = pl.BlockSpec((1, w, w), lambda b, h, t: (h, 0, 0))
    in_specs = [row(C_RGX), row(C_RGY),
                pl.BlockSpec((CONV_W, w), lambda b, h, t: (0, h)), vec, blk, blk, vec, vec, vec,
                pl.BlockSpec((s * SUBLANE, w), lambda b, h, t: (b, h)),
                pl.BlockSpec((None, s, 1, 1, w), lambda b, h, t: (l, b, h, 0, 0))]
    tail = (jax.ShapeDtypeStruct((grp.n_seq * SUBLANE, LRU_WIDTH), F32),
            pl.BlockSpec((s * SUBLANE, w), lambda b, h, t: (b, h)))
    scratch = [pltpu.VMEM((s, 1, w), F32), pltpu.VMEM((s * SUBLANE, w), F32)]
    return _mixer_call(functools.partial(_rglru_body, r, grp.g), grp, l, depth, m_total, in_specs,
                       [hmain, hmain, conv_w, conv_b, w_a, w_x, b_a, b_x, lam, bufpad, h0],
                       LRU_WIDTH, w, [(1, w)], [tail], scratch, prev, "rglru")


def _pad_conv_state(buf):
    b, k, c = buf.shape
    return jnp.pad(buf.astype(F32), ((0, 0), (SUBLANE - k, 0), (0, 0))).reshape(b * SUBLANE, c)


def _unpad_conv_state(tail, n_seq):
    return tail.reshape(n_seq, SUBLANE, -1)[:, SUBLANE - (CONV_W - 1):, :]


R_I1, R_I2, R_W1, R_W2, R_P1, R_P2 = 0, 1, 2, 3, 4, 5
MOE_TM = 512
MOE_TD = 512


def _router_call(x, router, tm):
    m, d = x.shape
    n_exp = router.shape[1]
    wr = jnp.pad(router.astype(F32), ((0, 0), (0, LANE - n_exp)))

    def body(x_ref, w_ref, o_ref, cnt_ref, carry):
        i = pl.program_id(0)

        @pl.when(i == 0)
        def _():
            carry[...] = jnp.zeros_like(carry)

        xh, xl = _split2(x_ref[...])
        wh, wl = _split2(w_ref[...])
        logits = _dot(xh, wh) + (_dot(xh, wl) + _dot(xl, wh))
        lane = lax.broadcasted_iota(jnp.int32, logits.shape, 1)
        lg = jnp.where(lane < n_exp, logits, -jnp.inf)
        v1 = jnp.max(lg, axis=1, keepdims=True)
        i1 = jnp.min(jnp.where(lg == v1, lane, LANE), axis=1, keepdims=True)
        lg2 = jnp.where(lane == i1, -jnp.inf, lg)
        v2 = jnp.max(lg2, axis=1, keepdims=True)
        i2 = jnp.min(jnp.where(lg2 == v2, lane, LANE), axis=1, keepdims=True)
        e2 = jnp.exp(v2 - v1)
        w1 = 1.0 / (1.0 + e2)
        w2 = e2 / (1.0 + e2)
        sel = ((lane == i1) | (lane == i2)).astype(F32)
        below = (lax.broadcasted_iota(jnp.int32, (tm, tm), 1)
                 < lax.broadcasted_iota(jnp.int32, (tm, tm), 0)).astype(BF16)
        rank = jnp.dot(below, sel.astype(BF16), preferred_element_type=F32) + carry[0:1, :]
        p1 = jnp.sum(jnp.where(lane == i1, rank, 0.0), axis=1, keepdims=True)
        p2 = jnp.sum(jnp.where(lane == i2, rank, 0.0), axis=1, keepdims=True)
        rec = jnp.where(lane == R_I1, i1.astype(F32), jnp.where(lane == R_I2, i2.astype(F32), 0.0))
        rec = jnp.where(lane == R_W1, w1, jnp.where(lane == R_W2, w2, rec))
        o_ref[...] = jnp.where(lane == R_P1, p1, jnp.where(lane == R_P2, p2, rec))
        carry[...] = carry[...] + jnp.sum(sel, axis=0, keepdims=True)
        cnt_ref[...] = carry[...]

    return pl.pallas_call(
        body,
        out_shape=[jax.ShapeDtypeStruct((m, LANE), F32), jax.ShapeDtypeStruct((SUBLANE, LANE), F32)],
        grid=(m // tm,),
        in_specs=[pl.BlockSpec((tm, d), lambda i: (i, 0)), pl.BlockSpec((d, LANE), lambda i: (0, 0))],
        out_specs=[pl.BlockSpec((tm, LANE), lambda i: (i, 0)), pl.BlockSpec((SUBLANE, LANE), lambda i: (0, 0))],
        scratch_shapes=[pltpu.VMEM((SUBLANE, LANE), F32)],
        compiler_params=_cparams(("arbitrary",)),
        name="router",
    )(x, wr)


def _source_table(starts, e_arr, p_arr, m, ns):
    td = MOE_TD

    def body(start_ref, e_ref, p_ref, dst_ref, tbl_ref):
        i = pl.program_id(0)

        @pl.when(i == 0)
        def _():
            def clear(r, c):
                tbl_ref[r] = 0
                return c

            lax.fori_loop(0, ns, clear, 0, unroll=8)

        base = i * td

        def put(j, c):
            row = start_ref[e_ref[0, j]] + p_ref[0, j]
            dst_ref[0, j] = row
            tbl_ref[row] = base + (j >> 1)
            return c

        lax.fori_loop(0, 2 * td, put, 0, unroll=8)

    smem = pl.BlockSpec((None, 1, 2 * td), lambda i, s: (i, 0, 0), memory_space=pltpu.SMEM)
    return pl.pallas_call(
        body,
        grid_spec=pltpu.PrefetchScalarGridSpec(
            num_scalar_prefetch=1, grid=(m // td,), in_specs=[smem, smem],
            out_specs=[smem, pl.BlockSpec(memory_space=pltpu.SMEM)]),
        out_shape=[jax.ShapeDtypeStruct(e_arr.shape, jnp.int32), jax.ShapeDtypeStruct((ns,), jnp.int32)],
        compiler_params=_cparams(("arbitrary",)),
        name="moe_table",
    )(starts, e_arr, p_arr)


def _start_rows(copy_of, n_rows):
    def issue(r, c):
        copy_of(2 * r).start(priority=0)
        copy_of(2 * r + 1).start(priority=1)
        return c

    lax.fori_loop(0, n_rows // 2, issue, 0, unroll=4)


def _wait_rows(src_ref, dst_ref, sem):
    pltpu.make_async_copy(src_ref.at[pl.ds(0, dst_ref.shape[0])], dst_ref, sem).wait()


def _dispatch(x, table, n_used):
    m, d = x.shape
    ns = table.shape[0]
    tm = MOE_TM
    nt = ns // tm

    def body(nu_ref, src_ref, nxt_ref, x_ref, o_ref, buf, sem):
        i = pl.program_id(0)
        slot = i & 1

        def rows(idx_ref, s):
            return lambda r: pltpu.make_async_copy(x_ref.at[pl.ds(idx_ref[0, r], 1)],
                                                   buf.at[s, pl.ds(r, 1)], sem.at[s])

        @pl.when(i == 0)
        def _():
            _start_rows(rows(src_ref, 0), tm)

        @pl.when(i + 1 < nu_ref[0])
        def _():
            _start_rows(rows(nxt_ref, 1 - slot), tm)

        @pl.when(i < nu_ref[0])
        def _():
            _wait_rows(x_ref, buf.at[slot], sem.at[slot])
            o_ref[...] = buf[slot].astype(BF16)

    idx = lambda off: pl.BlockSpec((None, 1, tm), lambda i, nu: (jnp.minimum(i + off, nt - 1), 0, 0),
                                   memory_space=pltpu.SMEM)
    tbl = table.reshape(nt, 1, tm)
    return pl.pallas_call(
        body,
        grid_spec=pltpu.PrefetchScalarGridSpec(
            num_scalar_prefetch=1, grid=(nt,),
            in_specs=[idx(0), idx(1), pl.BlockSpec(memory_space=pl.ANY)],
            out_specs=pl.BlockSpec((tm, d), lambda i, nu: (jnp.minimum(i, nu[0] - 1), 0)),
            scratch_shapes=[pltpu.VMEM((2, tm, d), F32), pltpu.SemaphoreType.DMA((2,))]),
        out_shape=jax.ShapeDtypeStruct((ns, d), BF16),
        compiler_params=_cparams(("arbitrary",)),
        name="moe_dispatch",
    )(n_used, tbl, tbl, x)


def _grouped(rows, ws, tile_expert, tile_first, n_used, epilogue, out_dtype, n_cols, tn, name, w_buffers=2):
    ns, k = rows.shape
    nt = ns // MOE_TM
    used = lambda i, nu: jnp.minimum(i, nu[0] - 1)
    w_spec = pl.BlockSpec((None, k, tn), lambda n, i, te, tf, nu: (te[i], 0, n),
                          pipeline_mode=pl.Buffered(w_buffers))

    def body(te_ref, tf_ref, nu_ref, x_ref, *refs):
        w_refs, o_ref, wb_refs = refs[:len(ws)], refs[len(ws)], refs[len(ws) + 1:]
        i = pl.program_id(1)

        @pl.when(tf_ref[i] == 1)
        def _():
            for w, wb in zip(w_refs, wb_refs):
                wb[...] = w[...].astype(BF16)

        @pl.when(i < nu_ref[0])
        def _():
            xv = x_ref[...].astype(BF16)
            accs = [jnp.dot(xv, wb[...], preferred_element_type=F32) for wb in wb_refs]
            o_ref[...] = epilogue(accs).astype(o_ref.dtype)

    return pl.pallas_call(
        body,
        grid_spec=pltpu.PrefetchScalarGridSpec(
            num_scalar_prefetch=3, grid=(n_cols // tn, nt),
            in_specs=[pl.BlockSpec((MOE_TM, k), lambda n, i, te, tf, nu: (used(i, nu), 0))] + [w_spec] * len(ws),
            out_specs=pl.BlockSpec((MOE_TM, tn), lambda n, i, te, tf, nu: (used(i, nu), n)),
            scratch_shapes=[pltpu.VMEM((k, tn), BF16) for _ in ws]),
        out_shape=jax.ShapeDtypeStruct((ns, n_cols), out_dtype),
        compiler_params=_cparams(("arbitrary", "arbitrary")),
        name=name,
    )(tile_expert, tile_first, n_used, rows, *ws)


def _combine_ln(x, ys, dst_rows, rec, ple, gamma, beta, split_rows):
    tm = MOE_TD
    m, n = x.shape
    n_tiles = m // tm
    tiles_a = None if split_rows is None else split_rows // tm

    def body(ds_ref, dn_ref, ys_ref, x_ref, r_ref, e_ref, g_ref, b_ref, oa_ref, ob_ref, ybuf, sem):
        i = pl.program_id(0)
        slot = i & 1

        def rows(d_ref, s):
            return lambda j: pltpu.make_async_copy(ys_ref.at[pl.ds(d_ref[0, j], 1)],
                                                   ybuf.at[s, j & 1, pl.ds(j >> 1, 1)], sem.at[s])

        @pl.when(i == 0)
        def _():
            _start_rows(rows(ds_ref, 0), 2 * tm)

        @pl.when(i + 1 < n_tiles)
        def _():
            _start_rows(rows(dn_ref, 1 - slot), 2 * tm)

        for k in range(2):
            _wait_rows(ys_ref, ybuf.at[slot, k], sem.at[slot])
        r = r_ref[...]
        lane = lax.broadcasted_iota(jnp.int32, r.shape, 1)
        w1 = jnp.sum(jnp.where(lane == R_W1, r, 0.0), axis=1, keepdims=True)
        w2 = jnp.sum(jnp.where(lane == R_W2, r, 0.0), axis=1, keepdims=True)
        y = DEEPNORM_ALPHA * x_ref[...] + (w1 * ybuf[slot, 0] + w2 * ybuf[slot, 1]) + e_ref[...]
        mu = jnp.mean(y, axis=-1, keepdims=True)
        yc = y - mu
        var = jnp.mean(yc * yc, axis=-1, keepdims=True)
        out = yc * lax.rsqrt(var + LN_EPS) * g_ref[...] + b_ref[...]
        if split_rows is None:
            oa_ref[...] = out
            ob_ref[...] = out.astype(BF16)
        else:
            @pl.when(i < tiles_a)
            def _():
                oa_ref[...] = out

            @pl.when(i >= tiles_a)
            def _():
                ob_ref[...] = out

    row = lambda i: (i, 0)
    vec = pl.BlockSpec((1, n), lambda i: (0, 0))
    smem = lambda off: pl.BlockSpec((None, 1, 2 * tm), lambda i: (jnp.minimum(i + off, n_tiles - 1), 0, 0),
                                    memory_space=pltpu.SMEM)
    if split_rows is None:
        out_shape = [jax.ShapeDtypeStruct((m, n), F32), jax.ShapeDtypeStruct((m, n), BF16)]
        out_specs = [pl.BlockSpec((tm, n), row), pl.BlockSpec((tm, n), row)]
    else:
        out_shape = [jax.ShapeDtypeStruct((split_rows, n), F32), jax.ShapeDtypeStruct((m - split_rows, n), F32)]
        out_specs = [pl.BlockSpec((tm, n), lambda i: (jnp.minimum(i, tiles_a - 1), 0)),
                     pl.BlockSpec((tm, n), lambda i: (jnp.maximum(i - tiles_a, 0), 0))]
    return pl.pallas_call(
        body,
        grid=(n_tiles,),
        in_specs=[smem(0), smem(1), pl.BlockSpec(memory_space=pl.ANY),
                  pl.BlockSpec((tm, n), row), pl.BlockSpec((tm, LANE), row), pl.BlockSpec((tm, n), row), vec, vec],
        out_specs=out_specs,
        scratch_shapes=[pltpu.VMEM((2, 2, tm, n), F32), pltpu.SemaphoreType.DMA((2,))],
        out_shape=out_shape,
        compiler_params=_cparams(("arbitrary",)),
        name="moe_combine_ln",
    )(dst_rows, dst_rows, ys, x, rec, ple, gamma.reshape(1, n), beta.reshape(1, n))


def _moe_ln(x, ple, router, w1, w3, w2, j, gamma, beta, split_rows):
    m, d = x.shape
    _, ne, _, fe = w1.shape
    w1, w3, w2 = (w.reshape((-1,) + w.shape[2:]) for w in (w1, w3, w2))
    rec, counts = _router_call(x, router, 512)
    cnt = counts[0, :ne].astype(jnp.int32)
    tiles_e = (cnt + MOE_TM - 1) // MOE_TM
    tile_end = jnp.cumsum(tiles_e)
    tile_beg = tile_end - tiles_e
    n_used = tile_end[ne - 1]
    nt = (2 * m) // MOE_TM + ne
    ids = jnp.minimum(jnp.arange(nt, dtype=jnp.int32), n_used - 1)
    tile_expert = jnp.sum((ids[:, None] >= tile_end[None, :]).astype(jnp.int32), axis=1)
    tile_first = (jnp.arange(nt, dtype=jnp.int32) == tile_beg[tile_expert]).astype(jnp.int32)
    starts = (tile_beg * MOE_TM).astype(jnp.int32)
    slots = lambda a, b: jnp.stack([rec[:, a], rec[:, b]], axis=1).astype(jnp.int32).reshape(m // MOE_TD, 1, 2 * MOE_TD)
    e_arr, p_arr = slots(R_I1, R_I2), slots(R_P1, R_P2)
    ns = nt * MOE_TM

    nu = n_used.reshape(1)
    dst_rows, table = _source_table(starts, e_arr, p_arr, m, ns)
    xs = _dispatch(x, table, nu)
    tile_w = tile_expert + j * ne
    hmid = _grouped(xs, [w1, w3], tile_w, tile_first, nu, lambda a: _silu(a[0]) * a[1], BF16, fe, fe // 2,
                    "moe_up", w_buffers=1)
    ys = _grouped(hmid, [w2], tile_w, tile_first, nu, lambda a: a[0], F32, d, 1024, "moe_down")
    return _combine_ln(x, ys, dst_rows, rec, ple, gamma, beta, split_rows)


def _run_group_mixers(grp, l, depth, m_total, hmain, gates, st, prm, prev):
    b = grp.n_seq
    ya, ml_c, ml_n, ml_m = _mlstm_call(
        grp, l, depth, m_total, hmain, gates, prm["gate_bias"], prm["ml_norm_w"],
        st["ml_C"].astype(F32), st["ml_n"].astype(F32).reshape(depth, b, ML_HEADS, 1, ML_DK),
        st["ml_m"].astype(F32).reshape(depth, b, ML_HEADS, 1, 1),
        [prev["ya"], prev["ml_C"], prev["ml_n"], prev["ml_m"]])
    yb, rg_h, rg_tail = _rglru_call(
        grp, l, depth, m_total, hmain, prm["rg_conv_w"], prm["rg_conv_b"], prm["rg_w_a"], prm["rg_w_x"],
        prm["rg_b_a"], prm["rg_b_x"], prm["rg_lambda"], _pad_conv_state(st["rg_conv"][l]),
        st["rg_h"].astype(F32).reshape(depth, b, LRU_BLOCKS, 1, LRU_BLOCK), [prev["yb"], prev["rg_h"]])
    yc, gd_s, tq, tk, tv = _gdn_call(
        _Group(b, grp.seq_len, grp.row0, long_chunks=8), l, depth, m_total, hmain, gates,
        prm["gd_conv_w"], prm["gd_a_log"], prm["gd_dt_bias"],
        prm["gd_norm_w"], _pad_conv_state(st["gd_conv"][l]), st["gd_S"].astype(F32),
        [prev["yc"], prev["gd_S"]])
    out = dict(ya=ya, yb=yb, yc=yc, ml_C=ml_c, ml_n=ml_n, ml_m=ml_m, rg_h=rg_h, gd_S=gd_s)
    tails = dict(rg_conv=_unpad_conv_state(rg_tail, b),
                 gd_conv=jnp.concatenate([_unpad_conv_state(x, b) for x in (tq, tk, tv)], axis=-1))
    return out, tails


_STATE_KEYS = ("ml_C", "ml_n", "ml_m", "rg_h", "gd_S")


def kernel(x_prompt, x_sample, state_mlstm_C, state_mlstm_n, state_mlstm_m, state_rglru_h, state_rglru_conv, state_gdn_S, state_gdn_conv, p_prompt, p_sample, w_in, ml_b_i, ml_b_f, ml_norm_w, rg_conv_w, rg_conv_b, rg_w_a, rg_b_a, rg_w_x, rg_b_x, rg_lambda, gd_conv_w, gd_A_log, gd_dt_bias, gd_norm_w, w_up_mlstm, w_up_rglru, w_up_gdn, w_out, ln1_g, ln1_b, ffn_w1, ffn_w3, ffn_w2, moe_router, moe_w1, moe_w3, moe_w2, ple_w, ple_gate_w, ln2_g, ln2_b):
    bp, tp, d = x_prompt.shape
    bs, ts, _ = x_sample.shape
    depth = w_in.shape[0]
    gp = _Group(bp, tp, 0)
    gs = _Group(bs, ts, bp * tp)
    m = gp.rows + gs.rows
    pd = x_prompt.dtype

    zeros = lambda *shape: jnp.zeros(shape, F32)
    st_p = dict(ml_C=zeros(depth, bp, ML_HEADS, ML_DV, ML_DK), ml_n=zeros(depth, bp, ML_HEADS, ML_DK),
                ml_m=zeros(depth, bp, ML_HEADS), rg_h=zeros(depth, bp, LRU_WIDTH),
                rg_conv=zeros(depth, bp, CONV_W - 1, LRU_WIDTH),
                gd_S=zeros(depth, bp, GDN_HEADS, GDN_DK, GDN_DV),
                gd_conv=zeros(depth, bp, CONV_W - 1, GDN_CONV_DIM))
    st_s = dict(ml_C=state_mlstm_C, ml_n=state_mlstm_n, ml_m=state_mlstm_m, rg_h=state_rglru_h,
                rg_conv=state_rglru_conv, gd_S=state_gdn_S, gd_conv=state_gdn_conv)

    x = jnp.concatenate([x_prompt.reshape(gp.rows, d), x_sample.reshape(gs.rows, d)], axis=0).astype(F32)
    xb = x.astype(BF16)
    w_in_t = jnp.swapaxes(w_in, 1, 2)
    lane_row = lambda v, off: jnp.zeros((1, LANE), F32).at[0, off:off + v.shape[0]].set(v.astype(F32))
    acc_p = {k: None for k in _STATE_KEYS}
    acc_s = {k: None for k in _STATE_KEYS}
    tails_p, tails_s = [], []

    for l in range(depth):
        hmain = _in_proj(xb, w_in_t, l, 1024)
        gates = _gate_proj(xb, w_in_t, l, 1024)

        prm = dict(
            gate_bias=lane_row(ml_b_i[l], G_MLI) + lane_row(ml_b_f[l], G_MLF),
            ml_norm_w=ml_norm_w[l].astype(F32).reshape(1, -1),
            rg_conv_w=rg_conv_w[l].astype(F32), rg_conv_b=rg_conv_b[l].astype(F32).reshape(1, -1),
            rg_w_a=rg_w_a[l], rg_w_x=rg_w_x[l], rg_b_a=rg_b_a[l].astype(F32).reshape(1, -1),
            rg_b_x=rg_b_x[l].astype(F32).reshape(1, -1), rg_lambda=rg_lambda[l].astype(F32).reshape(1, -1),
            gd_conv_w=gd_conv_w[l].astype(F32), gd_a_log=lane_row(gd_A_log[l], G_GDA),
            gd_dt_bias=lane_row(gd_dt_bias[l], G_GDA), gd_norm_w=gd_norm_w[l].astype(F32).reshape(1, -1))
        out_p, tl_p = _run_group_mixers(gp, l, depth, m, hmain, gates, st_p, prm,
                                        dict(acc_p, ya=None, yb=None, yc=None))
        out_s, tl_s = _run_group_mixers(gs, l, depth, m, hmain, gates, st_s, prm,
                                        dict(acc_s, ya=out_p["ya"], yb=out_p["yb"], yc=out_p["yc"]))
        acc_p = {k: out_p[k] for k in _STATE_KEYS}
        acc_s = {k: out_s[k] for k in _STATE_KEYS}
        tails_p.append(tl_p)
        tails_s.append(tl_s)
        ya, yb, yc = out_s["ya"], out_s["yb"], out_s["yc"]

        tn, tm = 1024, 512
        up_spec = pl.BlockSpec((None, ya.shape[1], tn), lambda n, i: (l, 0, n))
        mg_spec = lambda br: pl.BlockSpec((tm, tn), lambda n, i: (i, (C_MG + br * d) // tn + n))
        merged = _proj([ya, yb, yc], [w_up_mlstm, w_up_rglru, w_up_gdn], [up_spec] * 3,
                       [hmain] * 3, [mg_spec(0), mg_spec(1), mg_spec(2)],
                       lambda accs, ex, ids: [_sigmoid(ex[0]) * accs[0] + _sigmoid(ex[1]) * accs[1]
                                              + _sigmoid(ex[2]) * accs[2]],
                       [jax.ShapeDtypeStruct((m, d), BF16)], d, tm, tn, "merge")[0]
        x, xb = _rowfull_ln(merged, w_out[l].astype(BF16), x, None, ln1_g[l].astype(F32), ln1_b[l].astype(F32),
                            512, d, "out_proj_ln")

        p_l = jnp.concatenate([p_prompt[l].reshape(gp.rows, -1), p_sample[l].reshape(gs.rows, -1)], axis=0)
        ple = _proj([xb, p_l], [ple_gate_w, ple_w],
                    [pl.BlockSpec((None, d, 1024), lambda n, i: (l, 0, n)),
                     pl.BlockSpec((None, P_DIM, 1024), lambda n, i: (l, 0, n))],
                    [], [], lambda accs, ex, ids: [_sigmoid(accs[0]) * accs[1]],
                    [jax.ShapeDtypeStruct((m, d), F32)], d, 1024, 1024, "ple")[0]
        j = l // 2
        split = gp.rows if l == depth - 1 else None
        g2, b2 = ln2_g[l].astype(F32), ln2_b[l].astype(F32)
        if l % 2 == 0:
            ff = ffn_w1.shape[2]
            hmid = _proj([xb], [ffn_w1, ffn_w3], [pl.BlockSpec((None, d, 512), lambda n, i: (j, 0, n))] * 2,
                         [], [], lambda accs, ex, ids: [_silu(accs[0]) * accs[1]],
                         [jax.ShapeDtypeStruct((m, ff), BF16)], ff, 1024, 512, "ffn_up", x_of_w=(0, 0))[0]
            x, xb = _rowfull_ln(hmid, ffn_w2[j].astype(BF16), x, ple, g2, b2, 256, ff, "ffn_down_ln",
                                split_rows=split)
        else:
            x, xb = _moe_ln(x, ple, moe_router[j], moe_w1, moe_w3, moe_w2, j, g2, b2, split)

    y_prompt = x.reshape(bp, tp, d).astype(pd)
    y_sample = xb.reshape(bs, ts, d).astype(x_sample.dtype)

    def finish(acc, tails, n_seq, dtypes):
        conv = {k: jnp.stack([tl[k] for tl in tails]) for k in ("rg_conv", "gd_conv")}
        vals = dict(ml_C=acc["ml_C"], ml_n=acc["ml_n"].reshape(depth, n_seq, ML_HEADS, ML_DK),
                    ml_m=acc["ml_m"].reshape(depth, n_seq, ML_HEADS),
                    rg_h=acc["rg_h"].reshape(depth, n_seq, LRU_WIDTH), rg_conv=conv["rg_conv"],
                    gd_S=acc["gd_S"], gd_conv=conv["gd_conv"])
        names = ("ml_C", "ml_n", "ml_m", "rg_h", "rg_conv", "gd_S", "gd_conv")
        return tuple(vals[k].astype(dtypes[k]) for k in names)

    return ((y_prompt, y_sample) + finish(acc_p, tails_p, bp, {k: pd for k in st_p})
            + finish(acc_s, tails_s, bs, {k: v.dtype for k, v in st_s.items()}))
```
